```python
import jax, jax.numpy as jnp
from jax import lax
import numpy as np

D_MODEL = 1024
BATCH = 2
SEQ = 8192
DEPTH = 2

CHUNK = 64
N_META = 16
D_MIX = D_MODEL
HEAD_DIM = 64
N_Q_HEADS = 8
N_KV_HEADS = 2
Q_PER_KV = N_Q_HEADS // N_KV_HEADS
D_ATTN = N_Q_HEADS * HEAD_DIM
D_KV = N_KV_HEADS * HEAD_DIM
WINDOW = 128
WINDOW_CHUNKS = WINDOW // CHUNK
POOL_SIZES = (2, 4, 8, 16)
N_POOL_GROUPS = len(POOL_SIZES)
D_POOL = D_MIX - D_ATTN
POOL_GROUP = D_POOL // N_POOL_GROUPS
D_IN_PROJ = D_ATTN + 2 * D_KV + D_POOL
D_FF = D_MODEL * 7 // 2
N_EXPERTS = 8
TOP_K = 2
N_DENSE = (DEPTH + 1) // 2
N_MOE = DEPTH // 2
EPS = 1e-5
NEG_INF = -1e30

kernel_name = "hymba_swa_sink_pool_moe_trunk"


def rmsnorm(x, g):
    x32 = x.astype(jnp.float32)
    y = x32 * lax.rsqrt(jnp.mean(x32 * x32, axis=-1, keepdims=True) + EPS)
    return (y * g.astype(jnp.float32)).astype(x.dtype)


def sink_softmax(s, sink):
    sink_b = jnp.broadcast_to(sink, s.shape[:-1] + (1,))
    p = jax.nn.softmax(jnp.concatenate([s, sink_b], axis=-1), axis=-1)
    return p[..., :-1]


def chunk_window_attention(q, k, v, sinks):
    b, l, _ = q.shape
    n_chunks = (l - N_META) // CHUNK
    q = q.reshape(b, l, N_KV_HEADS, Q_PER_KV, HEAD_DIM) * (HEAD_DIM ** -0.5)
    k = k.reshape(b, l, N_KV_HEADS, HEAD_DIM)
    v = v.reshape(b, l, N_KV_HEADS, HEAD_DIM)
    sink = sinks.astype(jnp.float32).reshape(N_KV_HEADS, Q_PER_KV)

    qm, km, vm = q[:, :N_META], k[:, :N_META], v[:, :N_META]
    s_meta = jnp.einsum('bqhgd,bkhd->bhgqk', qm, km).astype(jnp.float32)
    p_meta = sink_softmax(s_meta, sink[None, :, :, None, None])
    o_meta = jnp.einsum('bhgqk,bkhd->bqhgd', p_meta.astype(v.dtype), vm)
    o_meta = o_meta.reshape(b, N_META, D_ATTN)

    def to_chunks(t):
        t = jnp.pad(t[:, N_META:], ((0, 0), (WINDOW_CHUNKS * CHUNK, 0), (0, 0), (0, 0)))
        return t.reshape(b, n_chunks + WINDOW_CHUNKS, CHUNK, N_KV_HEADS, HEAD_DIM)

    def banded(t, t_meta):
        tc = to_chunks(t)
        meta_b = jnp.broadcast_to(t_meta[:, None], (b, n_chunks, N_META, N_KV_HEADS, HEAD_DIM))
        bands = [tc[:, i:i + n_chunks] for i in range(WINDOW_CHUNKS + 1)]
        return jnp.concatenate([meta_b] + bands, axis=2)

    kb = banded(k, km)
    vb = banded(v, vm)
    chunk_idx = jnp.arange(n_chunks)[:, None]
    valid = jnp.concatenate(
        [jnp.ones((n_chunks, N_META), dtype=bool)]
        + [jnp.broadcast_to(chunk_idx >= WINDOW_CHUNKS - i, (n_chunks, CHUNK))
           for i in range(WINDOW_CHUNKS + 1)], axis=1)

    qf = q[:, N_META:].reshape(b, n_chunks, CHUNK, N_KV_HEADS, Q_PER_KV, HEAD_DIM)
    s = jnp.einsum('bnqhgd,bnkhd->bnhgqk', qf, kb).astype(jnp.float32)
    s = jnp.where(valid[None, :, None, None, None, :], s, NEG_INF)
    p = sink_softmax(s, sink[None, None, :, :, None, None])
    o = jnp.einsum('bnhgqk,bnkhd->bnqhgd', p.astype(v.dtype), vb)
    o = o.reshape(b, n_chunks * CHUNK, D_ATTN)
    return jnp.concatenate([o_meta, o], axis=1)


def multiscale_pool(u, w_pool, scale):
    b, l, _ = u.shape
    u32 = u.astype(jnp.float32)
    cs = jnp.cumsum(u32, axis=1)
    outs = []
    for gi, w in enumerate(POOL_SIZES):
        sl = slice(gi * POOL_GROUP, (gi + 1) * POOL_GROUP)
        c = cs[..., sl]
        c_lag = jnp.pad(c, ((0, 0), (w, 0), (0, 0)))[:, :l]
        count = jnp.minimum(jnp.arange(1, l + 1), w).astype(jnp.float32)[None, :, None]
        outs.append((c - c_lag) / count - u32[..., sl])
    d = jnp.stack(outs, axis=2).astype(u.dtype)
    y = jnp.einsum('blgc,gcd->blgd', d, w_pool).reshape(b, l, D_POOL)
    return y * scale


def swiglu(x, w_gate, w_up, w_down):
    return (jax.nn.silu(x @ w_gate) * (x @ w_up)) @ w_down


def moe_swiglu(x, w_router, w_gate, w_up, w_down):
    logits = (x @ w_router).astype(jnp.float32)
    top_vals, top_idx = lax.top_k(logits, TOP_K)
    gates = jax.nn.softmax(top_vals, axis=-1)
    gate_full = jnp.sum(jax.nn.one_hot(top_idx, N_EXPERTS, dtype=jnp.float32) * gates[..., None], axis=1)
    gate_full = gate_full.astype(x.dtype)
    y = jnp.zeros_like(x)
    for e in range(N_EXPERTS):
        y = y + gate_full[:, e:e + 1] * swiglu(x, w_gate[e], w_up[e], w_down[e])
    return y


def setup_inputs(seed: int = 0) -> dict:
    key = jax.random.key(seed)
    ks = jax.random.split(key, 20)
    f32 = jnp.float32

    def nrm(k, shape, scale):
        return jax.random.normal(k, shape, f32) * scale

    return {
        "x": nrm(ks[0], (BATCH, SEQ, D_MODEL), 1.0),
        "meta_tokens": nrm(ks[1], (N_META, D_MODEL), 1.0),
        "norm_mix_g": 1.0 + nrm(ks[2], (DEPTH, D_MODEL), 0.02),
        "w_in": nrm(ks[3], (DEPTH, D_MODEL, D_IN_PROJ), D_MODEL ** -0.5),
        "attn_sinks": nrm(ks[4], (DEPTH, N_Q_HEADS), 0.5),
        "pool_w": nrm(ks[5], (DEPTH, N_POOL_GROUPS, POOL_GROUP, POOL_GROUP), POOL_GROUP ** -0.5),
        "pool_scale": 1.0 + nrm(ks[6], (DEPTH, D_POOL), 0.02),
        "w_out": nrm(ks[7], (DEPTH, D_MIX, D_MODEL), D_MIX ** -0.5),
        "norm_ffn_g": 1.0 + nrm(ks[8], (DEPTH, D_MODEL), 0.02),
        "dense_w_gate": nrm(ks[9], (N_DENSE, D_MODEL, D_FF), D_MODEL ** -0.5),
        "dense_w_up": nrm(ks[10], (N_DENSE, D_MODEL, D_FF), D_MODEL ** -0.5),
        "dense_w_down": nrm(ks[11], (N_DENSE, D_FF, D_MODEL), D_FF ** -0.5),
        "moe_w_router": nrm(ks[12], (N_MOE, D_MODEL, N_EXPERTS), D_MODEL ** -0.5),
        "moe_w_gate": nrm(ks[13], (N_MOE, N_EXPERTS, D_MODEL, D_FF), D_MODEL ** -0.5),
        "moe_w_up": nrm(ks[14], (N_MOE, N_EXPERTS, D_MODEL, D_FF), D_MODEL ** -0.5),
        "moe_w_down": nrm(ks[15], (N_MOE, N_EXPERTS, D_FF, D_MODEL), D_FF ** -0.5),
        "final_norm_g": 1.0 + nrm(ks[16], (D_MODEL,), 0.02),
    }


def reference(x, meta_tokens, norm_mix_g, w_in, attn_sinks, pool_w, pool_scale, w_out,
              norm_ffn_g, dense_w_gate, dense_w_up, dense_w_down, moe_w_router,
              moe_w_gate, moe_w_up, moe_w_down, final_norm_g):
    b = x.shape[0]
    meta = jnp.broadcast_to(meta_tokens[None].astype(x.dtype), (b, N_META, D_MODEL))
    h = jnp.concatenate([meta, x], axis=1)
    for layer in range(DEPTH):
        xn = rmsnorm(h, norm_mix_g[layer])
        proj = xn @ w_in[layer]
        q = proj[..., :D_ATTN]
        k = proj[..., D_ATTN:D_ATTN + D_KV]
        v = proj[..., D_ATTN + D_KV:D_ATTN + 2 * D_KV]
        u = proj[..., D_ATTN + 2 * D_KV:]
        a = chunk_window_attention(q, k, v, attn_sinks[layer])
        p = multiscale_pool(u, pool_w[layer], pool_scale[layer])
        h = h + jnp.concatenate([a, p], axis=-1) @ w_out[layer]
        xn = rmsnorm(h, norm_ffn_g[layer])
        if layer % 2 == 0:
            i = layer // 2
            f = swiglu(xn, dense_w_gate[i], dense_w_up[i], dense_w_down[i])
        else:
            i = layer // 2
            f = moe_swiglu(xn.reshape(-1, D_MODEL), moe_w_router[i], moe_w_gate[i],
                           moe_w_up[i], moe_w_down[i]).reshape(h.shape)
        h = h + f
    return rmsnorm(h[:, N_META:], final_norm_g)
```

```python
import functools

import jax
import jax.numpy as jnp
from jax import lax
from jax.experimental import pallas as pl
from jax.experimental.pallas import tpu as pltpu

F32 = jnp.float32
BF16 = jnp.bfloat16

CHUNK = 64
N_META = 16
HEAD_DIM = 64
N_Q_HEADS = 8
D_ATTN = 512
D_KV = 128
D_POOL = 512
POOL_SIZES = (2, 4, 8, 16)
POOL_GROUP = 128
N_EXPERTS = 8
EPS = 1e-5
NEG_INF = -1e30

LANES = 128
HALF = LANES // 2
Q_BLOCK = 2 * CHUNK
KEY_WINDOW = 4 * CHUNK
HALO = 2 * CHUNK
POOL_HALO = max(POOL_SIZES)
VMEM_LIMIT = 56 * 1024 * 1024

MIX_TILE = 512
FFN_TILE = 512
FFN_BLOCK = 1792
ROW_TILE = 512


def _rmsnorm(x, g):
    return x * lax.rsqrt(jnp.mean(x * x, axis=-1, keepdims=True) + EPS) * g


def _dot(a, b):
    return jnp.dot(a, b, preferred_element_type=F32)


def _dot_nt(a, b):
    return lax.dot_general(a, b, (((1,), (1,)), ((), ())), preferred_element_type=F32)


def _head_variants(t):
    low = lax.broadcasted_iota(jnp.int32, t.shape, 1) < HALF
    t_r = pltpu.roll(t, HALF, axis=1)
    zero = jnp.zeros_like(t)
    return (jnp.where(low, t, zero).astype(BF16),
            jnp.where(low, zero, t_r).astype(BF16),
            jnp.where(low, t_r, zero).astype(BF16),
            jnp.where(low, zero, t).astype(BF16))


def _sink_softmax_pv(s_parts, v_parts, sink):
    m = sink
    for s in s_parts:
        m = jnp.maximum(m, jnp.max(s, axis=1, keepdims=True))
    den = jnp.exp(sink - m)
    o = None
    for s, v in zip(s_parts, v_parts):
        e = jnp.exp(s - m)
        den = den + jnp.sum(e, axis=1, keepdims=True)
        pv = _dot(e.astype(BF16), v)
        o = pv if o is None else o + pv
    return o / den


def _mix_kernel(sinks_ref, h_ref, gmix_ref, win_ref, km_ref, vm_ref, um_ref, poolw_ref,
                pscale_ref, wout_ref, gffn_ref, hout_ref, xn_ref,
                q_s, kvar_s, vvar_s, uw_s, a_s, *, tq):
    i = pl.program_id(1)

    @pl.when(i == 0)
    def _():
        kvar_s[:, 0:HALO, :] = jnp.zeros((4, HALO, LANES), BF16)
        vvar_s[:, 0:HALO, :] = jnp.zeros((4, HALO, LANES), BF16)
        uw_s[0:POOL_HALO, :] = um_ref[...]

    x = h_ref[...]
    xn = _rmsnorm(x, gmix_ref[...]).astype(BF16)
    q_s[...] = (_dot(xn, win_ref[:, 0:D_ATTN]) * (HEAD_DIM ** -0.5)).astype(BF16)
    k = _dot(xn, win_ref[:, D_ATTN:D_ATTN + D_KV])
    v = _dot(xn, win_ref[:, D_ATTN + D_KV:D_ATTN + 2 * D_KV])
    uw_s[POOL_HALO:POOL_HALO + tq, :] = _dot(xn, win_ref[:, D_ATTN + 2 * D_KV:])
    for idx, (kk, vv) in enumerate(zip(_head_variants(k), _head_variants(v))):
        kvar_s[idx, HALO:HALO + tq, :] = kk
        vvar_s[idx, HALO:HALO + tq, :] = vv
    km_var = _head_variants(km_ref[...])
    vm_var = _head_variants(vm_ref[...])

    q_chunk = lax.broadcasted_iota(jnp.int32, (Q_BLOCK, KEY_WINDOW), 0) // CHUNK
    k_chunk = lax.broadcasted_iota(jnp.int32, (Q_BLOCK, KEY_WINDOW), 1) // CHUNK
    band = (k_chunk >= q_chunk) & (k_chunk <= q_chunk + 2)

    def block_body(blk, carry):
        r0 = pl.multiple_of(blk * Q_BLOCK, Q_BLOCK)
        first_chunk = jnp.where((i == 0) & (blk == 0), 2, 0)
        valid = band & (k_chunk >= first_chunk)
        for j in range(D_ATTN // LANES):
            col = q_s[pl.ds(r0, Q_BLOCK), j * LANES:(j + 1) * LANES]
            acc = None
            for par in range(2):
                var = 2 * (j // 2) + par
                s = _dot_nt(col, kvar_s[var, pl.ds(r0, KEY_WINDOW), :])
                s = jnp.where(valid, s, NEG_INF)
                sm = _dot_nt(col, km_var[var])
                o = _sink_softmax_pv(
                    (sm, s), (vm_var[var], vvar_s[var, pl.ds(r0, KEY_WINDOW), :]),
                    sinks_ref[2 * j + par])
                acc = o if acc is None else acc + o
            a_s[pl.ds(r0, Q_BLOCK), j * LANES:(j + 1) * LANES] = acc.astype(BF16)
        return carry

    lax.fori_loop(0, tq // Q_BLOCK, block_body, 0)

    pooled = []
    for g, w in enumerate(POOL_SIZES):
        cs = slice(g * POOL_GROUP, (g + 1) * POOL_GROUP)
        cur = uw_s[POOL_HALO:POOL_HALO + tq, cs]
        acc = cur
        for lag in range(1, w):
            acc = acc + uw_s[POOL_HALO - lag:POOL_HALO - lag + tq, cs]
        d = acc * (1.0 / w) - cur
        pooled.append(_dot(d.astype(BF16), poolw_ref[g]) * pscale_ref[:, cs])
    p = jnp.concatenate(pooled, axis=1).astype(BF16)

    hn = x + _dot(a_s[...], wout_ref[0:D_ATTN, :]) + _dot(p, wout_ref[D_ATTN:, :])
    hout_ref[...] = hn
    xn_ref[...] = _rmsnorm(hn, gffn_ref[...]).astype(xn_ref.dtype)

    kvar_s[:, 0:HALO, :] = kvar_s[:, tq:tq + HALO, :]
    vvar_s[:, 0:HALO, :] = vvar_s[:, tq:tq + HALO, :]
    uw_s[0:POOL_HALO, :] = uw_s[tq:tq + POOL_HALO, :]


def _mix_call(h, batch, sinks, gmix, win, km, vm, um, poolw, pscale, wout, gffn, xn_dtype):
    t, d = h.shape
    tq = MIX_TILE
    nt = t // batch // tq
    assert nt * tq * batch == t
    full = lambda shape: pl.BlockSpec(shape, lambda b, i: (0,) * len(shape))
    row = pl.BlockSpec((tq, d), lambda b, i: (b * nt + i, 0))
    return pl.pallas_call(
        functools.partial(_mix_kernel, tq=tq),
        grid=(batch, nt),
        in_specs=[
            pl.BlockSpec(memory_space=pltpu.SMEM),
            row, full((1, d)), full(win.shape), full(km.shape), full(vm.shape), full(um.shape),
            full(poolw.shape), full((1, D_POOL)), full(wout.shape), full((1, d)),
        ],
        out_specs=[row, row],
        out_shape=[jax.ShapeDtypeStruct((t, d), F32), jax.ShapeDtypeStruct((t, d), xn_dtype)],
        scratch_shapes=[
            pltpu.VMEM((tq, D_ATTN), BF16),
            pltpu.VMEM((4, HALO + tq, LANES), BF16),
            pltpu.VMEM((4, HALO + tq, LANES), BF16),
            pltpu.VMEM((POOL_HALO + tq, D_POOL), F32),
            pltpu.VMEM((tq, D_ATTN), BF16),
        ],
        compiler_params=pltpu.CompilerParams(
            dimension_semantics=("arbitrary", "arbitrary"), vmem_limit_bytes=VMEM_LIMIT),
        name="mix",
    )(sinks, h, gmix, win, km, vm, um, poolw, pscale, wout, gffn)


def _meta_kernel(sinks_ref, h_ref, gmix_ref, win_ref, poolw_ref, pscale_ref, wout_ref, gffn_ref,
                 kvu_ref, hout_ref, xn_ref, uw_s):
    x = h_ref[...]
    xn = _rmsnorm(x, gmix_ref[...]).astype(BF16)
    q = (_dot(xn, win_ref[:, 0:D_ATTN]) * (HEAD_DIM ** -0.5)).astype(BF16)
    k = _dot(xn, win_ref[:, D_ATTN:D_ATTN + D_KV])
    v = _dot(xn, win_ref[:, D_ATTN + D_KV:D_ATTN + 2 * D_KV])
    u = _dot(xn, win_ref[:, D_ATTN + 2 * D_KV:])
    kvu_ref[:, 0:D_KV] = k
    kvu_ref[:, D_KV:2 * D_KV] = v
    kvu_ref[:, 2 * D_KV:] = u

    k_var = _head_variants(k)
    v_var = _head_variants(v)
    cols = []
    for j in range(D_ATTN // LANES):
        col = q[:, j * LANES:(j + 1) * LANES]
        acc = None
        for par in range(2):
            var = 2 * (j // 2) + par
            o = _sink_softmax_pv((_dot_nt(col, k_var[var]),), (v_var[var],),
                                 sinks_ref[2 * j + par])
            acc = o if acc is None else acc + o
        cols.append(acc)
    a = jnp.concatenate(cols, axis=1).astype(BF16)

    uw_s[0:POOL_HALO, :] = jnp.zeros((POOL_HALO, D_POOL), F32)
    uw_s[POOL_HALO:, :] = u
    pos = lax.broadcasted_iota(jnp.int32, (N_META, 1), 0)
    pooled = []
    for g, w in enumerate(POOL_SIZES):
        cs = slice(g * POOL_GROUP, (g + 1) * POOL_GROUP)
        cur = uw_s[POOL_HALO:, cs]
        acc = cur
        for lag in range(1, w):
            acc = acc + uw_s[POOL_HALO - lag:POOL_HALO - lag + N_META, cs]
        count = jnp.minimum(pos + 1, w).astype(F32)
        d = acc / count - cur
        pooled.append(_dot(d.astype(BF16), poolw_ref[g]) * pscale_ref[:, cs])
    p = jnp.concatenate(pooled, axis=1).astype(BF16)

    hn = x + _dot(a, wout_ref[0:D_ATTN, :]) + _dot(p, wout_ref[D_ATTN:, :])
    hout_ref[...] = hn
    xn_ref[...] = _rmsnorm(hn, gffn_ref[...]).astype(xn_ref.dtype)


def _meta_call(h, sinks, gmix, win, poolw, pscale, wout, gffn):
    n, d = h.shape
    vmem = pl.BlockSpec(memory_space=pltpu.VMEM)
    return pl.pallas_call(
        _meta_kernel,
        in_specs=[pl.BlockSpec(memory_space=pltpu.SMEM)] + [vmem] * 7,
        out_specs=[vmem, vmem, vmem],
        out_shape=[jax.ShapeDtypeStruct((n, 2 * D_KV + D_POOL), F32),
                   jax.ShapeDtypeStruct((n, d), F32),
                   jax.ShapeDtypeStruct((n, d), BF16)],
        scratch_shapes=[pltpu.VMEM((POOL_HALO + N_META, D_POOL), F32)],
        compiler_params=pltpu.CompilerParams(vmem_limit_bytes=VMEM_LIMIT),
        name="meta_mix",
    )(sinks, h, gmix, win, poolw, pscale, wout, gffn)


def _ffn_kernel(te_ref, na_ref, x_ref, *refs):
    del te_ref
    res_ref = refs[0] if len(refs) == 5 else None
    wg_ref, wu_ref, wd_ref, out_ref = refs[-4:]
    i, j = pl.program_id(0), pl.program_id(1)

    @pl.when(i < na_ref[0])
    def _():
        @pl.when(j == 0)
        def _():
            out_ref[...] = jnp.zeros_like(out_ref) if res_ref is None else res_ref[...]

        x = x_ref[...].astype(BF16)
        gate = _dot(x, wg_ref[...])
        up = _dot(x, wu_ref[...])
        mid = (gate * jax.nn.sigmoid(gate) * up).astype(BF16)
        out_ref[...] += _dot(mid, wd_ref[...])


def _ffn_call(x, res, wg, wu, wd, tile_expert, n_active, tm):
    rows, d = x.shape
    ff = wg.shape[-1]
    n_tiles = rows // tm
    nj = ff // FFN_BLOCK
    assert n_tiles * tm == rows and nj * FFN_BLOCK == ff

    def live(i, na):
        return jnp.minimum(i, na[0] - 1)

    row = pl.BlockSpec((tm, d), lambda i, j, te, na: (live(i, na), 0))
    col_w = pl.BlockSpec(
        (None, d, FFN_BLOCK),
        lambda i, j, te, na: (te[live(i, na)], 0, jnp.where(i < na[0], j, nj - 1)))
    row_w = pl.BlockSpec(
        (None, FFN_BLOCK, d),
        lambda i, j, te, na: (te[live(i, na)], jnp.where(i < na[0], j, nj - 1), 0))
    acts = (x,) if res is None else (x, res)
    return pl.pallas_call(
        _ffn_kernel,
        grid_spec=pltpu.PrefetchScalarGridSpec(
            num_scalar_prefetch=2, grid=(n_tiles, nj),
            in_specs=[row] * len(acts) + [col_w, col_w, row_w], out_specs=row),
        out_shape=jax.ShapeDtypeStruct((rows, d), F32),
        compiler_params=pltpu.CompilerParams(
            dimension_semantics=("arbitrary", "arbitrary"), vmem_limit_bytes=VMEM_LIMIT),
        name="ffn",
    )(tile_expert, n_active, *acts, wg, wu, wd)


def _route_kernel(x_ref, wr_ref, tri_ref, idx_ref, gate_ref, counts_ref, run_s):
    i = pl.program_id(0)

    @pl.when(i == 0)
    def _():
        run_s[...] = jnp.zeros_like(run_s)

    logits = jnp.dot(x_ref[...], wr_ref[...], preferred_element_type=F32,
                     precision=lax.Precision.HIGHEST)
    lane = lax.broadcasted_iota(jnp.int32, logits.shape, 1).astype(F32)
    none = float(N_EXPERTS)
    m1 = jnp.max(logits, axis=1, keepdims=True)
    e1 = jnp.min(jnp.where(logits == m1, lane, none), axis=1, keepdims=True)
    rest = jnp.where(lane == e1, -jnp.inf, logits)
    m2 = jnp.max(rest, axis=1, keepdims=True)
    e2 = jnp.min(jnp.where(rest == m2, lane, none), axis=1, keepdims=True)
    t = jnp.exp(m2 - m1)
    gate_ref[:, 0:1] = 1.0 / (1.0 + t)
    gate_ref[:, 1:2] = t / (1.0 + t)

    chosen = ((lane == e1) | (lane == e2)).astype(F32)
    before = _dot(tri_ref[...], chosen.astype(BF16)) + run_s[...]
    idx_ref[:, 0:1] = e1.astype(jnp.int32)
    idx_ref[:, 1:2] = e2.astype(jnp.int32)
    idx_ref[:, 2:3] = jnp.sum(jnp.where(lane == e1, before, 0.0), axis=1,
                              keepdims=True).astype(jnp.int32)
    idx_ref[:, 3:4] = jnp.sum(jnp.where(lane == e2, before, 0.0), axis=1,
                              keepdims=True).astype(jnp.int32)
    run_s[...] += jnp.sum(chosen, axis=0, keepdims=True)
    counts_ref[...] = run_s[...].astype(jnp.int32)


def _route_call(xn, w_router):
    t, d = xn.shape
    tr = ROW_TILE
    tri = (lax.broadcasted_iota(jnp.int32, (tr, tr), 1)
           < lax.broadcasted_iota(jnp.int32, (tr, tr), 0)).astype(BF16)
    return pl.pallas_call(
        _route_kernel,
        grid=(t // tr,),
        in_specs=[pl.BlockSpec((tr, d), lambda i: (i, 0)),
                  pl.BlockSpec((d, N_EXPERTS), lambda i: (0, 0)),
                  pl.BlockSpec((tr, tr), lambda i: (0, 0))],
        out_specs=[pl.BlockSpec((tr, 4), lambda i: (i, 0)),
                   pl.BlockSpec((tr, 2), lambda i: (i, 0)),
                   pl.BlockSpec((1, N_EXPERTS), lambda i: (0, 0))],
        out_shape=[jax.ShapeDtypeStruct((t, 4), jnp.int32),
                   jax.ShapeDtypeStruct((t, 2), F32),
                   jax.ShapeDtypeStruct((1, N_EXPERTS), jnp.int32)],
        scratch_shapes=[pltpu.VMEM((1, N_EXPERTS), F32)],
        compiler_params=pltpu.CompilerParams(
            dimension_semantics=("arbitrary",), vmem_limit_bytes=VMEM_LIMIT),
        name="route",
    )(xn, w_router, tri)


def _dispatch_kernel(pad_lo_ref, pad_hi_ref, pos_ref, x_ref, xs_ref, zero_s, sem, *, td):
    i = pl.program_id(0)

    def row_copy(src, r_src, r_dst):
        return pltpu.make_async_copy(src.at[pl.ds(r_src, 1)], xs_ref.at[pl.ds(r_dst, 1)], sem)

    @pl.when(i == 0)
    def _():
        zero_s[...] = jnp.zeros_like(zero_s)
        for e in range(N_EXPERTS):
            def start(r, c):
                row_copy(zero_s, 0, r).start()
                return c

            def wait(r, c):
                row_copy(zero_s, 0, r).wait()
                return c

            lax.fori_loop(pad_lo_ref[e], pad_hi_ref[e], start, 0)
            lax.fori_loop(pad_lo_ref[e], pad_hi_ref[e], wait, 0)

    def issue(r, c):
        row_copy(x_ref, r, pos_ref[0, r]).start()
        row_copy(x_ref, r, pos_ref[0, td + r]).start()
        return c

    lax.fori_loop(0, td, issue, 0)
    for _ in range(2):
        pltpu.make_async_copy(x_ref, xs_ref.at[pl.ds(0, td)], sem).wait()


def _dispatch_call(xn, pos, pad_lo, pad_hi, rows):
    t, d = xn.shape
    td = ROW_TILE
    return pl.pallas_call(
        functools.partial(_dispatch_kernel, td=td),
        grid_spec=pltpu.PrefetchScalarGridSpec(
            num_scalar_prefetch=2, grid=(t // td,),
            in_specs=[pl.BlockSpec((None, 1, 2 * td), lambda i, lo, hi: (i, 0, 0),
                                   memory_space=pltpu.SMEM),
                      pl.BlockSpec((td, d), lambda i, lo, hi: (i, 0))],
            out_specs=pl.BlockSpec(memory_space=pl.ANY),
            scratch_shapes=[pltpu.VMEM((8, d), F32), pltpu.SemaphoreType.DMA]),
        out_shape=jax.ShapeDtypeStruct((rows, d), F32),
        compiler_params=pltpu.CompilerParams(
            dimension_semantics=("arbitrary",), vmem_limit_bytes=VMEM_LIMIT,
            has_side_effects=True),
        name="dispatch",
    )(pad_lo, pad_hi, pos, xn)


def _combine_kernel(pos_ref, h_ref, gate_ref, gfin_ref, y_ref, out_ref, buf, sem, *, tc):
    def issue(r, c):
        for kk in range(2):
            pltpu.make_async_copy(y_ref.at[pl.ds(pos_ref[0, kk * tc + r], 1)],
                                  buf.at[kk, pl.ds(r, 1)], sem).start()
        return c

    lax.fori_loop(0, tc, issue, 0)
    for kk in range(2):
        pltpu.make_async_copy(y_ref.at[pl.ds(0, tc)], buf.at[kk], sem).wait()
    gates = gate_ref[...]
    hn = h_ref[...] + (gates[:, 0:1] * buf[0] + gates[:, 1:2] * buf[1])
    out_ref[...] = _rmsnorm(hn, gfin_ref[...])


def _combine_call(h, y, pos, gates, gfin):
    t, d = h.shape
    tc = ROW_TILE
    return pl.pallas_call(
        functools.partial(_combine_kernel, tc=tc),
        grid=(t // tc,),
        in_specs=[pl.BlockSpec((None, 1, 2 * tc), lambda i: (i, 0, 0), memory_space=pltpu.SMEM),
                  pl.BlockSpec((tc, d), lambda i: (i, 0)),
                  pl.BlockSpec((tc, 2), lambda i: (i, 0)),
                  pl.BlockSpec((1, d), lambda i: (0, 0)),
                  pl.BlockSpec(memory_space=pl.ANY)],
        out_specs=pl.BlockSpec((tc, d), lambda i: (i, 0)),
        out_shape=jax.ShapeDtypeStruct((t, d), F32),
        scratch_shapes=[pltpu.VMEM((2, tc, d), F32), pltpu.SemaphoreType.DMA],
        compiler_params=pltpu.CompilerParams(
            dimension_semantics=("arbitrary",), vmem_limit_bytes=VMEM_LIMIT),
        name="combine",
    )(pos, h, gates, gfin, y)


def _moe(h, xn, w_router, wg, wu, wd, gfin):
    t, d = h.shape
    tm = FFN_TILE
    idx, gates, counts = _route_call(xn, w_router)
    counts = counts[0]
    padded = (counts + tm - 1) // tm * tm
    offsets = jnp.cumsum(padded) - padded
    pos = jnp.stack([offsets[idx[:, 0]] + idx[:, 2], offsets[idx[:, 1]] + idx[:, 3]])
    pos = pos.reshape(2, t // ROW_TILE, ROW_TILE).transpose(1, 0, 2)
    pos = pos.reshape(t // ROW_TILE, 1, 2 * ROW_TILE)

    n_tiles = 2 * t // tm + N_EXPERTS
    tile_end = jnp.cumsum(padded // tm)
    n_active = tile_end[-1:]
    tile_expert = jnp.searchsorted(tile_end, jnp.arange(n_tiles), side="right")
    tile_expert = jnp.minimum(tile_expert, N_EXPERTS - 1).astype(jnp.int32)

    xs = _dispatch_call(xn, pos, offsets + counts, offsets + padded, n_tiles * tm)
    y = _ffn_call(xs, None, wg, wu, wd, tile_expert, n_active.astype(jnp.int32), tm)
    return _combine_call(h, y, pos, gates, gfin)


def kernel(x, meta_tokens, norm_mix_g, w_in, attn_sinks, pool_w, pool_scale, w_out, norm_ffn_g,
           dense_w_gate, dense_w_up, dense_w_down, moe_w_router, moe_w_gate, moe_w_up,
           moe_w_down, final_norm_g):
    b, s, d = x.shape
    h = x.reshape(b * s, d)
    row = lambda a: a.reshape(1, -1)
    w_in_b, w_out_b, pool_w_b = w_in.astype(BF16), w_out.astype(BF16), pool_w.astype(BF16)
    one_tile = (jnp.zeros((1,), jnp.int32), jnp.ones((1,), jnp.int32))

    kvu0, hm, xnm = _meta_call(meta_tokens, attn_sinks[0], row(norm_mix_g[0]), w_in_b[0],
                               pool_w_b[0], row(pool_scale[0]), w_out_b[0], row(norm_ffn_g[0]))
    dense = (dense_w_gate.astype(BF16), dense_w_up.astype(BF16), dense_w_down.astype(BF16))
    hm = _ffn_call(xnm, hm, *dense, *one_tile, N_META)
    kvu1, _, _ = _meta_call(hm, attn_sinks[1], row(norm_mix_g[1]), w_in_b[1], pool_w_b[1],
                            row(pool_scale[1]), w_out_b[1], row(norm_ffn_g[1]))

    def mix(h, layer, kvu, xn_dtype):
        return _mix_call(h, b, attn_sinks[layer], row(norm_mix_g[layer]), w_in_b[layer],
                         kvu[:, 0:D_KV], kvu[:, D_KV:2 * D_KV], kvu[:, 2 * D_KV:],
                         pool_w_b[layer], row(pool_scale[layer]), w_out_b[layer],
                         row(norm_ffn_g[layer]), xn_dtype)

    h, xn = mix(h, 0, kvu0, BF16)
    n_tiles = b * s // FFN_TILE
    h = _ffn_call(xn, h, *dense, jnp.zeros((n_tiles,), jnp.int32),
                  jnp.full((1,), n_tiles, jnp.int32), FFN_TILE)
    h, xn = mix(h, 1, kvu1, F32)
    out = _moe(h, xn, moe_w_router[0], moe_w_gate[0].astype(BF16), moe_w_up[0].astype(BF16),
               moe_w_down[0].astype(BF16), row(final_norm_g))
    return out.reshape(b, s, d)
```

```python
import functools

import jax
import jax.numpy as jnp
from jax import lax
from jax.experimental import pallas as pl
from jax.experimental.pallas import tpu as pltpu

F32 = jnp.float32
BF16 = jnp.bfloat16

CHUNK = 64
N_META = 16
HEAD_DIM = 64
N_Q_HEADS = 8
D_ATTN = 512
D_KV = 128
D_POOL = 512
POOL_SIZES = (2, 4, 8, 16)
POOL_GROUP = 128
N_EXPERTS = 8
EPS = 1e-5
NEG_INF = -1e30

LANES = 128
HALF = LANES // 2
Q_BLOCK = 2 * CHUNK
KEY_WINDOW = 4 * CHUNK
HALO = 2 * CHUNK
E_WIDTH = KEY_WINDOW + LANES
SOFTMAX_ROWS = 32
POOL_HALO = max(POOL_SIZES)
VMEM_LIMIT = 56 * 1024 * 1024

MIX_TILE = 512
FFN_TILE = 512
FFN_BLOCK = 1792
ROW_TILE = 512


def _rmsnorm(x, g):
    return x * lax.rsqrt(jnp.mean(x * x, axis=-1, keepdims=True) + EPS) * g


def _dot(a, b):
    return jnp.dot(a, b, preferred_element_type=F32)


def _dot_nt(a, b):
    return lax.dot_general(a, b, (((1,), (1,)), ((), ())), preferred_element_type=F32)


def _head_variants(t, fill=0.0):
    low = lax.broadcasted_iota(jnp.int32, t.shape, 1) < HALF
    t_r = pltpu.roll(t, HALF, axis=1)
    other = jnp.full_like(t, fill)
    return (jnp.where(low, t, other).astype(BF16),
            jnp.where(low, other, t_r).astype(BF16),
            jnp.where(low, t_r, other).astype(BF16),
            jnp.where(low, other, t).astype(BF16))


def _sink_softmax_pv(s_parts, v_parts, sink):
    m = sink
    for s in s_parts:
        m = jnp.maximum(m, jnp.max(s, axis=1, keepdims=True))
    den = jnp.exp(sink - m)
    o = None
    for s, v in zip(s_parts, v_parts):
        e = jnp.exp(s - m)
        den = den + jnp.sum(e, axis=1, keepdims=True)
        pv = _dot(e.astype(BF16), v)
        o = pv if o is None else o + pv
    return o / den


def _meta_tiles(km, vm):
    km_var = _head_variants(km)
    vm_var = _head_variants(vm, 1.0)
    zeros = lambda n: jnp.zeros((n, LANES), BF16)
    row = lax.broadcasted_iota(jnp.int32, (N_META, LANES), 0)
    low = lax.broadcasted_iota(jnp.int32, (N_META, LANES), 1) < HALF
    rest = LANES - 3 * N_META
    k_tiles, v_tiles = [], []
    for h in range(2):
        k_tiles.append(jnp.concatenate([km_var[2 * h], km_var[2 * h + 1], zeros(LANES - 2 * N_META)]))
        for par in range(2):
            den_half = low if par == 1 else jnp.logical_not(low)
            sink_rows = jnp.where((row == par) & den_half, 1.0, 0.0).astype(BF16)
            v_tiles.append(jnp.concatenate(
                [vm_var[2 * h] if par == 0 else zeros(N_META),
                 vm_var[2 * h + 1] if par == 1 else zeros(N_META), sink_rows, zeros(rest)]))
    return k_tiles, v_tiles


def _mix_kernel(sinks_ref, h_ref, gmix_ref, win_ref, km_ref, vm_ref, um_ref, poolw_ref,
                pscale_ref, wout_ref, gffn_ref, hout_ref, xn_ref,
                q_s, kvar_s, vvar_s, uw_s, a_s, kmeta_s, vmeta_s, s_s, sm_s, e_s, *, tq):
    i = pl.program_id(1)

    @pl.when(i == 0)
    def _():
        kvar_s[:, 0:HALO, :] = jnp.zeros((4, HALO, LANES), BF16)
        vvar_s[:, 0:HALO, :] = jnp.zeros((4, HALO, LANES), BF16)
        uw_s[0:POOL_HALO, :] = um_ref[...]

    x = h_ref[...]
    xn = _rmsnorm(x, gmix_ref[...]).astype(BF16)
    q_s[...] = (_dot(xn, win_ref[:, 0:D_ATTN]) * (HEAD_DIM ** -0.5)).astype(BF16)
    k = _dot(xn, win_ref[:, D_ATTN:D_ATTN + D_KV])
    v = _dot(xn, win_ref[:, D_ATTN + D_KV:D_ATTN + 2 * D_KV])
    uw_s[POOL_HALO:POOL_HALO + tq, :] = _dot(xn, win_ref[:, D_ATTN + 2 * D_KV:])
    for idx, (kk, vv) in enumerate(zip(_head_variants(k), _head_variants(v, 1.0))):
        kvar_s[idx, HALO:HALO + tq, :] = kk
        vvar_s[idx, HALO:HALO + tq, :] = vv
    k_tiles, v_tiles = _meta_tiles(km_ref[...], vm_ref[...])
    for h in range(2):
        kmeta_s[h] = k_tiles[h]
    for hp in range(4):
        vmeta_s[hp] = v_tiles[hp]

    def block_body(blk, carry):
        r0 = pl.multiple_of(blk * Q_BLOCK, Q_BLOCK)
        q_chunk = lax.broadcasted_iota(jnp.int32, (Q_BLOCK, KEY_WINDOW), 0) // CHUNK
        k_chunk = lax.broadcasted_iota(jnp.int32, (Q_BLOCK, KEY_WINDOW), 1) // CHUNK
        first_chunk = jnp.where((i == 0) & (blk == 0), 2, 0)
        valid = (k_chunk >= jnp.maximum(q_chunk, first_chunk)) & (k_chunk <= q_chunk + 2)

        for j in range(D_ATTN // LANES):
            col = q_s[pl.ds(r0, Q_BLOCK), j * LANES:(j + 1) * LANES]
            for par in range(2):
                s = _dot_nt(col, kvar_s[2 * (j // 2) + par, pl.ds(r0, KEY_WINDOW), :])
                s_s[j, :, par * KEY_WINDOW:(par + 1) * KEY_WINDOW] = jnp.where(valid, s, NEG_INF)
            sm_s[j] = _dot_nt(col, kmeta_s[j // 2])

        lane = lax.broadcasted_iota(jnp.int32, (1, LANES), 1)
        for j in range(D_ATTN // LANES):
            for par in range(2):
                own_meta = (lane >= par * N_META) & (lane < (par + 1) * N_META)
                other = jnp.where(lane == 2 * N_META + par, sinks_ref[2 * j + par], NEG_INF)
                for g in range(Q_BLOCK // SOFTMAX_ROWS):
                    rows = slice(g * SOFTMAX_ROWS, (g + 1) * SOFTMAX_ROWS)
                    s = s_s[j, rows, par * KEY_WINDOW:(par + 1) * KEY_WINDOW]
                    sm = jnp.where(own_meta, sm_s[j, rows, :], other)
                    m = jnp.maximum(jnp.maximum(s[:, :LANES], s[:, LANES:]), sm)
                    m = jnp.max(m, axis=1, keepdims=True)
                    e_s[j, rows, par * E_WIDTH:par * E_WIDTH + KEY_WINDOW] = (
                        jnp.exp(s - m).astype(BF16))
                    e_s[j, rows, par * E_WIDTH + KEY_WINDOW:(par + 1) * E_WIDTH] = (
                        jnp.exp(sm - m).astype(BF16))

        low = lax.broadcasted_iota(jnp.int32, (Q_BLOCK, LANES), 1) < HALF
        for j in range(D_ATTN // LANES):
            o = []
            for par in range(2):
                hp = 2 * (j // 2) + par
                e0 = par * E_WIDTH
                o.append(_dot(e_s[j, :, e0:e0 + KEY_WINDOW], vvar_s[hp, pl.ds(r0, KEY_WINDOW), :])
                         + _dot(e_s[j, :, e0 + KEY_WINDOW:e0 + E_WIDTH], vmeta_s[hp]))
            num = jnp.where(low, o[0], o[1])
            den = pltpu.roll(jnp.where(low, o[1], o[0]), HALF, axis=1)
            a_s[pl.ds(r0, Q_BLOCK), j * LANES:(j + 1) * LANES] = (num / den).astype(BF16)
        return carry

    lax.fori_loop(0, tq // Q_BLOCK, block_body, 0)

    pooled = []
    for g, w in enumerate(POOL_SIZES):
        cs = slice(g * POOL_GROUP, (g + 1) * POOL_GROUP)
        cur = uw_s[POOL_HALO:POOL_HALO + tq, cs]
        acc = cur
        for lag in range(1, w):
            acc = acc + uw_s[POOL_HALO - lag:POOL_HALO - lag + tq, cs]
        d = acc * (1.0 / w) - cur
        pooled.append(_dot(d.astype(BF16), poolw_ref[g]) * pscale_ref[:, cs])
    p = jnp.concatenate(pooled, axis=1).astype(BF16)

    hn = x + _dot(a_s[...], wout_ref[0:D_ATTN, :]) + _dot(p, wout_ref[D_ATTN:, :])
    hout_ref[...] = hn
    xn_ref[...] = _rmsnorm(hn, gffn_ref[...]).astype(xn_ref.dtype)

    kvar_s[:, 0:HALO, :] = kvar_s[:, tq:tq + HALO, :]
    vvar_s[:, 0:HALO, :] = vvar_s[:, tq:tq + HALO, :]
    uw_s[0:POOL_HALO, :] = uw_s[tq:tq + POOL_HALO, :]


def _mix_call(h, batch, sinks, gmix, win, km, vm, um, poolw, pscale, wout, gffn, xn_dtype):
    t, d = h.shape
    tq = MIX_TILE
    nt = t // batch // tq
    assert nt * tq * batch == t
    full = lambda shape: pl.BlockSpec(shape, lambda b, i: (0,) * len(shape))
    row = pl.BlockSpec((tq, d), lambda b, i: (b * nt + i, 0))
    return pl.pallas_call(
        functools.partial(_mix_kernel, tq=tq),
        grid=(batch, nt),
        in_specs=[
            pl.BlockSpec(memory_space=pltpu.SMEM),
            row, full((1, d)), full(win.shape), full(km.shape), full(vm.shape), full(um.shape),
            full(poolw.shape), full((1, D_POOL)), full(wout.shape), full((1, d)),
        ],
        out_specs=[row, row],
        out_shape=[jax.ShapeDtypeStruct((t, d), F32), jax.ShapeDtypeStruct((t, d), xn_dtype)],
        scratch_shapes=[
            pltpu.VMEM((tq, D_ATTN), BF16),
            pltpu.VMEM((4, HALO + tq, LANES), BF16),
            pltpu.VMEM((4, HALO + tq, LANES), BF16),
            pltpu.VMEM((POOL_HALO + tq, D_POOL), F32),
            pltpu.VMEM((tq, D_ATTN), BF16),
            pltpu.VMEM((2, LANES, LANES), BF16),
            pltpu.VMEM((4, LANES, LANES), BF16),
            pltpu.VMEM((D_ATTN // LANES, Q_BLOCK, 2 * KEY_WINDOW), F32),
            pltpu.VMEM((D_ATTN // LANES, Q_BLOCK, LANES), F32),
            pltpu.VMEM((D_ATTN // LANES, Q_BLOCK, 2 * E_WIDTH), BF16),
        ],
        compiler_params=pltpu.CompilerParams(
            dimension_semantics=("arbitrary", "arbitrary"), vmem_limit_bytes=VMEM_LIMIT),
        name="mix",
    )(sinks, h, gmix, win, km, vm, um, poolw, pscale, wout, gffn)


def _meta_kernel(sinks_ref, h_ref, gmix_ref, win_ref, poolw_ref, pscale_ref, wout_ref, gffn_ref,
                 kvu_ref, hout_ref, xn_ref, uw_s):
    x = h_ref[...]
    xn = _rmsnorm(x, gmix_ref[...]).astype(BF16)
    q = (_dot(xn, win_ref[:, 0:D_ATTN]) * (HEAD_DIM ** -0.5)).astype(BF16)
    k = _dot(xn, win_ref[:, D_ATTN:D_ATTN + D_KV])
    v = _dot(xn, win_ref[:, D_ATTN + D_KV:D_ATTN + 2 * D_KV])
    u = _dot(xn, win_ref[:, D_ATTN + 2 * D_KV:])
    kvu_ref[:, 0:D_KV] = k
    kvu_ref[:, D_KV:2 * D_KV] = v
    kvu_ref[:, 2 * D_KV:] = u

    k_var = _head_variants(k)
    v_var = _head_variants(v)
    cols = []
    for j in range(D_ATTN // LANES):
        col = q[:, j * LANES:(j + 1) * LANES]
        acc = None
        for par in range(2):
            var = 2 * (j // 2) + par
            o = _sink_softmax_pv((_dot_nt(col, k_var[var]),), (v_var[var],),
                                 sinks_ref[2 * j + par])
            acc = o if acc is None else acc + o
        cols.append(acc)
    a = jnp.concatenate(cols, axis=1).astype(BF16)

    uw_s[0:POOL_HALO, :] = jnp.zeros((POOL_HALO, D_POOL), F32)
    uw_s[POOL_HALO:, :] = u
    pos = lax.broadcasted_iota(jnp.int32, (N_META, 1), 0)
    pooled = []
    for g, w in enumerate(POOL_SIZES):
        cs = slice(g * POOL_GROUP, (g + 1) * POOL_GROUP)
        cur = uw_s[POOL_HALO:, cs]
        acc = cur
        for lag in range(1, w):
            acc = acc + uw_s[POOL_HALO - lag:POOL_HALO - lag + N_META, cs]
        count = jnp.minimum(pos + 1, w).astype(F32)
        d = acc / count - cur
        pooled.append(_dot(d.astype(BF16), poolw_ref[g]) * pscale_ref[:, cs])
    p = jnp.concatenate(pooled, axis=1).astype(BF16)

    hn = x + _dot(a, wout_ref[0:D_ATTN, :]) + _dot(p, wout_ref[D_ATTN:, :])
    hout_ref[...] = hn
    xn_ref[...] = _rmsnorm(hn, gffn_ref[...]).astype(xn_ref.dtype)


def _meta_call(h, sinks, gmix, win, poolw, pscale, wout, gffn):
    n, d = h.shape
    vmem = pl.BlockSpec(memory_space=pltpu.VMEM)
    return pl.pallas_call(
        _meta_kernel,
        in_specs=[pl.BlockSpec(memory_space=pltpu.SMEM)] + [vmem] * 7,
        out_specs=[vmem, vmem, vmem],
        out_shape=[jax.ShapeDtypeStruct((n, 2 * D_KV + D_POOL), F32),
                   jax.ShapeDtypeStruct((n, d), F32),
                   jax.ShapeDtypeStruct((n, d), BF16)],
        scratch_shapes=[pltpu.VMEM((POOL_HALO + N_META, D_POOL), F32)],
        compiler_params=pltpu.CompilerParams(vmem_limit_bytes=VMEM_LIMIT),
        name="meta_mix",
    )(sinks, h, gmix, win, poolw, pscale, wout, gffn)


def _ffn_kernel(te_ref, na_ref, x_ref, *refs):
    del te_ref
    res_ref = refs[0] if len(refs) == 5 else None
    wg_ref, wu_ref, wd_ref, out_ref = refs[-4:]
    i, j = pl.program_id(0), pl.program_id(1)

    @pl.when(i < na_ref[0])
    def _():
        @pl.when(j == 0)
        def _():
            out_ref[...] = jnp.zeros_like(out_ref) if res_ref is None else res_ref[...]

        x = x_ref[...].astype(BF16)
        gate = _dot(x, wg_ref[...])
        up = _dot(x, wu_ref[...])
        mid = (gate * jax.nn.sigmoid(gate) * up).astype(BF16)
        out_ref[...] += _dot(mid, wd_ref[...])


def _ffn_call(x, res, wg, wu, wd, tile_expert, n_active, tm):
    rows, d = x.shape
    ff = wg.shape[-1]
    n_tiles = rows // tm
    nj = ff // FFN_BLOCK
    assert n_tiles * tm == rows and nj * FFN_BLOCK == ff

    def live(i, na):
        return jnp.minimum(i, na[0] - 1)

    row = pl.BlockSpec((tm, d), lambda i, j, te, na: (live(i, na), 0))
    col_w = pl.BlockSpec(
        (None, d, FFN_BLOCK),
        lambda i, j, te, na: (te[live(i, na)], 0, jnp.where(i < na[0], j, nj - 1)))
    row_w = pl.BlockSpec(
        (None, FFN_BLOCK, d),
        lambda i, j, te, na: (te[live(i, na)], jnp.where(i < na[0], j, nj - 1), 0))
    acts = (x,) if res is None else (x, res)
    return pl.pallas_call(
        _ffn_kernel,
        grid_spec=pltpu.PrefetchScalarGridSpec(
            num_scalar_prefetch=2, grid=(n_tiles, nj),
            in_specs=[row] * len(acts) + [col_w, col_w, row_w], out_specs=row),
        out_shape=jax.ShapeDtypeStruct((rows, d), F32),
        compiler_params=pltpu.CompilerParams(
            dimension_semantics=("arbitrary", "arbitrary"), vmem_limit_bytes=VMEM_LIMIT),
        name="ffn",
    )(tile_expert, n_active, *acts, wg, wu, wd)


def _route_kernel(x_ref, wr_ref, tri_ref, idx_ref, gate_ref, counts_ref, run_s):
    i = pl.program_id(0)

    @pl.when(i == 0)
    def _():
        run_s[...] = jnp.zeros_like(run_s)

    logits = jnp.dot(x_ref[...], wr_ref[...], preferred_element_type=F32,
                     precision=lax.Precision.HIGHEST)
    lane = lax.broadcasted_iota(jnp.int32, logits.shape, 1).astype(F32)
    none = float(N_EXPERTS)
    m1 = jnp.max(logits, axis=1, keepdims=True)
    e1 = jnp.min(jnp.where(logits == m1, lane, none), axis=1, keepdims=True)
    rest = jnp.where(lane == e1, -jnp.inf, logits)
    m2 = jnp.max(rest, axis=1, keepdims=True)
    e2 = jnp.min(jnp.where(rest == m2, lane, none), axis=1, keepdims=True)
    t = jnp.exp(m2 - m1)
    gate_ref[:, 0:1] = 1.0 / (1.0 + t)
    gate_ref[:, 1:2] = t / (1.0 + t)

    chosen = ((lane == e1) | (lane == e2)).astype(F32)
    before = _dot(tri_ref[...], chosen.astype(BF16)) + run_s[...]
    idx_ref[:, 0:1] = e1.astype(jnp.int32)
    idx_ref[:, 1:2] = e2.astype(jnp.int32)
    idx_ref[:, 2:3] = jnp.sum(jnp.where(lane == e1, before, 0.0), axis=1,
                              keepdims=True).astype(jnp.int32)
    idx_ref[:, 3:4] = jnp.sum(jnp.where(lane == e2, before, 0.0), axis=1,
                              keepdims=True).astype(jnp.int32)
    run_s[...] += jnp.sum(chosen, axis=0, keepdims=True)
    counts_ref[...] = run_s[...].astype(jnp.int32)


def _route_call(xn, w_router):
    t, d = xn.shape
    tr = ROW_TILE
    tri = (lax.broadcasted_iota(jnp.int32, (tr, tr), 1)
           < lax.broadcasted_iota(jnp.int32, (tr, tr), 0)).astype(BF16)
    return pl.pallas_call(
        _route_kernel,
        grid=(t // tr,),
        in_specs=[pl.BlockSpec((tr, d), lambda i: (i, 0)),
                  pl.BlockSpec((d, N_EXPERTS), lambda i: (0, 0)),
                  pl.BlockSpec((tr, tr), lambda i: (0, 0))],
        out_specs=[pl.BlockSpec((tr, 4), lambda i: (i, 0)),
                   pl.BlockSpec((tr, 2), lambda i: (i, 0)),
                   pl.BlockSpec((1, N_EXPERTS), lambda i: (0, 0))],
        out_shape=[jax.ShapeDtypeStruct((t, 4), jnp.int32),
                   jax.ShapeDtypeStruct((t, 2), F32),
                   jax.ShapeDtypeStruct((1, N_EXPERTS), jnp.int32)],
        scratch_shapes=[pltpu.VMEM((1, N_EXPERTS), F32)],
        compiler_params=pltpu.CompilerParams(
            dimension_semantics=("arbitrary",), vmem_limit_bytes=VMEM_LIMIT),
        name="route",
    )(xn, w_router, tri)


def _dispatch_kernel(pad_lo_ref, pad_hi_ref, pos_ref, x_ref, xs_ref, zero_s, sem, *, td):
    i = pl.program_id(0)

    def row_copy(src, r_src, r_dst):
        return pltpu.make_async_copy(src.at[pl.ds(r_src, 1)], xs_ref.at[pl.ds(r_dst, 1)], sem)

    @pl.when(i == 0)
    def _():
        zero_s[...] = jnp.zeros_like(zero_s)
        for e in range(N_EXPERTS):
            def start(r, c):
                row_copy(zero_s, 0, r).start()
                return c

            def wait(r, c):
                row_copy(zero_s, 0, r).wait()
                return c

            lax.fori_loop(pad_lo_ref[e], pad_hi_ref[e], start, 0)
            lax.fori_loop(pad_lo_ref[e], pad_hi_ref[e], wait, 0)

    def issue(r, c):
        row_copy(x_ref, r, pos_ref[0, r]).start()
        row_copy(x_ref, r, pos_ref[0, td + r]).start()
        return c

    lax.fori_loop(0, td, issue, 0)
    for _ in range(2):
        pltpu.make_async_copy(x_ref, xs_ref.at[pl.ds(0, td)], sem).wait()


def _dispatch_call(xn, pos, pad_lo, pad_hi, rows):
    t, d = xn.shape
    td = ROW_TILE
    return pl.pallas_call(
        functools.partial(_dispatch_kernel, td=td),
        grid_spec=pltpu.PrefetchScalarGridSpec(
            num_scalar_prefetch=2, grid=(t // td,),
            in_specs=[pl.BlockSpec((None, 1, 2 * td), lambda i, lo, hi: (i, 0, 0),
                                   memory_space=pltpu.SMEM),
                      pl.BlockSpec((td, d), lambda i, lo, hi: (i, 0))],
            out_specs=pl.BlockSpec(memory_space=pl.ANY),
            scratch_shapes=[pltpu.VMEM((8, d), F32), pltpu.SemaphoreType.DMA]),
        out_shape=jax.ShapeDtypeStruct((rows, d), F32),
        compiler_params=pltpu.CompilerParams(
            dimension_semantics=("arbitrary",), vmem_limit_bytes=VMEM_LIMIT,
            has_side_effects=True),
        name="dispatch",
    )(pad_lo, pad_hi, pos, xn)


def _combine_kernel(pos_ref, h_ref, gate_ref, gfin_ref, y_ref, out_ref, buf, sem, *, tc):
    def issue(r, c):
        for kk in range(2):
            pltpu.make_async_copy(y_ref.at[pl.ds(pos_ref[0, kk * tc + r], 1)],
                                  buf.at[kk, pl.ds(r, 1)], sem).start()
        return c

    lax.fori_loop(0, tc, issue, 0)
    for kk in range(2):
        pltpu.make_async_copy(y_ref.at[pl.ds(0, tc)], buf.at[kk], sem).wait()
    gates = gate_ref[...]
    hn = h_ref[...] + (gates[:, 0:1] * buf[0] + gates[:, 1:2] * buf[1])
    out_ref[...] = _rmsnorm(hn, gfin_ref[...])


def _combine_call(h, y, pos, gates, gfin):
    t, d = h.shape
    tc = ROW_TILE
    return pl.pallas_call(
        functools.partial(_combine_kernel, tc=tc),
        grid=(t // tc,),
        in_specs=[pl.BlockSpec((None, 1, 2 * tc), lambda i: (i, 0, 0), memory_space=pltpu.SMEM),
                  pl.BlockSpec((tc, d), lambda i: (i, 0)),
                  pl.BlockSpec((tc, 2), lambda i: (i, 0)),
                  pl.BlockSpec((1, d), lambda i: (0, 0)),
                  pl.BlockSpec(memory_space=pl.ANY)],
        out_specs=pl.BlockSpec((tc, d), lambda i: (i, 0)),
        out_shape=jax.ShapeDtypeStruct((t, d), F32),
        scratch_shapes=[pltpu.VMEM((2, tc, d), F32), pltpu.SemaphoreType.DMA],
        compiler_params=pltpu.CompilerParams(
            dimension_semantics=("arbitrary",), vmem_limit_bytes=VMEM_LIMIT),
        name="combine",
    )(pos, h, gates, gfin, y)


def _moe(h, xn, w_router, wg, wu, wd, gfin):
    t, d = h.shape
    tm = FFN_TILE
    idx, gates, counts = _route_call(xn, w_router)
    counts = counts[0]
    padded = (counts + tm - 1) // tm * tm
    offsets = jnp.cumsum(padded) - padded
    pos = jnp.stack([offsets[idx[:, 0]] + idx[:, 2], offsets[idx[:, 1]] + idx[:, 3]])
    pos = pos.reshape(2, t // ROW_TILE, ROW_TILE).transpose(1, 0, 2)
    pos = pos.reshape(t // ROW_TILE, 1, 2 * ROW_TILE)

    n_tiles = 2 * t // tm + N_EXPERTS
    tile_end = jnp.cumsum(padded // tm)
    n_active = tile_end[-1:]
    tile_expert = jnp.sum(jnp.arange(n_tiles)[:, None] >= tile_end[None, :], axis=1)
    tile_expert = jnp.minimum(tile_expert, N_EXPERTS - 1).astype(jnp.int32)

    xs = _dispatch_call(xn, pos, offsets + counts, offsets + padded, n_tiles * tm)
    y = _ffn_call(xs, None, wg, wu, wd, tile_expert, n_active.astype(jnp.int32), tm)
    return _combine_call(h, y, pos, gates, gfin)


def kernel(x, meta_tokens, norm_mix_g, w_in, attn_sinks, pool_w, pool_scale, w_out, norm_ffn_g,
           dense_w_gate, dense_w_up, dense_w_down, moe_w_router, moe_w_gate, moe_w_up,
           moe_w_down, final_norm_g):
    b, s, d = x.shape
    h = x.reshape(b * s, d)
    row = lambda a: a.reshape(1, -1)
    w_in_b, w_out_b, pool_w_b = w_in.astype(BF16), w_out.astype(BF16), pool_w.astype(BF16)
    one_tile = (jnp.zeros((1,), jnp.int32), jnp.ones((1,), jnp.int32))

    kvu0, hm, xnm = _meta_call(meta_tokens, attn_sinks[0], row(norm_mix_g[0]), w_in_b[0],
                               pool_w_b[0], row(pool_scale[0]), w_out_b[0], row(norm_ffn_g[0]))
    dense = (dense_w_gate.astype(BF16), dense_w_up.astype(BF16), dense_w_down.astype(BF16))
    hm = _ffn_call(xnm, hm, *dense, *one_tile, N_META)
    kvu1, _, _ = _meta_call(hm, attn_sinks[1], row(norm_mix_g[1]), w_in_b[1], pool_w_b[1],
                            row(pool_scale[1]), w_out_b[1], row(norm_ffn_g[1]))

    def mix(h, layer, kvu, xn_dtype):
        return _mix_call(h, b, attn_sinks[layer], row(norm_mix_g[layer]), w_in_b[layer],
                         kvu[:, 0:D_KV], kvu[:, D_KV:2 * D_KV], kvu[:, 2 * D_KV:],
                         pool_w_b[layer], row(pool_scale[layer]), w_out_b[layer],
                         row(norm_ffn_g[layer]), xn_dtype)

    h, xn = mix(h, 0, kvu0, BF16)
    n_tiles = b * s // FFN_TILE
    h = _ffn_call(xn, h, *dense, jnp.zeros((n_tiles,), jnp.int32),
                  jnp.full((1,), n_tiles, jnp.int32), FFN_TILE)
    h, xn = mix(h, 1, kvu1, F32)
    out = _moe(h, xn, moe_w_router[0], moe_w_gate[0].astype(BF16), moe_w_up[0].astype(BF16),
               moe_w_down[0].astype(BF16), row(final_norm_g))
    return out.reshape(b, s, d)
```

```python
import functools

import jax
import jax.numpy as jnp
from jax import lax
from jax.experimental import pallas as pl
from jax.experimental.pallas import tpu as pltpu

F32 = jnp.float32
BF16 = jnp.bfloat16

CHUNK = 64
N_META = 16
HEAD_DIM = 64
N_Q_HEADS = 8
D_ATTN = 512
D_KV = 128
D_POOL = 512
POOL_SIZES = (2, 4, 8, 16)
POOL_GROUP = 128
N_EXPERTS = 8
EPS = 1e-5
NEG_INF = -1e30

LANES = 128
HALF = LANES // 2
Q_BLOCK = 2 * CHUNK
KEY_WINDOW = 4 * CHUNK
HALO = 2 * CHUNK
E_WIDTH = KEY_WINDOW + LANES
SOFTMAX_ROWS = 32
POOL_HALO = max(POOL_SIZES)
VMEM_LIMIT = 56 * 1024 * 1024

MIX_TILE = 512
FFN_TILE = 512
FFN_BLOCK = 1792
ROW_TILE = 512


def _rmsnorm(x, g):
    return x * lax.rsqrt(jnp.mean(x * x, axis=-1, keepdims=True) + EPS) * g


def _dot(a, b):
    return jnp.dot(a, b, preferred_element_type=F32)


def _dot_nt(a, b):
    return lax.dot_general(a, b, (((1,), (1,)), ((), ())), preferred_element_type=F32)


def _head_variants(t, fill=0.0):
    low = lax.broadcasted_iota(jnp.int32, t.shape, 1) < HALF
    t_r = pltpu.roll(t, HALF, axis=1)
    other = jnp.full_like(t, fill)
    return (jnp.where(low, t, other).astype(BF16),
            jnp.where(low, other, t_r).astype(BF16),
            jnp.where(low, t_r, other).astype(BF16),
            jnp.where(low, other, t).astype(BF16))


def _sink_softmax_pv(s_parts, v_parts, sink):
    m = sink
    for s in s_parts:
        m = jnp.maximum(m, jnp.max(s, axis=1, keepdims=True))
    den = jnp.exp(sink - m)
    o = None
    for s, v in zip(s_parts, v_parts):
        e = jnp.exp(s - m)
        den = den + jnp.sum(e, axis=1, keepdims=True)
        pv = _dot(e.astype(BF16), v)
        o = pv if o is None else o + pv
    return o / den


def _meta_tiles(km, vm):
    km_var = _head_variants(km)
    vm_var = _head_variants(vm, 1.0)
    zeros = lambda n: jnp.zeros((n, LANES), BF16)
    row = lax.broadcasted_iota(jnp.int32, (N_META, LANES), 0)
    low = lax.broadcasted_iota(jnp.int32, (N_META, LANES), 1) < HALF
    rest = LANES - 3 * N_META
    k_tiles, v_tiles = [], []
    for h in range(2):
        k_tiles.append(jnp.concatenate([km_var[2 * h], km_var[2 * h + 1], zeros(LANES - 2 * N_META)]))
        for par in range(2):
            den_half = low if par == 1 else jnp.logical_not(low)
            sink_rows = jnp.where((row == par) & den_half, 1.0, 0.0).astype(BF16)
            v_tiles.append(jnp.concatenate(
                [vm_var[2 * h] if par == 0 else zeros(N_META),
                 vm_var[2 * h + 1] if par == 1 else zeros(N_META), sink_rows, zeros(rest)]))
    return k_tiles, v_tiles


def _mix_kernel(sinks_ref, h_ref, gmix_ref, win_ref, km_ref, vm_ref, um_ref, poolw_ref,
                pscale_ref, wout_ref, gffn_ref, hout_ref, xn_ref,
                q_s, kvar_s, vvar_s, uw_s, a_s, kmeta_s, vmeta_s, s_s, sm_s, e_s, *, tq):
    i = pl.program_id(1)

    @pl.when(i == 0)
    def _():
        kvar_s[:, 0:HALO, :] = jnp.zeros((4, HALO, LANES), BF16)
        vvar_s[:, 0:HALO, :] = jnp.zeros((4, HALO, LANES), BF16)
        uw_s[0:POOL_HALO, :] = um_ref[...]

    x = h_ref[...]
    xn = _rmsnorm(x, gmix_ref[...]).astype(BF16)
    q_s[...] = (_dot(xn, win_ref[:, 0:D_ATTN]) * (HEAD_DIM ** -0.5)).astype(BF16)
    k = _dot(xn, win_ref[:, D_ATTN:D_ATTN + D_KV])
    v = _dot(xn, win_ref[:, D_ATTN + D_KV:D_ATTN + 2 * D_KV])
    uw_s[POOL_HALO:POOL_HALO + tq, :] = _dot(xn, win_ref[:, D_ATTN + 2 * D_KV:])
    for idx, (kk, vv) in enumerate(zip(_head_variants(k), _head_variants(v, 1.0))):
        kvar_s[idx, HALO:HALO + tq, :] = kk
        vvar_s[idx, HALO:HALO + tq, :] = vv
    k_tiles, v_tiles = _meta_tiles(km_ref[...], vm_ref[...])
    for h in range(2):
        kmeta_s[h] = k_tiles[h]
    for hp in range(4):
        vmeta_s[hp] = v_tiles[hp]

    def block_body(blk, carry):
        r0 = pl.multiple_of(blk * Q_BLOCK, Q_BLOCK)
        q_chunk = lax.broadcasted_iota(jnp.int32, (Q_BLOCK, KEY_WINDOW), 0) // CHUNK
        k_chunk = lax.broadcasted_iota(jnp.int32, (Q_BLOCK, KEY_WINDOW), 1) // CHUNK
        first_chunk = jnp.where((i == 0) & (blk == 0), 2, 0)
        valid = (k_chunk >= jnp.maximum(q_chunk, first_chunk)) & (k_chunk <= q_chunk + 2)

        for j in range(D_ATTN // LANES):
            col = q_s[pl.ds(r0, Q_BLOCK), j * LANES:(j + 1) * LANES]
            for par in range(2):
                s = _dot_nt(col, kvar_s[2 * (j // 2) + par, pl.ds(r0, KEY_WINDOW), :])
                s_s[j, :, par * KEY_WINDOW:(par + 1) * KEY_WINDOW] = jnp.where(valid, s, NEG_INF)
            sm_s[j] = _dot_nt(col, kmeta_s[j // 2])

        lane = lax.broadcasted_iota(jnp.int32, (1, LANES), 1)
        for j in range(D_ATTN // LANES):
            for par in range(2):
                own_meta = (lane >= par * N_META) & (lane < (par + 1) * N_META)
                other = jnp.where(lane == 2 * N_META + par, sinks_ref[2 * j + par], NEG_INF)
                for g in range(Q_BLOCK // SOFTMAX_ROWS):
                    rows = slice(g * SOFTMAX_ROWS, (g + 1) * SOFTMAX_ROWS)
                    s = s_s[j, rows, par * KEY_WINDOW:(par + 1) * KEY_WINDOW]
                    sm = jnp.where(own_meta, sm_s[j, rows, :], other)
                    m = jnp.maximum(jnp.maximum(s[:, :LANES], s[:, LANES:]), sm)
                    m = jnp.max(m, axis=1, keepdims=True)
                    e_s[j, rows, par * E_WIDTH:par * E_WIDTH + KEY_WINDOW] = (
                        jnp.exp(s - m).astype(BF16))
                    e_s[j, rows, par * E_WIDTH + KEY_WINDOW:(par + 1) * E_WIDTH] = (
                        jnp.exp(sm - m).astype(BF16))

        low = lax.broadcasted_iota(jnp.int32, (Q_BLOCK, LANES), 1) < HALF
        for j in range(D_ATTN // LANES):
            o = []
            for par in range(2):
                hp = 2 * (j // 2) + par
                e0 = par * E_WIDTH
                o.append(_dot(e_s[j, :, e0:e0 + KEY_WINDOW], vvar_s[hp, pl.ds(r0, KEY_WINDOW), :])
                         + _dot(e_s[j, :, e0 + KEY_WINDOW:e0 + E_WIDTH], vmeta_s[hp]))
            num = jnp.where(low, o[0], o[1])
            den = pltpu.roll(jnp.where(low, o[1], o[0]), HALF, axis=1)
            a_s[pl.ds(r0, Q_BLOCK), j * LANES:(j + 1) * LANES] = (num / den).astype(BF16)
        return carry

    lax.fori_loop(0, tq // Q_BLOCK, block_body, 0)

    pooled = []
    for g, w in enumerate(POOL_SIZES):
        cs = slice(g * POOL_GROUP, (g + 1) * POOL_GROUP)
        cur = uw_s[POOL_HALO:POOL_HALO + tq, cs]
        acc = cur
        for lag in range(1, w):
            acc = acc + uw_s[POOL_HALO - lag:POOL_HALO - lag + tq, cs]
        d = acc * (1.0 / w) - cur
        pooled.append(_dot(d.astype(BF16), poolw_ref[g]) * pscale_ref[:, cs])
    p = jnp.concatenate(pooled, axis=1).astype(BF16)

    hn = x + _dot(a_s[...], wout_ref[0:D_ATTN, :]) + _dot(p, wout_ref[D_ATTN:, :])
    hout_ref[...] = hn
    xn_ref[...] = _rmsnorm(hn, gffn_ref[...]).astype(xn_ref.dtype)

    kvar_s[:, 0:HALO, :] = kvar_s[:, tq:tq + HALO, :]
    vvar_s[:, 0:HALO, :] = vvar_s[:, tq:tq + HALO, :]
    uw_s[0:POOL_HALO, :] = uw_s[tq:tq + POOL_HALO, :]


def _mix_call(h, batch, sinks, gmix, win, km, vm, um, poolw, pscale, wout, gffn, xn_dtype):
    t, d = h.shape
    tq = MIX_TILE
    nt = t // batch // tq
    assert nt * tq * batch == t
    full = lambda shape: pl.BlockSpec(shape, lambda b, i: (0,) * len(shape))
    row = pl.BlockSpec((tq, d), lambda b, i: (b * nt + i, 0))
    return pl.pallas_call(
        functools.partial(_mix_kernel, tq=tq),
        grid=(batch, nt),
        in_specs=[
            pl.BlockSpec(memory_space=pltpu.SMEM),
            row, full((1, d)), full(win.shape), full(km.shape), full(vm.shape), full(um.shape),
            full(poolw.shape), full((1, D_POOL)), full(wout.shape), full((1, d)),
        ],
        out_specs=[row, row],
        out_shape=[jax.ShapeDtypeStruct((t, d), F32), jax.ShapeDtypeStruct((t, d), xn_dtype)],
        scratch_shapes=[
            pltpu.VMEM((tq, D_ATTN), BF16),
            pltpu.VMEM((4, HALO + tq, LANES), BF16),
            pltpu.VMEM((4, HALO + tq, LANES), BF16),
            pltpu.VMEM((POOL_HALO + tq, D_POOL), F32),
            pltpu.VMEM((tq, D_ATTN), BF16),
            pltpu.VMEM((2, LANES, LANES), BF16),
            pltpu.VMEM((4, LANES, LANES), BF16),
            pltpu.VMEM((D_ATTN // LANES, Q_BLOCK, 2 * KEY_WINDOW), F32),
            pltpu.VMEM((D_ATTN // LANES, Q_BLOCK, LANES), F32),
            pltpu.VMEM((D_ATTN // LANES, Q_BLOCK, 2 * E_WIDTH), BF16),
        ],
        compiler_params=pltpu.CompilerParams(
            dimension_semantics=("arbitrary", "arbitrary"), vmem_limit_bytes=VMEM_LIMIT),
        name="mix",
    )(sinks, h, gmix, win, km, vm, um, poolw, pscale, wout, gffn)


def _meta_kernel(sinks_ref, h_ref, gmix_ref, win_ref, poolw_ref, pscale_ref, wout_ref, gffn_ref,
                 kvu_ref, hout_ref, xn_ref, uw_s):
    x = h_ref[...]
    xn = _rmsnorm(x, gmix_ref[...]).astype(BF16)
    q = (_dot(xn, win_ref[:, 0:D_ATTN]) * (HEAD_DIM ** -0.5)).astype(BF16)
    k = _dot(xn, win_ref[:, D_ATTN:D_ATTN + D_KV])
    v = _dot(xn, win_ref[:, D_ATTN + D_KV:D_ATTN + 2 * D_KV])
    u = _dot(xn, win_ref[:, D_ATTN + 2 * D_KV:])
    kvu_ref[:, 0:D_KV] = k
    kvu_ref[:, D_KV:2 * D_KV] = v
    kvu_ref[:, 2 * D_KV:] = u

    k_var = _head_variants(k)
    v_var = _head_variants(v)
    cols = []
    for j in range(D_ATTN // LANES):
        col = q[:, j * LANES:(j + 1) * LANES]
        acc = None
        for par in range(2):
            var = 2 * (j // 2) + par
            o = _sink_softmax_pv((_dot_nt(col, k_var[var]),), (v_var[var],),
                                 sinks_ref[2 * j + par])
            acc = o if acc is None else acc + o
        cols.append(acc)
    a = jnp.concatenate(cols, axis=1).astype(BF16)

    uw_s[0:POOL_HALO, :] = jnp.zeros((POOL_HALO, D_POOL), F32)
    uw_s[POOL_HALO:, :] = u
    pos = lax.broadcasted_iota(jnp.int32, (N_META, 1), 0)
    pooled = []
    for g, w in enumerate(POOL_SIZES):
        cs = slice(g * POOL_GROUP, (g + 1) * POOL_GROUP)
        cur = uw_s[POOL_HALO:, cs]
        acc = cur
        for lag in range(1, w):
            acc = acc + uw_s[POOL_HALO - lag:POOL_HALO - lag + N_META, cs]
        count = jnp.minimum(pos + 1, w).astype(F32)
        d = acc / count - cur
        pooled.append(_dot(d.astype(BF16), poolw_ref[g]) * pscale_ref[:, cs])
    p = jnp.concatenate(pooled, axis=1).astype(BF16)

    hn = x + _dot(a, wout_ref[0:D_ATTN, :]) + _dot(p, wout_ref[D_ATTN:, :])
    hout_ref[...] = hn
    xn_ref[...] = _rmsnorm(hn, gffn_ref[...]).astype(xn_ref.dtype)


def _meta_call(h, sinks, gmix, win, poolw, pscale, wout, gffn):
    n, d = h.shape
    vmem = pl.BlockSpec(memory_space=pltpu.VMEM)
    return pl.pallas_call(
        _meta_kernel,
        in_specs=[pl.BlockSpec(memory_space=pltpu.SMEM)] + [vmem] * 7,
        out_specs=[vmem, vmem, vmem],
        out_shape=[jax.ShapeDtypeStruct((n, 2 * D_KV + D_POOL), F32),
                   jax.ShapeDtypeStruct((n, d), F32),
                   jax.ShapeDtypeStruct((n, d), BF16)],
        scratch_shapes=[pltpu.VMEM((POOL_HALO + N_META, D_POOL), F32)],
        compiler_params=pltpu.CompilerParams(vmem_limit_bytes=VMEM_LIMIT),
        name="meta_mix",
    )(sinks, h, gmix, win, poolw, pscale, wout, gffn)


def _ffn_kernel(tab_ref, na_ref, x_ref, *refs, masked):
    res_ref = refs[0] if len(refs) == 5 else None
    wg_ref, wu_ref, wd_ref, out_ref = refs[-4:]
    i, j = pl.program_id(0), pl.program_id(1)

    @pl.when(i < na_ref[0])
    def _():
        @pl.when(j == 0)
        def _():
            out_ref[...] = jnp.zeros_like(out_ref) if res_ref is None else res_ref[...]

        x = x_ref[...]
        if masked:
            r = lax.broadcasted_iota(jnp.int32, (x.shape[0], 1), 0)
            x = jnp.where(r < tab_ref[2, i], x, 0.0)
        x = x.astype(BF16)
        gate = _dot(x, wg_ref[...])
        up = _dot(x, wu_ref[...])
        mid = (gate * jax.nn.sigmoid(gate) * up).astype(BF16)
        out_ref[...] += _dot(mid, wd_ref[...])


def _ffn_call(x, res, wg, wu, wd, table, n_active, tm, masked):
    rows, d = x.shape
    ff = wg.shape[-1]
    n_grid = table.shape[1]
    nj = ff // FFN_BLOCK
    assert rows % tm == 0 and nj * FFN_BLOCK == ff

    def live(i, na):
        return jnp.maximum(jnp.minimum(i, na[0] - 1), 0)

    def ff_block(i, j, na):
        return jnp.where(i < na[0], j, nj - 1)

    row = pl.BlockSpec((tm, d), lambda i, j, tab, na: (tab[1, live(i, na)], 0))
    col_w = pl.BlockSpec((None, d, FFN_BLOCK),
                         lambda i, j, tab, na: (tab[0, live(i, na)], 0, ff_block(i, j, na)))
    row_w = pl.BlockSpec((None, FFN_BLOCK, d),
                         lambda i, j, tab, na: (tab[0, live(i, na)], ff_block(i, j, na), 0))
    acts = (x,) if res is None else (x, res)
    return pl.pallas_call(
        functools.partial(_ffn_kernel, masked=masked),
        grid_spec=pltpu.PrefetchScalarGridSpec(
            num_scalar_prefetch=2, grid=(n_grid, nj),
            in_specs=[row] * len(acts) + [col_w, col_w, row_w], out_specs=row),
        out_shape=jax.ShapeDtypeStruct((rows, d), F32),
        compiler_params=pltpu.CompilerParams(
            dimension_semantics=("arbitrary", "arbitrary"), vmem_limit_bytes=VMEM_LIMIT),
        name="ffn",
    )(table, n_active, *acts, wg, wu, wd)


def _dense_table(n_tiles, tm):
    ids = jnp.arange(n_tiles, dtype=jnp.int32)
    return (jnp.stack([jnp.zeros_like(ids), ids, jnp.full_like(ids, tm)]),
            jnp.full((1,), n_tiles, jnp.int32))


def _split_bf16(x):
    hi = x.astype(BF16)
    return hi, (x - hi.astype(F32)).astype(BF16)


def _for_each_routed_row(n_rows, pos_ref, fn):
    chunks = n_rows // LANES
    for c in range(chunks):
        def body(gl, carry, c=c):
            for u in range(8):
                for kk in range(2):
                    fn(c * (LANES // 8) + gl, u, kk, pos_ref[kk * chunks + c, gl * 8 + u])
            return carry

        lax.fori_loop(0, LANES // 8, body, 0)


def _route_dispatch_kernel(x_ref, wr_ref, tri_ref, xs_ref, pos_ref, gate_ref, counts_ref,
                           run_s, pos_sm, sem_pos, sem_rows, *, tr, cap):
    i = pl.program_id(0)

    @pl.when(i == 0)
    def _():
        run_s[...] = jnp.zeros_like(run_s)

    x_hi, x_lo = _split_bf16(x_ref[...].reshape(tr, -1))
    w_hi, w_lo = _split_bf16(wr_ref[...])
    logits = _dot_nt(w_hi, x_hi) + (_dot_nt(w_lo, x_hi) + _dot_nt(w_hi, x_lo))

    expert = lax.broadcasted_iota(jnp.int32, logits.shape, 0).astype(F32)
    none = float(N_EXPERTS)
    m1 = jnp.max(logits, axis=0, keepdims=True)
    e1 = jnp.min(jnp.where(logits == m1, expert, none), axis=0, keepdims=True)
    rest = jnp.where(expert == e1, -jnp.inf, logits)
    m2 = jnp.max(rest, axis=0, keepdims=True)
    e2 = jnp.min(jnp.where(rest == m2, expert, none), axis=0, keepdims=True)
    t = jnp.exp(m2 - m1)
    gate_ref[0:1, :] = 1.0 / (1.0 + t)
    gate_ref[1:2, :] = t / (1.0 + t)

    chosen = ((expert == e1) | (expert == e2)).astype(F32)
    before = _dot(chosen.astype(BF16), tri_ref[...]) + run_s[:, 0:1]
    pieces = []
    for e in (e1, e2):
        rank = jnp.sum(jnp.where(expert == e, before, 0.0), axis=0, keepdims=True)
        pos = (e * float(cap) + rank).astype(jnp.int32)
        pieces += [pos[:, c * LANES:(c + 1) * LANES] for c in range(tr // LANES)]
    pos_ref[0] = jnp.concatenate(pieces, axis=0)
    run_s[...] += jnp.sum(chosen, axis=1, keepdims=True)
    counts_ref[...] = run_s[...].astype(jnp.int32)

    to_smem = pltpu.make_async_copy(pos_ref, pos_sm, sem_pos)
    to_smem.start()
    to_smem.wait()
    _for_each_routed_row(tr, pos_sm.at[0], lambda g, u, kk, p: pltpu.make_async_copy(
        x_ref.at[g, pl.ds(u, 1)], xs_ref.at[p >> 3, pl.ds(p & 7, 1)], sem_rows).start())
    for _ in range(2):
        pltpu.make_async_copy(x_ref, xs_ref.at[pl.ds(0, tr // 8)], sem_rows).wait()


def _route_dispatch_call(xn, w_router):
    t, d = xn.shape
    tr = ROW_TILE
    n = t // tr
    tri = (lax.broadcasted_iota(jnp.int32, (tr, tr), 0)
           < lax.broadcasted_iota(jnp.int32, (tr, tr), 1)).astype(BF16)
    return pl.pallas_call(
        functools.partial(_route_dispatch_kernel, tr=tr, cap=t),
        grid=(n,),
        in_specs=[pl.BlockSpec((tr // 8, 8, d), lambda i: (i, 0, 0)),
                  pl.BlockSpec((N_EXPERTS, d), lambda i: (0, 0)),
                  pl.BlockSpec((tr, tr), lambda i: (0, 0))],
        out_specs=[pl.BlockSpec(memory_space=pl.ANY),
                   pl.BlockSpec((1, 2 * tr // LANES, LANES), lambda i: (i, 0, 0)),
                   pl.BlockSpec((None, 2, tr), lambda i: (i, 0, 0)),
                   pl.BlockSpec((N_EXPERTS, LANES), lambda i: (0, 0))],
        out_shape=[jax.ShapeDtypeStruct((N_EXPERTS * t // 8, 8, d), F32),
                   jax.ShapeDtypeStruct((n, 2 * tr // LANES, LANES), jnp.int32),
                   jax.ShapeDtypeStruct((n, 2, tr), F32),
                   jax.ShapeDtypeStruct((N_EXPERTS, LANES), jnp.int32)],
        scratch_shapes=[pltpu.VMEM((N_EXPERTS, LANES), F32),
                        pltpu.SMEM((1, 2 * tr // LANES, LANES), jnp.int32),
                        pltpu.SemaphoreType.DMA, pltpu.SemaphoreType.DMA],
        compiler_params=pltpu.CompilerParams(
            dimension_semantics=("arbitrary",), vmem_limit_bytes=VMEM_LIMIT),
        name="route_dispatch",
    )(xn.reshape(t // 8, 8, d), w_router.T, tri)


def _combine_kernel(pos_ref, pos_next_ref, h_ref, gate_ref, gfin_ref, y_ref, out_ref, buf, sem,
                    *, tc, n):
    i = pl.program_id(0)
    slot = i % 2

    def issue(p_ref, s):
        _for_each_routed_row(tc, p_ref, lambda g, u, kk, p: pltpu.make_async_copy(
            y_ref.at[p >> 3, pl.ds(p & 7, 1)], buf.at[s, kk, g, pl.ds(u, 1)], sem.at[s]).start())

    @pl.when(i == 0)
    def _():
        issue(pos_ref, 0)

    @pl.when(i + 1 < n)
    def _():
        issue(pos_next_ref, 1 - slot)

    for kk in range(2):
        pltpu.make_async_copy(y_ref.at[pl.ds(0, tc // 8)], buf.at[slot, kk], sem.at[slot]).wait()
    gates = gate_ref[...]
    hn = h_ref[...] + (gates[:, :, 0:1] * buf[slot, 0] + gates[:, :, 1:2] * buf[slot, 1])
    out_ref[...] = _rmsnorm(hn, gfin_ref[...])


def _combine_call(h, y, pos, gates, gfin):
    t, d = h.shape
    tc = ROW_TILE
    n = t // tc
    smem_pos = lambda index_map: pl.BlockSpec((None, 2 * tc // LANES, LANES), index_map,
                                              memory_space=pltpu.SMEM)
    rows = lambda width: pl.BlockSpec((tc // 8, 8, width), lambda i: (i, 0, 0))
    out = pl.pallas_call(
        functools.partial(_combine_kernel, tc=tc, n=n),
        grid=(n,),
        in_specs=[smem_pos(lambda i: (i, 0, 0)),
                  smem_pos(lambda i: (jnp.minimum(i + 1, n - 1), 0, 0)),
                  rows(d), rows(2),
                  pl.BlockSpec((1, d), lambda i: (0, 0)),
                  pl.BlockSpec(memory_space=pl.ANY)],
        out_specs=rows(d),
        out_shape=jax.ShapeDtypeStruct((t // 8, 8, d), F32),
        scratch_shapes=[pltpu.VMEM((2, 2, tc // 8, 8, d), F32), pltpu.SemaphoreType.DMA((2,))],
        compiler_params=pltpu.CompilerParams(
            dimension_semantics=("arbitrary",), vmem_limit_bytes=VMEM_LIMIT),
        name="combine",
    )(pos, pos, h.reshape(t // 8, 8, d), gates.reshape(t // 8, 8, 2), gfin,
      y.reshape(-1, 8, d))
    return out.reshape(t, d)


def _moe(h, xn, w_router, wg, wu, wd, gfin):
    t, d = h.shape
    tm = FFN_TILE
    xs, pos, gates, counts = _route_dispatch_call(xn, w_router)
    counts = counts[:, 0]

    n_grid = 2 * t // tm + N_EXPERTS
    tiles_of = (counts + tm - 1) // tm
    tile_end = jnp.cumsum(tiles_of)
    ids = jnp.arange(n_grid, dtype=jnp.int32)
    expert = jnp.minimum(jnp.sum(ids[:, None] >= tile_end[None, :], axis=1), N_EXPERTS - 1)
    local = ids - (tile_end - tiles_of)[expert]
    table = jnp.stack([expert, expert * (t // tm) + local,
                       jnp.clip(counts[expert] - local * tm, 0, tm)]).astype(jnp.int32)
    y = _ffn_call(xs.reshape(-1, d), None, wg, wu, wd, table, tile_end[-1:].astype(jnp.int32),
                  tm, True)
    return _combine_call(h, y, pos, gates.transpose(0, 2, 1).reshape(t, 2), gfin)


def kernel(x, meta_tokens, norm_mix_g, w_in, attn_sinks, pool_w, pool_scale, w_out, norm_ffn_g,
           dense_w_gate, dense_w_up, dense_w_down, moe_w_router, moe_w_gate, moe_w_up,
           moe_w_down, final_norm_g):
    b, s, d = x.shape
    h = x.reshape(b * s, d)
    row = lambda a: a.reshape(1, -1)
    w_in_b, w_out_b, pool_w_b = w_in.astype(BF16), w_out.astype(BF16), pool_w.astype(BF16)

    kvu0, hm, xnm = _meta_call(meta_tokens, attn_sinks[0], row(norm_mix_g[0]), w_in_b[0],
                               pool_w_b[0], row(pool_scale[0]), w_out_b[0], row(norm_ffn_g[0]))
    dense = (dense_w_gate.astype(BF16), dense_w_up.astype(BF16), dense_w_down.astype(BF16))
    hm = _ffn_call(xnm, hm, *dense, *_dense_table(1, N_META), N_META, False)
    kvu1, _, _ = _meta_call(hm, attn_sinks[1], row(norm_mix_g[1]), w_in_b[1], pool_w_b[1],
                            row(pool_scale[1]), w_out_b[1], row(norm_ffn_g[1]))

    def mix(h, layer, kvu, xn_dtype):
        return _mix_call(h, b, attn_sinks[layer], row(norm_mix_g[layer]), w_in_b[layer],
                         kvu[:, 0:D_KV], kvu[:, D_KV:2 * D_KV], kvu[:, 2 * D_KV:],
                         pool_w_b[layer], row(pool_scale[layer]), w_out_b[layer],
                         row(norm_ffn_g[layer]), xn_dtype)

    h, xn = mix(h, 0, kvu0, BF16)
    h = _ffn_call(xn, h, *dense, *_dense_table(b * s // FFN_TILE, FFN_TILE), FFN_TILE, False)
    h, xn = mix(h, 1, kvu1, F32)
    out = _moe(h, xn, moe_w_router[0], moe_w_gate[0].astype(BF16), moe_w_up[0].astype(BF16),
               moe_w_down[0].astype(BF16), row(final_norm_g))
    return out.reshape(b, s, d)
```

```python
import functools

import jax
import jax.numpy as jnp
from jax import lax
from jax.experimental import pallas as pl
from jax.experimental.pallas import tpu as pltpu

F32 = jnp.float32
BF16 = jnp.bfloat16

CHUNK = 64
N_META = 16
HEAD_DIM = 64
N_Q_HEADS = 8
D_ATTN = 512
D_KV = 128
D_POOL = 512
POOL_SIZES = (2, 4, 8, 16)
POOL_GROUP = 128
N_EXPERTS = 8
EPS = 1e-5
NEG_INF = -1e30

LANES = 128
HALF = LANES // 2
Q_BLOCK = 2 * CHUNK
KEY_WINDOW = 4 * CHUNK
HALO = 2 * CHUNK
E_WIDTH = KEY_WINDOW + LANES
SOFTMAX_ROWS = 32
POOL_HALO = max(POOL_SIZES)
VMEM_LIMIT = 56 * 1024 * 1024

MIX_TILE = 512
FFN_TILE = 512
FFN_SUBTILE = 512
FFN_BLOCK = 1792
ROW_TILE = 512


def _rmsnorm(x, g):
    return x * lax.rsqrt(jnp.mean(x * x, axis=-1, keepdims=True) + EPS) * g


def _dot(a, b):
    return jnp.dot(a, b, preferred_element_type=F32)


def _dot_nt(a, b):
    return lax.dot_general(a, b, (((1,), (1,)), ((), ())), preferred_element_type=F32)


def _head_variants(t, fill=0.0):
    low = lax.broadcasted_iota(jnp.int32, t.shape, 1) < HALF
    t_r = pltpu.roll(t, HALF, axis=1)
    other = jnp.full_like(t, fill)
    return (jnp.where(low, t, other).astype(BF16),
            jnp.where(low, other, t_r).astype(BF16),
            jnp.where(low, t_r, other).astype(BF16),
            jnp.where(low, other, t).astype(BF16))


def _sink_softmax_pv(s_parts, v_parts, sink):
    m = sink
    for s in s_parts:
        m = jnp.maximum(m, jnp.max(s, axis=1, keepdims=True))
    den = jnp.exp(sink - m)
    o = None
    for s, v in zip(s_parts, v_parts):
        e = jnp.exp(s - m)
        den = den + jnp.sum(e, axis=1, keepdims=True)
        pv = _dot(e.astype(BF16), v)
        o = pv if o is None else o + pv
    return o / den


def _meta_tiles(km, vm):
    km_var = _head_variants(km)
    vm_var = _head_variants(vm, 1.0)
    zeros = lambda n: jnp.zeros((n, LANES), BF16)
    row = lax.broadcasted_iota(jnp.int32, (N_META, LANES), 0)
    low = lax.broadcasted_iota(jnp.int32, (N_META, LANES), 1) < HALF
    rest = LANES - 3 * N_META
    k_tiles, v_tiles = [], []
    for h in range(2):
        k_tiles.append(jnp.concatenate([km_var[2 * h], km_var[2 * h + 1], zeros(LANES - 2 * N_META)]))
        for par in range(2):
            den_half = low if par == 1 else jnp.logical_not(low)
            sink_rows = jnp.where((row == par) & den_half, 1.0, 0.0).astype(BF16)
            v_tiles.append(jnp.concatenate(
                [vm_var[2 * h] if par == 0 else zeros(N_META),
                 vm_var[2 * h + 1] if par == 1 else zeros(N_META), sink_rows, zeros(rest)]))
    return k_tiles, v_tiles


def _mix_kernel(sinks_ref, h_ref, gmix_ref, win_ref, km_ref, vm_ref, um_ref, poolw_ref,
                pscale_ref, wout_ref, gffn_ref, hout_ref, xn_ref,
                q_s, kvar_s, vvar_s, uw_s, a_s, kmeta_s, vmeta_s, s_s, sm_s, e_s, *, tq):
    i = pl.program_id(1)

    @pl.when(i == 0)
    def _():
        kvar_s[:, 0:HALO, :] = jnp.zeros((4, HALO, LANES), BF16)
        vvar_s[:, 0:HALO, :] = jnp.zeros((4, HALO, LANES), BF16)
        uw_s[0:POOL_HALO, :] = um_ref[...]

    x = h_ref[...]
    xn = _rmsnorm(x, gmix_ref[...]).astype(BF16)
    q_s[...] = (_dot(xn, win_ref[:, 0:D_ATTN]) * (HEAD_DIM ** -0.5)).astype(BF16)
    k = _dot(xn, win_ref[:, D_ATTN:D_ATTN + D_KV])
    v = _dot(xn, win_ref[:, D_ATTN + D_KV:D_ATTN + 2 * D_KV])
    uw_s[POOL_HALO:POOL_HALO + tq, :] = _dot(xn, win_ref[:, D_ATTN + 2 * D_KV:])
    for idx, (kk, vv) in enumerate(zip(_head_variants(k), _head_variants(v, 1.0))):
        kvar_s[idx, HALO:HALO + tq, :] = kk
        vvar_s[idx, HALO:HALO + tq, :] = vv
    k_tiles, v_tiles = _meta_tiles(km_ref[...], vm_ref[...])
    for h in range(2):
        kmeta_s[h] = k_tiles[h]
    for hp in range(4):
        vmeta_s[hp] = v_tiles[hp]

    def block_body(blk, carry):
        r0 = pl.multiple_of(blk * Q_BLOCK, Q_BLOCK)
        q_chunk = lax.broadcasted_iota(jnp.int32, (Q_BLOCK, KEY_WINDOW), 0) // CHUNK
        k_chunk = lax.broadcasted_iota(jnp.int32, (Q_BLOCK, KEY_WINDOW), 1) // CHUNK
        first_chunk = jnp.where((i == 0) & (blk == 0), 2, 0)
        valid = (k_chunk >= jnp.maximum(q_chunk, first_chunk)) & (k_chunk <= q_chunk + 2)

        for j in range(D_ATTN // LANES):
            col = q_s[pl.ds(r0, Q_BLOCK), j * LANES:(j + 1) * LANES]
            for par in range(2):
                s = _dot_nt(col, kvar_s[2 * (j // 2) + par, pl.ds(r0, KEY_WINDOW), :])
                s_s[j, :, par * KEY_WINDOW:(par + 1) * KEY_WINDOW] = jnp.where(valid, s, NEG_INF)
            sm_s[j] = _dot_nt(col, kmeta_s[j // 2])

        lane = lax.broadcasted_iota(jnp.int32, (1, LANES), 1)
        for j in range(D_ATTN // LANES):
            for par in range(2):
                own_meta = (lane >= par * N_META) & (lane < (par + 1) * N_META)
                other = jnp.where(lane == 2 * N_META + par, sinks_ref[2 * j + par], NEG_INF)
                for g in range(Q_BLOCK // SOFTMAX_ROWS):
                    rows = slice(g * SOFTMAX_ROWS, (g + 1) * SOFTMAX_ROWS)
                    s = s_s[j, rows, par * KEY_WINDOW:(par + 1) * KEY_WINDOW]
                    sm = jnp.where(own_meta, sm_s[j, rows, :], other)
                    m = jnp.maximum(jnp.maximum(s[:, :LANES], s[:, LANES:]), sm)
                    m = jnp.max(m, axis=1, keepdims=True)
                    e_s[j, rows, par * E_WIDTH:par * E_WIDTH + KEY_WINDOW] = (
                        jnp.exp(s - m).astype(BF16))
                    e_s[j, rows, par * E_WIDTH + KEY_WINDOW:(par + 1) * E_WIDTH] = (
                        jnp.exp(sm - m).astype(BF16))

        low = lax.broadcasted_iota(jnp.int32, (Q_BLOCK, LANES), 1) < HALF
        for j in range(D_ATTN // LANES):
            o = []
            for par in range(2):
                hp = 2 * (j // 2) + par
                e0 = par * E_WIDTH
                o.append(_dot(e_s[j, :, e0:e0 + KEY_WINDOW], vvar_s[hp, pl.ds(r0, KEY_WINDOW), :])
                         + _dot(e_s[j, :, e0 + KEY_WINDOW:e0 + E_WIDTH], vmeta_s[hp]))
            num = jnp.where(low, o[0], o[1])
            den = pltpu.roll(jnp.where(low, o[1], o[0]), HALF, axis=1)
            a_s[pl.ds(r0, Q_BLOCK), j * LANES:(j + 1) * LANES] = (num / den).astype(BF16)
        return carry

    lax.fori_loop(0, tq // Q_BLOCK, block_body, 0)

    pooled = []
    for g, w in enumerate(POOL_SIZES):
        cs = slice(g * POOL_GROUP, (g + 1) * POOL_GROUP)
        cur = uw_s[POOL_HALO:POOL_HALO + tq, cs]
        acc = cur
        for lag in range(1, w):
            acc = acc + uw_s[POOL_HALO - lag:POOL_HALO - lag + tq, cs]
        d = acc * (1.0 / w) - cur
        pooled.append(_dot(d.astype(BF16), poolw_ref[g]) * pscale_ref[:, cs])
    p = jnp.concatenate(pooled, axis=1).astype(BF16)

    hn = x + _dot(a_s[...], wout_ref[0:D_ATTN, :]) + _dot(p, wout_ref[D_ATTN:, :])
    hout_ref[...] = hn
    xn_ref[...] = _rmsnorm(hn, gffn_ref[...]).astype(xn_ref.dtype)

    kvar_s[:, 0:HALO, :] = kvar_s[:, tq:tq + HALO, :]
    vvar_s[:, 0:HALO, :] = vvar_s[:, tq:tq + HALO, :]
    uw_s[0:POOL_HALO, :] = uw_s[tq:tq + POOL_HALO, :]


def _mix_call(h, batch, sinks, gmix, win, km, vm, um, poolw, pscale, wout, gffn, xn_dtype):
    t, d = h.shape
    tq = MIX_TILE
    nt = t // batch // tq
    assert nt * tq * batch == t
    full = lambda shape: pl.BlockSpec(shape, lambda b, i: (0,) * len(shape))
    row = pl.BlockSpec((tq, d), lambda b, i: (b * nt + i, 0))
    return pl.pallas_call(
        functools.partial(_mix_kernel, tq=tq),
        grid=(batch, nt),
        in_specs=[
            pl.BlockSpec(memory_space=pltpu.SMEM),
            row, full((1, d)), full(win.shape), full(km.shape), full(vm.shape), full(um.shape),
            full(poolw.shape), full((1, D_POOL)), full(wout.shape), full((1, d)),
        ],
        out_specs=[row, row],
        out_shape=[jax.ShapeDtypeStruct((t, d), F32), jax.ShapeDtypeStruct((t, d), xn_dtype)],
        scratch_shapes=[
            pltpu.VMEM((tq, D_ATTN), BF16),
            pltpu.VMEM((4, HALO + tq, LANES), BF16),
            pltpu.VMEM((4, HALO + tq, LANES), BF16),
            pltpu.VMEM((POOL_HALO + tq, D_POOL), F32),
            pltpu.VMEM((tq, D_ATTN), BF16),
            pltpu.VMEM((2, LANES, LANES), BF16),
            pltpu.VMEM((4, LANES, LANES), BF16),
            pltpu.VMEM((D_ATTN // LANES, Q_BLOCK, 2 * KEY_WINDOW), F32),
            pltpu.VMEM((D_ATTN // LANES, Q_BLOCK, LANES), F32),
            pltpu.VMEM((D_ATTN // LANES, Q_BLOCK, 2 * E_WIDTH), BF16),
        ],
        compiler_params=pltpu.CompilerParams(
            dimension_semantics=("arbitrary", "arbitrary"), vmem_limit_bytes=VMEM_LIMIT),
        name="mix",
    )(sinks, h, gmix, win, km, vm, um, poolw, pscale, wout, gffn)


def _meta_kernel(sinks_ref, h_ref, gmix_ref, win_ref, poolw_ref, pscale_ref, wout_ref, gffn_ref,
                 kvu_ref, hout_ref, xn_ref, uw_s):
    x = h_ref[...]
    xn = _rmsnorm(x, gmix_ref[...]).astype(BF16)
    q = (_dot(xn, win_ref[:, 0:D_ATTN]) * (HEAD_DIM ** -0.5)).astype(BF16)
    k = _dot(xn, win_ref[:, D_ATTN:D_ATTN + D_KV])
    v = _dot(xn, win_ref[:, D_ATTN + D_KV:D_ATTN + 2 * D_KV])
    u = _dot(xn, win_ref[:, D_ATTN + 2 * D_KV:])
    kvu_ref[:, 0:D_KV] = k
    kvu_ref[:, D_KV:2 * D_KV] = v
    kvu_ref[:, 2 * D_KV:] = u

    k_var = _head_variants(k)
    v_var = _head_variants(v)
    cols = []
    for j in range(D_ATTN // LANES):
        col = q[:, j * LANES:(j + 1) * LANES]
        acc = None
        for par in range(2):
            var = 2 * (j // 2) + par
            o = _sink_softmax_pv((_dot_nt(col, k_var[var]),), (v_var[var],),
                                 sinks_ref[2 * j + par])
            acc = o if acc is None else acc + o
        cols.append(acc)
    a = jnp.concatenate(cols, axis=1).astype(BF16)

    uw_s[0:POOL_HALO, :] = jnp.zeros((POOL_HALO, D_POOL), F32)
    uw_s[POOL_HALO:, :] = u
    pos = lax.broadcasted_iota(jnp.int32, (N_META, 1), 0)
    pooled = []
    for g, w in enumerate(POOL_SIZES):
        cs = slice(g * POOL_GROUP, (g + 1) * POOL_GROUP)
        cur = uw_s[POOL_HALO:, cs]
        acc = cur
        for lag in range(1, w):
            acc = acc + uw_s[POOL_HALO - lag:POOL_HALO - lag + N_META, cs]
        count = jnp.minimum(pos + 1, w).astype(F32)
        d = acc / count - cur
        pooled.append(_dot(d.astype(BF16), poolw_ref[g]) * pscale_ref[:, cs])
    p = jnp.concatenate(pooled, axis=1).astype(BF16)

    hn = x + _dot(a, wout_ref[0:D_ATTN, :]) + _dot(p, wout_ref[D_ATTN:, :])
    hout_ref[...] = hn
    xn_ref[...] = _rmsnorm(hn, gffn_ref[...]).astype(xn_ref.dtype)


def _meta_call(h, sinks, gmix, win, poolw, pscale, wout, gffn):
    n, d = h.shape
    vmem = pl.BlockSpec(memory_space=pltpu.VMEM)
    return pl.pallas_call(
        _meta_kernel,
        in_specs=[pl.BlockSpec(memory_space=pltpu.SMEM)] + [vmem] * 7,
        out_specs=[vmem, vmem, vmem],
        out_shape=[jax.ShapeDtypeStruct((n, 2 * D_KV + D_POOL), F32),
                   jax.ShapeDtypeStruct((n, d), F32),
                   jax.ShapeDtypeStruct((n, d), BF16)],
        scratch_shapes=[pltpu.VMEM((POOL_HALO + N_META, D_POOL), F32)],
        compiler_params=pltpu.CompilerParams(vmem_limit_bytes=VMEM_LIMIT),
        name="meta_mix",
    )(sinks, h, gmix, win, poolw, pscale, wout, gffn)


def _ffn_kernel(tab_ref, na_ref, x_ref, *refs, masked):
    res_ref = refs[0] if len(refs) == 5 else None
    wg_ref, wu_ref, wd_ref, out_ref = refs[-4:]
    i, j = pl.program_id(0), pl.program_id(1)

    @pl.when(i < na_ref[0])
    def _():
        @pl.when(j == 0)
        def _():
            out_ref[...] = jnp.zeros_like(out_ref) if res_ref is None else res_ref[...]

        wg, wu, wd = (w[...].astype(BF16) for w in (wg_ref, wu_ref, wd_ref))
        valid = tab_ref[2, i]
        tm = x_ref.shape[0]
        sub = min(tm, FFN_SUBTILE)
        for r0 in range(0, tm, sub):
            def part(r0=r0):
                x = x_ref[r0:r0 + sub, :]
                if masked:
                    r = r0 + lax.broadcasted_iota(jnp.int32, (sub, 1), 0)
                    x = jnp.where(r < valid, x, 0.0)
                x = x.astype(BF16)
                gate = _dot(x, wg)
                up = _dot(x, wu)
                mid = (gate * jax.nn.sigmoid(gate) * up).astype(BF16)
                out_ref[r0:r0 + sub, :] += _dot(mid, wd)

            if masked and r0 > 0:
                pl.when(valid > r0)(part)
            else:
                part()


def _ffn_call(x, res, wg, wu, wd, table, n_active, tm, masked):
    rows, d = x.shape
    ff = wg.shape[-1]
    n_grid = table.shape[1]
    nj = ff // FFN_BLOCK
    assert rows % tm == 0 and nj * FFN_BLOCK == ff

    def live(i, na):
        return jnp.maximum(jnp.minimum(i, na[0] - 1), 0)

    def ff_block(i, j, na):
        return jnp.where(i < na[0], j, nj - 1)

    row = pl.BlockSpec((tm, d), lambda i, j, tab, na: (tab[1, live(i, na)], 0))
    col_w = pl.BlockSpec((None, d, FFN_BLOCK),
                         lambda i, j, tab, na: (tab[0, live(i, na)], 0, ff_block(i, j, na)))
    row_w = pl.BlockSpec((None, FFN_BLOCK, d),
                         lambda i, j, tab, na: (tab[0, live(i, na)], ff_block(i, j, na), 0))
    acts = (x,) if res is None else (x, res)
    return pl.pallas_call(
        functools.partial(_ffn_kernel, masked=masked),
        grid_spec=pltpu.PrefetchScalarGridSpec(
            num_scalar_prefetch=2, grid=(n_grid, nj),
            in_specs=[row] * len(acts) + [col_w, col_w, row_w], out_specs=row),
        out_shape=jax.ShapeDtypeStruct((rows, d), F32),
        compiler_params=pltpu.CompilerParams(
            dimension_semantics=("arbitrary", "arbitrary"), vmem_limit_bytes=VMEM_LIMIT),
        name="ffn",
    )(table, n_active, *acts, wg, wu, wd)


def _dense_table(n_tiles, tm):
    ids = jnp.arange(n_tiles, dtype=jnp.int32)
    return (jnp.stack([jnp.zeros_like(ids), ids, jnp.full_like(ids, tm)]),
            jnp.full((1,), n_tiles, jnp.int32))


def _split_bf16(x):
    hi = x.astype(BF16)
    return hi, (x - hi.astype(F32)).astype(BF16)


def _for_each_routed_row(n_rows, pos_ref, fn):
    chunks = n_rows // LANES
    for r in range(n_rows):
        for kk in range(2):
            fn(r, kk, pos_ref[kk * chunks + r // LANES, r % LANES])


def _route_dispatch_kernel(x_ref, wr_ref, tri_ref, xs_ref, pos_ref, gate_ref, counts_ref,
                           run_s, pos_sm, sem_pos, sem_rows, *, tr, cap):
    i = pl.program_id(0)

    @pl.when(i == 0)
    def _():
        run_s[...] = jnp.zeros_like(run_s)

    x_hi, x_lo = _split_bf16(x_ref[...])
    w_hi, w_lo = _split_bf16(wr_ref[...])
    logits = _dot_nt(w_hi, x_hi) + (_dot_nt(w_lo, x_hi) + _dot_nt(w_hi, x_lo))

    expert = lax.broadcasted_iota(jnp.int32, logits.shape, 0).astype(F32)
    none = float(N_EXPERTS)
    m1 = jnp.max(logits, axis=0, keepdims=True)
    e1 = jnp.min(jnp.where(logits == m1, expert, none), axis=0, keepdims=True)
    rest = jnp.where(expert == e1, -jnp.inf, logits)
    m2 = jnp.max(rest, axis=0, keepdims=True)
    e2 = jnp.min(jnp.where(rest == m2, expert, none), axis=0, keepdims=True)
    t = jnp.exp(m2 - m1)
    gate_ref[0:1, :] = 1.0 / (1.0 + t)
    gate_ref[1:2, :] = t / (1.0 + t)

    chosen = ((expert == e1) | (expert == e2)).astype(F32)
    before = _dot(chosen.astype(BF16), tri_ref[...]) + run_s[:, 0:1]
    pieces = []
    for e in (e1, e2):
        rank = jnp.sum(jnp.where(expert == e, before, 0.0), axis=0, keepdims=True)
        pos = (e * float(cap) + rank).astype(jnp.int32)
        pieces += [pos[:, c * LANES:(c + 1) * LANES] for c in range(tr // LANES)]
    pos_ref[0] = jnp.concatenate(pieces, axis=0)
    run_s[...] += jnp.sum(chosen, axis=1, keepdims=True)
    counts_ref[...] = run_s[...].astype(jnp.int32)

    to_smem = pltpu.make_async_copy(pos_ref, pos_sm, sem_pos)
    to_smem.start()
    to_smem.wait()
    _for_each_routed_row(tr, pos_sm.at[0], lambda r, kk, p: pltpu.make_async_copy(
        x_ref.at[pl.ds(r, 1)], xs_ref.at[pl.ds(p, 1)], sem_rows).start())
    for _ in range(2):
        pltpu.make_async_copy(x_ref, xs_ref.at[pl.ds(0, tr)], sem_rows).wait()


def _route_dispatch_call(xn, w_router):
    t, d = xn.shape
    tr = ROW_TILE
    n = t // tr
    tri = (lax.broadcasted_iota(jnp.int32, (tr, tr), 0)
           < lax.broadcasted_iota(jnp.int32, (tr, tr), 1)).astype(BF16)
    return pl.pallas_call(
        functools.partial(_route_dispatch_kernel, tr=tr, cap=t),
        grid=(n,),
        in_specs=[pl.BlockSpec((tr, d), lambda i: (i, 0)),
                  pl.BlockSpec((N_EXPERTS, d), lambda i: (0, 0)),
                  pl.BlockSpec((tr, tr), lambda i: (0, 0))],
        out_specs=[pl.BlockSpec(memory_space=pl.ANY),
                   pl.BlockSpec((1, 2 * tr // LANES, LANES), lambda i: (i, 0, 0)),
                   pl.BlockSpec((None, 2, tr), lambda i: (i, 0, 0)),
                   pl.BlockSpec((N_EXPERTS, LANES), lambda i: (0, 0))],
        out_shape=[jax.ShapeDtypeStruct((N_EXPERTS * t, d), F32),
                   jax.ShapeDtypeStruct((n, 2 * tr // LANES, LANES), jnp.int32),
                   jax.ShapeDtypeStruct((n, 2, tr), F32),
                   jax.ShapeDtypeStruct((N_EXPERTS, LANES), jnp.int32)],
        scratch_shapes=[pltpu.VMEM((N_EXPERTS, LANES), F32),
                        pltpu.SMEM((1, 2 * tr // LANES, LANES), jnp.int32),
                        pltpu.SemaphoreType.DMA, pltpu.SemaphoreType.DMA],
        compiler_params=pltpu.CompilerParams(
            dimension_semantics=("arbitrary",), vmem_limit_bytes=VMEM_LIMIT),
        name="route_dispatch",
    )(xn, w_router.T, tri)


def _combine_kernel(pos_ref, pos_next_ref, h_ref, gate_ref, gfin_ref, y_ref, out_ref, buf, sem,
                    *, tc, n):
    i = pl.program_id(0)
    slot = i % 2

    def issue(p_ref, s):
        _for_each_routed_row(tc, p_ref, lambda r, kk, p: pltpu.make_async_copy(
            y_ref.at[pl.ds(p, 1)], buf.at[s, kk, pl.ds(r, 1)], sem.at[s]).start())

    @pl.when(i == 0)
    def _():
        issue(pos_ref, 0)

    for s in range(2):
        @pl.when((i + 1 < n) & (slot == 1 - s))
        def _(s=s):
            issue(pos_next_ref, s)

    for kk in range(2):
        pltpu.make_async_copy(y_ref.at[pl.ds(0, tc)], buf.at[slot, kk], sem.at[slot]).wait()
    gates = gate_ref[...]
    hn = h_ref[...] + (gates[:, 0:1] * buf[slot, 0] + gates[:, 1:2] * buf[slot, 1])
    out_ref[...] = _rmsnorm(hn, gfin_ref[...])


def _combine_call(h, y, pos, gates, gfin):
    t, d = h.shape
    tc = ROW_TILE
    n = t // tc
    smem_pos = lambda index_map: pl.BlockSpec((None, 2 * tc // LANES, LANES), index_map,
                                              memory_space=pltpu.SMEM)
    rows = lambda width: pl.BlockSpec((tc, width), lambda i: (i, 0))
    return pl.pallas_call(
        functools.partial(_combine_kernel, tc=tc, n=n),
        grid=(n,),
        in_specs=[smem_pos(lambda i: (i, 0, 0)),
                  smem_pos(lambda i: (jnp.minimum(i + 1, n - 1), 0, 0)),
                  rows(d), rows(2),
                  pl.BlockSpec((1, d), lambda i: (0, 0)),
                  pl.BlockSpec(memory_space=pl.ANY)],
        out_specs=rows(d),
        out_shape=jax.ShapeDtypeStruct((t, d), F32),
        scratch_shapes=[pltpu.VMEM((2, 2, tc, d), F32), pltpu.SemaphoreType.DMA((2,))],
        compiler_params=pltpu.CompilerParams(
            dimension_semantics=("arbitrary",), vmem_limit_bytes=VMEM_LIMIT),
        name="combine",
    )(pos, pos, h, gates, gfin, y)


def _moe(h, xn, w_router, wg, wu, wd, gfin):
    t, d = h.shape
    tm = FFN_TILE
    xs, pos, gates, counts = _route_dispatch_call(xn, w_router)
    counts = counts[:, 0]

    n_grid = 2 * t // tm + N_EXPERTS
    tiles_of = (counts + tm - 1) // tm
    tile_end = jnp.cumsum(tiles_of)
    ids = jnp.arange(n_grid, dtype=jnp.int32)
    expert = jnp.minimum(jnp.sum(ids[:, None] >= tile_end[None, :], axis=1), N_EXPERTS - 1)
    local = ids - (tile_end - tiles_of)[expert]
    table = jnp.stack([expert, expert * (t // tm) + local,
                       jnp.clip(counts[expert] - local * tm, 0, tm)]).astype(jnp.int32)
    y = _ffn_call(xs, None, wg, wu, wd, table, tile_end[-1:].astype(jnp.int32), tm, True)
    return _combine_call(h, y, pos, gates.transpose(0, 2, 1).reshape(t, 2), gfin)


def kernel(x, meta_tokens, norm_mix_g, w_in, attn_sinks, pool_w, pool_scale, w_out, norm_ffn_g,
           dense_w_gate, dense_w_up, dense_w_down, moe_w_router, moe_w_gate, moe_w_up,
           moe_w_down, final_norm_g):
    b, s, d = x.shape
    h = x.reshape(b * s, d)
    row = lambda a: a.reshape(1, -1)
    w_in_b, w_out_b, pool_w_b = w_in.astype(BF16), w_out.astype(BF16), pool_w.astype(BF16)

    kvu0, hm, xnm = _meta_call(meta_tokens, attn_sinks[0], row(norm_mix_g[0]), w_in_b[0],
                               pool_w_b[0], row(pool_scale[0]), w_out_b[0], row(norm_ffn_g[0]))
    dense = (dense_w_gate.astype(BF16), dense_w_up.astype(BF16), dense_w_down.astype(BF16))
    hm = _ffn_call(xnm, hm, *dense, *_dense_table(1, N_META), N_META, False)
    kvu1, _, _ = _meta_call(hm, attn_sinks[1], row(norm_mix_g[1]), w_in_b[1], pool_w_b[1],
                            row(pool_scale[1]), w_out_b[1], row(norm_ffn_g[1]))

    def mix(h, layer, kvu, xn_dtype):
        return _mix_call(h, b, attn_sinks[layer], row(norm_mix_g[layer]), w_in_b[layer],
                         kvu[:, 0:D_KV], kvu[:, D_KV:2 * D_KV], kvu[:, 2 * D_KV:],
                         pool_w_b[layer], row(pool_scale[layer]), w_out_b[layer],
                         row(norm_ffn_g[layer]), xn_dtype)

    h, xn = mix(h, 0, kvu0, BF16)
    h = _ffn_call(xn, h, *dense, *_dense_table(b * s // FFN_TILE, FFN_TILE), FFN_TILE, False)
    h, xn = mix(h, 1, kvu1, F32)
    out = _moe(h, xn, moe_w_router[0], moe_w_gate[0].astype(BF16), moe_w_up[0].astype(BF16),
               moe_w_down[0].astype(BF16), row(final_norm_g))
    return out.reshape(b, s, d)
```

```python
import functools

import jax
import jax.numpy as jnp
from jax import lax
from jax.experimental import pallas as pl
from jax.experimental.pallas import tpu as pltpu

F32 = jnp.float32
BF16 = jnp.bfloat16

CHUNK = 64
N_META = 16
HEAD_DIM = 64
N_Q_HEADS = 8
D_ATTN = 512
D_KV = 128
D_POOL = 512
POOL_SIZES = (2, 4, 8, 16)
POOL_GROUP = 128
N_EXPERTS = 8
EPS = 1e-5
NEG_INF = -1e30

LANES = 128
HALF = LANES // 2
Q_BLOCK = 2 * CHUNK
KEY_WINDOW = 4 * CHUNK
HALO = 2 * CHUNK
E_WIDTH = KEY_WINDOW + LANES
SOFTMAX_ROWS = 32
POOL_HALO = max(POOL_SIZES)
VMEM_LIMIT = 56 * 1024 * 1024

MIX_TILE = 512
FFN_TILE = 512
FFN_SUBTILE = 512
FFN_BLOCK = 1792
ROW_TILE = 512


def _rmsnorm(x, g):
    return x * lax.rsqrt(jnp.mean(x * x, axis=-1, keepdims=True) + EPS) * g


def _dot(a, b):
    return jnp.dot(a, b, preferred_element_type=F32)


def _dot_nt(a, b):
    return lax.dot_general(a, b, (((1,), (1,)), ((), ())), preferred_element_type=F32)


def _head_variants(t, fill=0.0):
    low = lax.broadcasted_iota(jnp.int32, t.shape, 1) < HALF
    t_r = pltpu.roll(t, HALF, axis=1)
    other = jnp.full_like(t, fill)
    return (jnp.where(low, t, other).astype(BF16),
            jnp.where(low, other, t_r).astype(BF16),
            jnp.where(low, t_r, other).astype(BF16),
            jnp.where(low, other, t).astype(BF16))


def _sink_softmax_pv(s_parts, v_parts, sink):
    m = sink
    for s in s_parts:
        m = jnp.maximum(m, jnp.max(s, axis=1, keepdims=True))
    den = jnp.exp(sink - m)
    o = None
    for s, v in zip(s_parts, v_parts):
        e = jnp.exp(s - m)
        den = den + jnp.sum(e, axis=1, keepdims=True)
        pv = _dot(e.astype(BF16), v)
        o = pv if o is None else o + pv
    return o / den


def _meta_tiles(km, vm):
    km_var = _head_variants(km)
    vm_var = _head_variants(vm, 1.0)
    zeros = lambda n: jnp.zeros((n, LANES), BF16)
    row = lax.broadcasted_iota(jnp.int32, (N_META, LANES), 0)
    low = lax.broadcasted_iota(jnp.int32, (N_META, LANES), 1) < HALF
    rest = LANES - 3 * N_META
    k_tiles, v_tiles = [], []
    for h in range(2):
        k_tiles.append(jnp.concatenate([km_var[2 * h], km_var[2 * h + 1], zeros(LANES - 2 * N_META)]))
        for par in range(2):
            den_half = low if par == 1 else jnp.logical_not(low)
            sink_rows = jnp.where((row == par) & den_half, 1.0, 0.0).astype(BF16)
            v_tiles.append(jnp.concatenate(
                [vm_var[2 * h] if par == 0 else zeros(N_META),
                 vm_var[2 * h + 1] if par == 1 else zeros(N_META), sink_rows, zeros(rest)]))
    return k_tiles, v_tiles


def _mix_kernel(*refs, tq, n_cast):
    (sinks_ref, h_ref, gmix_ref, win_ref, km_ref, vm_ref, um_ref, poolw_ref, pscale_ref,
     wout_ref, gffn_ref) = refs[:11]
    cast_in = refs[11:11 + n_cast]
    hout_ref, xn_ref = refs[11 + n_cast:13 + n_cast]
    cast_out = refs[13 + n_cast:13 + 2 * n_cast]
    q_s, kvar_s, vvar_s, uw_s, a_s, kmeta_s, vmeta_s, s_s, sm_s, e_s = refs[13 + 2 * n_cast:]
    i = pl.program_id(1)

    for src, dst in zip(cast_in, cast_out):
        dst[...] = src[...].astype(BF16)

    @pl.when(i == 0)
    def _():
        kvar_s[:, 0:HALO, :] = jnp.zeros((4, HALO, LANES), BF16)
        vvar_s[:, 0:HALO, :] = jnp.zeros((4, HALO, LANES), BF16)
        uw_s[0:POOL_HALO, :] = um_ref[...]

    x = h_ref[...]
    xn = _rmsnorm(x, gmix_ref[...]).astype(BF16)
    q_s[...] = (_dot(xn, win_ref[:, 0:D_ATTN]) * (HEAD_DIM ** -0.5)).astype(BF16)
    k = _dot(xn, win_ref[:, D_ATTN:D_ATTN + D_KV])
    v = _dot(xn, win_ref[:, D_ATTN + D_KV:D_ATTN + 2 * D_KV])
    uw_s[POOL_HALO:POOL_HALO + tq, :] = _dot(xn, win_ref[:, D_ATTN + 2 * D_KV:])
    for idx, (kk, vv) in enumerate(zip(_head_variants(k), _head_variants(v, 1.0))):
        kvar_s[idx, HALO:HALO + tq, :] = kk
        vvar_s[idx, HALO:HALO + tq, :] = vv
    k_tiles, v_tiles = _meta_tiles(km_ref[...], vm_ref[...])
    for h in range(2):
        kmeta_s[h] = k_tiles[h]
    for hp in range(4):
        vmeta_s[hp] = v_tiles[hp]

    def block_body(blk, carry):
        r0 = pl.multiple_of(blk * Q_BLOCK, Q_BLOCK)
        q_chunk = lax.broadcasted_iota(jnp.int32, (Q_BLOCK, KEY_WINDOW), 0) // CHUNK
        k_chunk = lax.broadcasted_iota(jnp.int32, (Q_BLOCK, KEY_WINDOW), 1) // CHUNK
        first_chunk = jnp.where((i == 0) & (blk == 0), 2, 0)
        valid = (k_chunk >= jnp.maximum(q_chunk, first_chunk)) & (k_chunk <= q_chunk + 2)

        for j in range(D_ATTN // LANES):
            col = q_s[pl.ds(r0, Q_BLOCK), j * LANES:(j + 1) * LANES]
            for par in range(2):
                s = _dot_nt(col, kvar_s[2 * (j // 2) + par, pl.ds(r0, KEY_WINDOW), :])
                s_s[j, :, par * KEY_WINDOW:(par + 1) * KEY_WINDOW] = jnp.where(valid, s, NEG_INF)
            sm_s[j] = _dot_nt(col, kmeta_s[j // 2])

        lane = lax.broadcasted_iota(jnp.int32, (1, LANES), 1)
        for j in range(D_ATTN // LANES):
            for par in range(2):
                own_meta = (lane >= par * N_META) & (lane < (par + 1) * N_META)
                other = jnp.where(lane == 2 * N_META + par, sinks_ref[2 * j + par], NEG_INF)
                for g in range(Q_BLOCK // SOFTMAX_ROWS):
                    rows = slice(g * SOFTMAX_ROWS, (g + 1) * SOFTMAX_ROWS)
                    s = s_s[j, rows, par * KEY_WINDOW:(par + 1) * KEY_WINDOW]
                    sm = jnp.where(own_meta, sm_s[j, rows, :], other)
                    m = jnp.maximum(jnp.maximum(s[:, :LANES], s[:, LANES:]), sm)
                    m = jnp.max(m, axis=1, keepdims=True)
                    e_s[j, rows, par * E_WIDTH:par * E_WIDTH + KEY_WINDOW] = (
                        jnp.exp(s - m).astype(BF16))
                    e_s[j, rows, par * E_WIDTH + KEY_WINDOW:(par + 1) * E_WIDTH] = (
                        jnp.exp(sm - m).astype(BF16))

        low = lax.broadcasted_iota(jnp.int32, (Q_BLOCK, LANES), 1) < HALF
        for j in range(D_ATTN // LANES):
            o = []
            for par in range(2):
                hp = 2 * (j // 2) + par
                e0 = par * E_WIDTH
                o.append(_dot(e_s[j, :, e0:e0 + KEY_WINDOW], vvar_s[hp, pl.ds(r0, KEY_WINDOW), :])
                         + _dot(e_s[j, :, e0 + KEY_WINDOW:e0 + E_WIDTH], vmeta_s[hp]))
            num = jnp.where(low, o[0], o[1])
            den = pltpu.roll(jnp.where(low, o[1], o[0]), HALF, axis=1)
            a_s[pl.ds(r0, Q_BLOCK), j * LANES:(j + 1) * LANES] = (num / den).astype(BF16)
        return carry

    lax.fori_loop(0, tq // Q_BLOCK, block_body, 0)

    pooled = []
    for g, w in enumerate(POOL_SIZES):
        cs = slice(g * POOL_GROUP, (g + 1) * POOL_GROUP)
        cur = uw_s[POOL_HALO:POOL_HALO + tq, cs]
        acc = cur
        for lag in range(1, w):
            acc = acc + uw_s[POOL_HALO - lag:POOL_HALO - lag + tq, cs]
        d = acc * (1.0 / w) - cur
        pooled.append(_dot(d.astype(BF16), poolw_ref[g]) * pscale_ref[:, cs])
    p = jnp.concatenate(pooled, axis=1).astype(BF16)

    hn = x + _dot(a_s[...], wout_ref[0:D_ATTN, :]) + _dot(p, wout_ref[D_ATTN:, :])
    hout_ref[...] = hn
    xn_ref[...] = _rmsnorm(hn, gffn_ref[...]).astype(xn_ref.dtype)

    kvar_s[:, 0:HALO, :] = kvar_s[:, tq:tq + HALO, :]
    vvar_s[:, 0:HALO, :] = vvar_s[:, tq:tq + HALO, :]
    uw_s[0:POOL_HALO, :] = uw_s[tq:tq + POOL_HALO, :]


def _mix_call(h, batch, sinks, gmix, win, km, vm, um, poolw, pscale, wout, gffn, xn_dtype,
              to_bf16=()):
    t, d = h.shape
    tq = MIX_TILE
    nt = t // batch // tq
    assert nt * tq * batch == t
    full = lambda shape: pl.BlockSpec(shape, lambda b, i: (0,) * len(shape))
    row = pl.BlockSpec((tq, d), lambda b, i: (b * nt + i, 0))
    flat = [w.reshape(-1, w.shape[-1]) for w in to_bf16]
    slabs = [pl.BlockSpec((w.shape[0] // (batch * nt), w.shape[1]), lambda b, i: (b * nt + i, 0))
             for w in flat]
    assert all(w.shape[0] % (16 * batch * nt) == 0 for w in flat)
    outs = pl.pallas_call(
        functools.partial(_mix_kernel, tq=tq, n_cast=len(flat)),
        grid=(batch, nt),
        in_specs=[
            pl.BlockSpec(memory_space=pltpu.SMEM),
            row, full((1, d)), full(win.shape), full(km.shape), full(vm.shape), full(um.shape),
            full(poolw.shape), full((1, D_POOL)), full(wout.shape), full((1, d)),
        ] + slabs,
        out_specs=[row, row] + slabs,
        out_shape=[jax.ShapeDtypeStruct((t, d), F32), jax.ShapeDtypeStruct((t, d), xn_dtype)]
        + [jax.ShapeDtypeStruct(w.shape, BF16) for w in flat],
        scratch_shapes=[
            pltpu.VMEM((tq, D_ATTN), BF16),
            pltpu.VMEM((4, HALO + tq, LANES), BF16),
            pltpu.VMEM((4, HALO + tq, LANES), BF16),
            pltpu.VMEM((POOL_HALO + tq, D_POOL), F32),
            pltpu.VMEM((tq, D_ATTN), BF16),
            pltpu.VMEM((2, LANES, LANES), BF16),
            pltpu.VMEM((4, LANES, LANES), BF16),
            pltpu.VMEM((D_ATTN // LANES, Q_BLOCK, 2 * KEY_WINDOW), F32),
            pltpu.VMEM((D_ATTN // LANES, Q_BLOCK, LANES), F32),
            pltpu.VMEM((D_ATTN // LANES, Q_BLOCK, 2 * E_WIDTH), BF16),
        ],
        compiler_params=pltpu.CompilerParams(
            dimension_semantics=("arbitrary", "arbitrary"), vmem_limit_bytes=VMEM_LIMIT),
        name="mix",
    )(sinks, h, gmix, win, km, vm, um, poolw, pscale, wout, gffn, *flat)
    return outs[0], outs[1], *[o.reshape(w.shape) for o, w in zip(outs[2:], to_bf16)]


def _meta_kernel(sinks_ref, h_ref, gmix_ref, win_ref, poolw_ref, pscale_ref, wout_ref, gffn_ref,
                 kvu_ref, hout_ref, xn_ref, uw_s):
    x = h_ref[...]
    xn = _rmsnorm(x, gmix_ref[...]).astype(BF16)
    q = (_dot(xn, win_ref[:, 0:D_ATTN]) * (HEAD_DIM ** -0.5)).astype(BF16)
    k = _dot(xn, win_ref[:, D_ATTN:D_ATTN + D_KV])
    v = _dot(xn, win_ref[:, D_ATTN + D_KV:D_ATTN + 2 * D_KV])
    u = _dot(xn, win_ref[:, D_ATTN + 2 * D_KV:])
    kvu_ref[:, 0:D_KV] = k
    kvu_ref[:, D_KV:2 * D_KV] = v
    kvu_ref[:, 2 * D_KV:] = u

    k_var = _head_variants(k)
    v_var = _head_variants(v)
    cols = []
    for j in range(D_ATTN // LANES):
        col = q[:, j * LANES:(j + 1) * LANES]
        acc = None
        for par in range(2):
            var = 2 * (j // 2) + par
            o = _sink_softmax_pv((_dot_nt(col, k_var[var]),), (v_var[var],),
                                 sinks_ref[2 * j + par])
            acc = o if acc is None else acc + o
        cols.append(acc)
    a = jnp.concatenate(cols, axis=1).astype(BF16)

    uw_s[0:POOL_HALO, :] = jnp.zeros((POOL_HALO, D_POOL), F32)
    uw_s[POOL_HALO:, :] = u
    pos = lax.broadcasted_iota(jnp.int32, (N_META, 1), 0)
    pooled = []
    for g, w in enumerate(POOL_SIZES):
        cs = slice(g * POOL_GROUP, (g + 1) * POOL_GROUP)
        cur = uw_s[POOL_HALO:, cs]
        acc = cur
        for lag in range(1, w):
            acc = acc + uw_s[POOL_HALO - lag:POOL_HALO - lag + N_META, cs]
        count = jnp.minimum(pos + 1, w).astype(F32)
        d = acc / count - cur
        pooled.append(_dot(d.astype(BF16), poolw_ref[g]) * pscale_ref[:, cs])
    p = jnp.concatenate(pooled, axis=1).astype(BF16)

    hn = x + _dot(a, wout_ref[0:D_ATTN, :]) + _dot(p, wout_ref[D_ATTN:, :])
    hout_ref[...] = hn
    xn_ref[...] = _rmsnorm(hn, gffn_ref[...]).astype(xn_ref.dtype)


def _meta_call(h, sinks, gmix, win, poolw, pscale, wout, gffn):
    n, d = h.shape
    vmem = pl.BlockSpec(memory_space=pltpu.VMEM)
    return pl.pallas_call(
        _meta_kernel,
        in_specs=[pl.BlockSpec(memory_space=pltpu.SMEM)] + [vmem] * 7,
        out_specs=[vmem, vmem, vmem],
        out_shape=[jax.ShapeDtypeStruct((n, 2 * D_KV + D_POOL), F32),
                   jax.ShapeDtypeStruct((n, d), F32),
                   jax.ShapeDtypeStruct((n, d), BF16)],
        scratch_shapes=[pltpu.VMEM((POOL_HALO + N_META, D_POOL), F32)],
        compiler_params=pltpu.CompilerParams(vmem_limit_bytes=VMEM_LIMIT),
        name="meta_mix",
    )(sinks, h, gmix, win, poolw, pscale, wout, gffn)


def _ffn_kernel(tab_ref, na_ref, x_ref, *refs, masked):
    res_ref = refs[0] if len(refs) == 5 else None
    wg_ref, wu_ref, wd_ref, out_ref = refs[-4:]
    i, j = pl.program_id(0), pl.program_id(1)

    @pl.when(i < na_ref[0])
    def _():
        @pl.when(j == 0)
        def _():
            out_ref[...] = jnp.zeros_like(out_ref) if res_ref is None else res_ref[...]

        wg, wu, wd = (w[...].astype(BF16) for w in (wg_ref, wu_ref, wd_ref))
        valid = tab_ref[2, i]
        tm = x_ref.shape[0]
        sub = min(tm, FFN_SUBTILE)
        for r0 in range(0, tm, sub):
            def part(r0=r0):
                x = x_ref[r0:r0 + sub, :]
                if masked:
                    r = r0 + lax.broadcasted_iota(jnp.int32, (sub, 1), 0)
                    x = jnp.where(r < valid, x, 0.0)
                x = x.astype(BF16)
                gate = _dot(x, wg)
                up = _dot(x, wu)
                mid = (gate * jax.nn.sigmoid(gate) * up).astype(BF16)
                out_ref[r0:r0 + sub, :] += _dot(mid, wd)

            if masked and r0 > 0:
                pl.when(valid > r0)(part)
            else:
                part()


def _ffn_call(x, res, wg, wu, wd, table, n_active, tm, masked):
    rows, d = x.shape
    ff = wg.shape[-1]
    n_grid = table.shape[1]
    nj = ff // FFN_BLOCK
    assert rows % tm == 0 and nj * FFN_BLOCK == ff

    def live(i, na):
        return jnp.maximum(jnp.minimum(i, na[0] - 1), 0)

    def ff_block(i, j, na):
        return jnp.where(i < na[0], j, nj - 1)

    row = pl.BlockSpec((tm, d), lambda i, j, tab, na: (tab[1, live(i, na)], 0))
    col_w = pl.BlockSpec((None, d, FFN_BLOCK),
                         lambda i, j, tab, na: (tab[0, live(i, na)], 0, ff_block(i, j, na)))
    row_w = pl.BlockSpec((None, FFN_BLOCK, d),
                         lambda i, j, tab, na: (tab[0, live(i, na)], ff_block(i, j, na), 0))
    acts = (x,) if res is None else (x, res)
    return pl.pallas_call(
        functools.partial(_ffn_kernel, masked=masked),
        grid_spec=pltpu.PrefetchScalarGridSpec(
            num_scalar_prefetch=2, grid=(n_grid, nj),
            in_specs=[row] * len(acts) + [col_w, col_w, row_w], out_specs=row),
        out_shape=jax.ShapeDtypeStruct((rows, d), F32),
        compiler_params=pltpu.CompilerParams(
            dimension_semantics=("arbitrary", "arbitrary"), vmem_limit_bytes=VMEM_LIMIT),
        name="ffn",
    )(table, n_active, *acts, wg, wu, wd)


def _dense_table(n_tiles, tm):
    ids = jnp.arange(n_tiles, dtype=jnp.int32)
    return (jnp.stack([jnp.zeros_like(ids), ids, jnp.full_like(ids, tm)]),
            jnp.full((1,), n_tiles, jnp.int32))


def _split_bf16(x):
    hi = x.astype(BF16)
    return hi, (x - hi.astype(F32)).astype(BF16)


def _for_each_routed_row(n_rows, pos_ref, fn):
    chunks = n_rows // LANES
    for r in range(n_rows):
        for kk in range(2):
            fn(r, kk, pos_ref[kk * chunks + r // LANES, r % LANES])


def _route_dispatch_kernel(x_ref, wr_ref, tri_ref, xs_ref, pos_ref, gate_ref, counts_ref,
                           run_s, pos_sm, xbuf, sem_pos, sem_rows, *, tr, cap, n):
    i = pl.program_id(0)

    @pl.when(i == 0)
    def _():
        run_s[...] = jnp.zeros_like(run_s)

    x_hi, x_lo = _split_bf16(x_ref[...])
    w_hi, w_lo = _split_bf16(wr_ref[...])
    logits = _dot_nt(w_hi, x_hi) + (_dot_nt(w_lo, x_hi) + _dot_nt(w_hi, x_lo))

    expert = lax.broadcasted_iota(jnp.int32, logits.shape, 0).astype(F32)
    none = float(N_EXPERTS)
    m1 = jnp.max(logits, axis=0, keepdims=True)
    e1 = jnp.min(jnp.where(logits == m1, expert, none), axis=0, keepdims=True)
    rest = jnp.where(expert == e1, -jnp.inf, logits)
    m2 = jnp.max(rest, axis=0, keepdims=True)
    e2 = jnp.min(jnp.where(rest == m2, expert, none), axis=0, keepdims=True)
    t = jnp.exp(m2 - m1)
    gate_ref[0:1, :] = 1.0 / (1.0 + t)
    gate_ref[1:2, :] = t / (1.0 + t)

    chosen = ((expert == e1) | (expert == e2)).astype(F32)
    before = _dot(chosen.astype(BF16), tri_ref[...]) + run_s[:, 0:1]
    pieces = []
    for e in (e1, e2):
        rank = jnp.sum(jnp.where(expert == e, before, 0.0), axis=0, keepdims=True)
        pos = (e * float(cap) + rank).astype(jnp.int32)
        pieces += [pos[:, c * LANES:(c + 1) * LANES] for c in range(tr // LANES)]
    pos_ref[0] = jnp.concatenate(pieces, axis=0)
    run_s[...] += jnp.sum(chosen, axis=1, keepdims=True)
    counts_ref[...] = run_s[...].astype(jnp.int32)

    to_smem = pltpu.make_async_copy(pos_ref, pos_sm, sem_pos)
    to_smem.start()

    def wait_rows(s):
        for _ in range(2):
            pltpu.make_async_copy(xbuf.at[s], xs_ref.at[pl.ds(0, tr)], sem_rows.at[s]).wait()

    slot = i % 2

    @pl.when(i >= 2)
    def _():
        wait_rows(slot)

    xbuf[slot] = x_ref[...]
    to_smem.wait()
    for s in range(2):
        @pl.when(slot == s)
        def _(s=s):
            _for_each_routed_row(tr, pos_sm.at[0], lambda r, kk, p: pltpu.make_async_copy(
                xbuf.at[s, pl.ds(r, 1)], xs_ref.at[pl.ds(p, 1)], sem_rows.at[s]).start())

    @pl.when(i == n - 1)
    def _():
        wait_rows(slot)
        if n > 1:
            wait_rows(1 - slot)


def _route_dispatch_call(xn, w_router):
    t, d = xn.shape
    tr = ROW_TILE
    n = t // tr
    tri = (lax.broadcasted_iota(jnp.int32, (tr, tr), 0)
           < lax.broadcasted_iota(jnp.int32, (tr, tr), 1)).astype(BF16)
    return pl.pallas_call(
        functools.partial(_route_dispatch_kernel, tr=tr, cap=t, n=n),
        grid=(n,),
        in_specs=[pl.BlockSpec((tr, d), lambda i: (i, 0)),
                  pl.BlockSpec((N_EXPERTS, d), lambda i: (0, 0)),
                  pl.BlockSpec((tr, tr), lambda i: (0, 0))],
        out_specs=[pl.BlockSpec(memory_space=pl.ANY),
                   pl.BlockSpec((1, 2 * tr // LANES, LANES), lambda i: (i, 0, 0)),
                   pl.BlockSpec((None, 2, tr), lambda i: (i, 0, 0)),
                   pl.BlockSpec((N_EXPERTS, LANES), lambda i: (0, 0))],
        out_shape=[jax.ShapeDtypeStruct((N_EXPERTS * t, d), F32),
                   jax.ShapeDtypeStruct((n, 2 * tr // LANES, LANES), jnp.int32),
                   jax.ShapeDtypeStruct((n, 2, tr), F32),
                   jax.ShapeDtypeStruct((N_EXPERTS, LANES), jnp.int32)],
        scratch_shapes=[pltpu.VMEM((N_EXPERTS, LANES), F32),
                        pltpu.SMEM((1, 2 * tr // LANES, LANES), jnp.int32),
                        pltpu.VMEM((2, tr, d), F32),
                        pltpu.SemaphoreType.DMA, pltpu.SemaphoreType.DMA((2,))],
        compiler_params=pltpu.CompilerParams(
            dimension_semantics=("arbitrary",), vmem_limit_bytes=VMEM_LIMIT),
        name="route_dispatch",
    )(xn, w_router.T, tri)


def _combine_kernel(pos_ref, pos_next_ref, h_ref, gate_ref, gfin_ref, y_ref, out_ref, buf, sem,
                    *, tc, n):
    i = pl.program_id(0)
    slot = i % 2

    def issue(p_ref, s):
        _for_each_routed_row(tc, p_ref, lambda r, kk, p: pltpu.make_async_copy(
            y_ref.at[pl.ds(p, 1)], buf.at[s, kk, pl.ds(r, 1)], sem.at[s]).start())

    @pl.when(i == 0)
    def _():
        issue(pos_ref, 0)

    for s in range(2):
        @pl.when((i + 1 < n) & (slot == 1 - s))
        def _(s=s):
            issue(pos_next_ref, s)

    for kk in range(2):
        pltpu.make_async_copy(y_ref.at[pl.ds(0, tc)], buf.at[slot, kk], sem.at[slot]).wait()
    gates = gate_ref[...]
    hn = h_ref[...] + (gates[:, 0:1] * buf[slot, 0] + gates[:, 1:2] * buf[slot, 1])
    out_ref[...] = _rmsnorm(hn, gfin_ref[...])


def _combine_call(h, y, pos, gates, gfin):
    t, d = h.shape
    tc = ROW_TILE
    n = t // tc
    smem_pos = lambda index_map: pl.BlockSpec((None, 2 * tc // LANES, LANES), index_map,
                                              memory_space=pltpu.SMEM)
    rows = lambda width: pl.BlockSpec((tc, width), lambda i: (i, 0))
    return pl.pallas_call(
        functools.partial(_combine_kernel, tc=tc, n=n),
        grid=(n,),
        in_specs=[smem_pos(lambda i: (i, 0, 0)),
                  smem_pos(lambda i: (jnp.minimum(i + 1, n - 1), 0, 0)),
                  rows(d), rows(2),
                  pl.BlockSpec((1, d), lambda i: (0, 0)),
                  pl.BlockSpec(memory_space=pl.ANY)],
        out_specs=rows(d),
        out_shape=jax.ShapeDtypeStruct((t, d), F32),
        scratch_shapes=[pltpu.VMEM((2, 2, tc, d), F32), pltpu.SemaphoreType.DMA((2,))],
        compiler_params=pltpu.CompilerParams(
            dimension_semantics=("arbitrary",), vmem_limit_bytes=VMEM_LIMIT),
        name="combine",
    )(pos, pos, h, gates, gfin, y)


def _moe(h, xn, w_router, wg, wu, wd, gfin):
    t, d = h.shape
    tm = FFN_TILE
    xs, pos, gates, counts = _route_dispatch_call(xn, w_router)
    counts = counts[:, 0]

    n_grid = 2 * t // tm + N_EXPERTS
    tiles_of = (counts + tm - 1) // tm
    tile_end = jnp.cumsum(tiles_of)
    ids = jnp.arange(n_grid, dtype=jnp.int32)
    expert = jnp.minimum(jnp.sum(ids[:, None] >= tile_end[None, :], axis=1), N_EXPERTS - 1)
    local = ids - (tile_end - tiles_of)[expert]
    table = jnp.stack([expert, expert * (t // tm) + local,
                       jnp.clip(counts[expert] - local * tm, 0, tm)]).astype(jnp.int32)
    y = _ffn_call(xs, None, wg, wu, wd, table, tile_end[-1:].astype(jnp.int32), tm, True)
    return _combine_call(h, y, pos, gates.transpose(0, 2, 1).reshape(t, 2), gfin)


def kernel(x, meta_tokens, norm_mix_g, w_in, attn_sinks, pool_w, pool_scale, w_out, norm_ffn_g,
           dense_w_gate, dense_w_up, dense_w_down, moe_w_router, moe_w_gate, moe_w_up,
           moe_w_down, final_norm_g):
    b, s, d = x.shape
    h = x.reshape(b * s, d)
    row = lambda a: a.reshape(1, -1)
    w_in_b, w_out_b, pool_w_b = w_in.astype(BF16), w_out.astype(BF16), pool_w.astype(BF16)

    def mix(h, layer, kvu, xn_dtype, to_bf16):
        return _mix_call(h, b, attn_sinks[layer], row(norm_mix_g[layer]), w_in_b[layer],
                         kvu[:, 0:D_KV], kvu[:, D_KV:2 * D_KV], kvu[:, 2 * D_KV:],
                         pool_w_b[layer], row(pool_scale[layer]), w_out_b[layer],
                         row(norm_ffn_g[layer]), xn_dtype, to_bf16)

    kvu0, hm, xnm = _meta_call(meta_tokens, attn_sinks[0], row(norm_mix_g[0]), w_in_b[0],
                               pool_w_b[0], row(pool_scale[0]), w_out_b[0], row(norm_ffn_g[0]))
    h, xn, *dense, moe_gate = mix(h, 0, kvu0, BF16,
                                  (dense_w_gate, dense_w_up, dense_w_down, moe_w_gate[0]))
    hm = _ffn_call(xnm, hm, *dense, *_dense_table(1, N_META), N_META, False)
    kvu1, _, _ = _meta_call(hm, attn_sinks[1], row(norm_mix_g[1]), w_in_b[1], pool_w_b[1],
                            row(pool_scale[1]), w_out_b[1], row(norm_ffn_g[1]))
    h = _ffn_call(xn, h, *dense, *_dense_table(b * s // FFN_TILE, FFN_TILE), FFN_TILE, False)
    h, xn, moe_up, moe_down = mix(h, 1, kvu1, F32, (moe_w_up[0], moe_w_down[0]))
    out = _moe(h, xn, moe_w_router[0], moe_gate, moe_up, moe_down, row(final_norm_g))
    return out.reshape(b, s, d)
```

```python
import functools

import jax
import jax.numpy as jnp
from jax import lax
from jax.experimental import pallas as pl
from jax.experimental.pallas import tpu as pltpu

F32 = jnp.float32
BF16 = jnp.bfloat16

CHUNK = 64
N_META = 16
HEAD_DIM = 64
N_Q_HEADS = 8
D_ATTN = 512
D_KV = 128
D_POOL = 512
POOL_SIZES = (2, 4, 8, 16)
POOL_GROUP = 128
N_EXPERTS = 8
EPS = 1e-5
NEG_INF = -1e30
LOG2_E = 1.4426950408889634

LANES = 128
HALF = LANES // 2
Q_BLOCK = 2 * CHUNK
KEY_WINDOW = 4 * CHUNK
HALO = 2 * CHUNK
E_WIDTH = KEY_WINDOW + LANES
SOFTMAX_ROWS = 32
POOL_HALO = max(POOL_SIZES)
VMEM_LIMIT = 56 * 1024 * 1024

MIX_TILE = 512
FFN_TILE = 512
FFN_SUBTILE = 512
FFN_BLOCK = 1792
ROW_TILE = 512


def _rmsnorm(x, g):
    return x * lax.rsqrt(jnp.mean(x * x, axis=-1, keepdims=True) + EPS) * g


def _dot(a, b):
    return jnp.dot(a, b, preferred_element_type=F32)


def _dot_nt(a, b):
    return lax.dot_general(a, b, (((1,), (1,)), ((), ())), preferred_element_type=F32)


def _head_variants(t, fill=0.0):
    low = lax.broadcasted_iota(jnp.int32, t.shape, 1) < HALF
    t_r = pltpu.roll(t, HALF, axis=1)
    other = jnp.full_like(t, fill)
    return (jnp.where(low, t, other).astype(BF16),
            jnp.where(low, other, t_r).astype(BF16),
            jnp.where(low, t_r, other).astype(BF16),
            jnp.where(low, other, t).astype(BF16))


def _sink_softmax_pv(s_parts, v_parts, sink):
    m = sink
    for s in s_parts:
        m = jnp.maximum(m, jnp.max(s, axis=1, keepdims=True))
    den = jnp.exp(sink - m)
    o = None
    for s, v in zip(s_parts, v_parts):
        e = jnp.exp(s - m)
        den = den + jnp.sum(e, axis=1, keepdims=True)
        pv = _dot(e.astype(BF16), v)
        o = pv if o is None else o + pv
    return o / den


def _meta_tiles(km, vm):
    km_var = _head_variants(km)
    vm_var = _head_variants(vm, 1.0)
    zeros = lambda n: jnp.zeros((n, LANES), BF16)
    row = lax.broadcasted_iota(jnp.int32, (N_META, LANES), 0)
    low = lax.broadcasted_iota(jnp.int32, (N_META, LANES), 1) < HALF
    rest = LANES - 3 * N_META
    k_tiles, v_tiles = [], []
    for h in range(2):
        k_tiles.append(jnp.concatenate([km_var[2 * h], km_var[2 * h + 1], zeros(LANES - 2 * N_META)]))
        for par in range(2):
            den_half = low if par == 1 else jnp.logical_not(low)
            sink_rows = jnp.where((row == par) & den_half, 1.0, 0.0).astype(BF16)
            v_tiles.append(jnp.concatenate(
                [vm_var[2 * h] if par == 0 else zeros(N_META),
                 vm_var[2 * h + 1] if par == 1 else zeros(N_META), sink_rows, zeros(rest)]))
    return k_tiles, v_tiles


def _mix_kernel(*refs, tq, n_cast):
    (sinks_ref, h_ref, gmix_ref, win_ref, km_ref, vm_ref, um_ref, poolw_ref, pscale_ref,
     wout_ref, gffn_ref) = refs[:11]
    cast_in = refs[11:11 + n_cast]
    hout_ref, xn_ref = refs[11 + n_cast:13 + n_cast]
    cast_out = refs[13 + n_cast:13 + 2 * n_cast]
    q_s, kvar_s, vvar_s, uw_s, a_s, kmeta_s, vmeta_s, s_s, sm_s, e_s = refs[13 + 2 * n_cast:]
    i = pl.program_id(1)

    for src, dst in zip(cast_in, cast_out):
        dst[...] = src[...].astype(BF16)

    @pl.when(i == 0)
    def _():
        kvar_s[:, 0:HALO, :] = jnp.zeros((4, HALO, LANES), BF16)
        vvar_s[:, 0:HALO, :] = jnp.zeros((4, HALO, LANES), BF16)
        uw_s[0:POOL_HALO, :] = um_ref[...]

    x = h_ref[...]
    xn = _rmsnorm(x, gmix_ref[...]).astype(BF16)
    q_s[...] = (_dot(xn, win_ref[:, 0:D_ATTN]) * (HEAD_DIM ** -0.5 * LOG2_E)).astype(BF16)
    k = _dot(xn, win_ref[:, D_ATTN:D_ATTN + D_KV])
    v = _dot(xn, win_ref[:, D_ATTN + D_KV:D_ATTN + 2 * D_KV])
    uw_s[POOL_HALO:POOL_HALO + tq, :] = _dot(xn, win_ref[:, D_ATTN + 2 * D_KV:])
    for idx, (kk, vv) in enumerate(zip(_head_variants(k), _head_variants(v, 1.0))):
        kvar_s[idx, HALO:HALO + tq, :] = kk
        vvar_s[idx, HALO:HALO + tq, :] = vv
    k_tiles, v_tiles = _meta_tiles(km_ref[...], vm_ref[...])
    for h in range(2):
        kmeta_s[h] = k_tiles[h]
    for hp in range(4):
        vmeta_s[hp] = v_tiles[hp]

    def block_body(blk, carry):
        r0 = pl.multiple_of(blk * Q_BLOCK, Q_BLOCK)
        q_chunk = lax.broadcasted_iota(jnp.int32, (Q_BLOCK, KEY_WINDOW), 0) // CHUNK
        k_chunk = lax.broadcasted_iota(jnp.int32, (Q_BLOCK, KEY_WINDOW), 1) // CHUNK
        first_chunk = jnp.where((i == 0) & (blk == 0), 2, 0)
        valid = (k_chunk >= jnp.maximum(q_chunk, first_chunk)) & (k_chunk <= q_chunk + 2)

        for j in range(D_ATTN // LANES):
            col = q_s[pl.ds(r0, Q_BLOCK), j * LANES:(j + 1) * LANES]
            for par in range(2):
                s = _dot_nt(col, kvar_s[2 * (j // 2) + par, pl.ds(r0, KEY_WINDOW), :])
                s_s[j, :, par * KEY_WINDOW:(par + 1) * KEY_WINDOW] = jnp.where(valid, s, NEG_INF)
            sm_s[j] = _dot_nt(col, kmeta_s[j // 2])

        lane = lax.broadcasted_iota(jnp.int32, (1, LANES), 1)
        for j in range(D_ATTN // LANES):
            for par in range(2):
                own_meta = (lane >= par * N_META) & (lane < (par + 1) * N_META)
                sink = sinks_ref[2 * j + par] * LOG2_E
                other = jnp.where(lane == 2 * N_META + par, sink, NEG_INF)
                for g in range(Q_BLOCK // SOFTMAX_ROWS):
                    rows = slice(g * SOFTMAX_ROWS, (g + 1) * SOFTMAX_ROWS)
                    s = s_s[j, rows, par * KEY_WINDOW:(par + 1) * KEY_WINDOW]
                    sm = jnp.where(own_meta, sm_s[j, rows, :], other)
                    m = jnp.maximum(jnp.maximum(s[:, :LANES], s[:, LANES:]), sm)
                    m = jnp.max(m, axis=1, keepdims=True)
                    e_s[j, rows, par * E_WIDTH:par * E_WIDTH + KEY_WINDOW] = (
                        jnp.exp2(s - m).astype(BF16))
                    e_s[j, rows, par * E_WIDTH + KEY_WINDOW:(par + 1) * E_WIDTH] = (
                        jnp.exp2(sm - m).astype(BF16))

        low = lax.broadcasted_iota(jnp.int32, (Q_BLOCK, LANES), 1) < HALF
        for j in range(D_ATTN // LANES):
            o = []
            for par in range(2):
                hp = 2 * (j // 2) + par
                e0 = par * E_WIDTH
                o.append(_dot(e_s[j, :, e0:e0 + KEY_WINDOW], vvar_s[hp, pl.ds(r0, KEY_WINDOW), :])
                         + _dot(e_s[j, :, e0 + KEY_WINDOW:e0 + E_WIDTH], vmeta_s[hp]))
            num = jnp.where(low, o[0], o[1])
            den = pltpu.roll(jnp.where(low, o[1], o[0]), HALF, axis=1)
            a_s[pl.ds(r0, Q_BLOCK), j * LANES:(j + 1) * LANES] = (num / den).astype(BF16)
        return carry

    lax.fori_loop(0, tq // Q_BLOCK, block_body, 0, unroll=True)

    pooled = []
    for g, w in enumerate(POOL_SIZES):
        cs = slice(g * POOL_GROUP, (g + 1) * POOL_GROUP)
        cur = uw_s[POOL_HALO:POOL_HALO + tq, cs]
        acc = cur
        for lag in range(1, w):
            acc = acc + uw_s[POOL_HALO - lag:POOL_HALO - lag + tq, cs]
        d = acc * (1.0 / w) - cur
        pooled.append(_dot(d.astype(BF16), poolw_ref[g]) * pscale_ref[:, cs])
    p = jnp.concatenate(pooled, axis=1).astype(BF16)

    hn = x + _dot(a_s[...], wout_ref[0:D_ATTN, :]) + _dot(p, wout_ref[D_ATTN:, :])
    hout_ref[...] = hn
    xn_ref[...] = _rmsnorm(hn, gffn_ref[...]).astype(xn_ref.dtype)

    kvar_s[:, 0:HALO, :] = kvar_s[:, tq:tq + HALO, :]
    vvar_s[:, 0:HALO, :] = vvar_s[:, tq:tq + HALO, :]
    uw_s[0:POOL_HALO, :] = uw_s[tq:tq + POOL_HALO, :]


def _mix_call(h, batch, sinks, gmix, win, km, vm, um, poolw, pscale, wout, gffn, xn_dtype,
              to_bf16=()):
    t, d = h.shape
    tq = MIX_TILE
    nt = t // batch // tq
    assert nt * tq * batch == t
    full = lambda shape: pl.BlockSpec(shape, lambda b, i: (0,) * len(shape))
    row = pl.BlockSpec((tq, d), lambda b, i: (b * nt + i, 0))
    flat = [w.reshape(-1, w.shape[-1]) for w in to_bf16]
    slabs = [pl.BlockSpec((w.shape[0] // (batch * nt), w.shape[1]), lambda b, i: (b * nt + i, 0))
             for w in flat]
    assert all(w.shape[0] % (16 * batch * nt) == 0 for w in flat)
    outs = pl.pallas_call(
        functools.partial(_mix_kernel, tq=tq, n_cast=len(flat)),
        grid=(batch, nt),
        in_specs=[
            pl.BlockSpec(memory_space=pltpu.SMEM),
            row, full((1, d)), full(win.shape), full(km.shape), full(vm.shape), full(um.shape),
            full(poolw.shape), full((1, D_POOL)), full(wout.shape), full((1, d)),
        ] + slabs,
        out_specs=[row, row] + slabs,
        out_shape=[jax.ShapeDtypeStruct((t, d), F32), jax.ShapeDtypeStruct((t, d), xn_dtype)]
        + [jax.ShapeDtypeStruct(w.shape, BF16) for w in flat],
        scratch_shapes=[
            pltpu.VMEM((tq, D_ATTN), BF16),
            pltpu.VMEM((4, HALO + tq, LANES), BF16),
            pltpu.VMEM((4, HALO + tq, LANES), BF16),
            pltpu.VMEM((POOL_HALO + tq, D_POOL), F32),
            pltpu.VMEM((tq, D_ATTN), BF16),
            pltpu.VMEM((2, LANES, LANES), BF16),
            pltpu.VMEM((4, LANES, LANES), BF16),
            pltpu.VMEM((D_ATTN // LANES, Q_BLOCK, 2 * KEY_WINDOW), F32),
            pltpu.VMEM((D_ATTN // LANES, Q_BLOCK, LANES), F32),
            pltpu.VMEM((D_ATTN // LANES, Q_BLOCK, 2 * E_WIDTH), BF16),
        ],
        compiler_params=pltpu.CompilerParams(
            dimension_semantics=("arbitrary", "arbitrary"), vmem_limit_bytes=VMEM_LIMIT),
        name="mix",
    )(sinks, h, gmix, win, km, vm, um, poolw, pscale, wout, gffn, *flat)
    return outs[0], outs[1], *[o.reshape(w.shape) for o, w in zip(outs[2:], to_bf16)]


def _meta_kernel(sinks_ref, h_ref, gmix_ref, win_ref, poolw_ref, pscale_ref, wout_ref, gffn_ref,
                 kvu_ref, hout_ref, xn_ref, uw_s):
    x = h_ref[...]
    xn = _rmsnorm(x, gmix_ref[...]).astype(BF16)
    q = (_dot(xn, win_ref[:, 0:D_ATTN]) * (HEAD_DIM ** -0.5)).astype(BF16)
    k = _dot(xn, win_ref[:, D_ATTN:D_ATTN + D_KV])
    v = _dot(xn, win_ref[:, D_ATTN + D_KV:D_ATTN + 2 * D_KV])
    u = _dot(xn, win_ref[:, D_ATTN + 2 * D_KV:])
    kvu_ref[:, 0:D_KV] = k
    kvu_ref[:, D_KV:2 * D_KV] = v
    kvu_ref[:, 2 * D_KV:] = u

    k_var = _head_variants(k)
    v_var = _head_variants(v)
    cols = []
    for j in range(D_ATTN // LANES):
        col = q[:, j * LANES:(j + 1) * LANES]
        acc = None
        for par in range(2):
            var = 2 * (j // 2) + par
            o = _sink_softmax_pv((_dot_nt(col, k_var[var]),), (v_var[var],),
                                 sinks_ref[2 * j + par])
            acc = o if acc is None else acc + o
        cols.append(acc)
    a = jnp.concatenate(cols, axis=1).astype(BF16)

    uw_s[0:POOL_HALO, :] = jnp.zeros((POOL_HALO, D_POOL), F32)
    uw_s[POOL_HALO:, :] = u
    pos = lax.broadcasted_iota(jnp.int32, (N_META, 1), 0)
    pooled = []
    for g, w in enumerate(POOL_SIZES):
        cs = slice(g * POOL_GROUP, (g + 1) * POOL_GROUP)
        cur = uw_s[POOL_HALO:, cs]
        acc = cur
        for lag in range(1, w):
            acc = acc + uw_s[POOL_HALO - lag:POOL_HALO - lag + N_META, cs]
        count = jnp.minimum(pos + 1, w).astype(F32)
        d = acc / count - cur
        pooled.append(_dot(d.astype(BF16), poolw_ref[g]) * pscale_ref[:, cs])
    p = jnp.concatenate(pooled, axis=1).astype(BF16)

    hn = x + _dot(a, wout_ref[0:D_ATTN, :]) + _dot(p, wout_ref[D_ATTN:, :])
    hout_ref[...] = hn
    xn_ref[...] = _rmsnorm(hn, gffn_ref[...]).astype(xn_ref.dtype)


def _meta_call(h, sinks, gmix, win, poolw, pscale, wout, gffn):
    n, d = h.shape
    vmem = pl.BlockSpec(memory_space=pltpu.VMEM)
    return pl.pallas_call(
        _meta_kernel,
        in_specs=[pl.BlockSpec(memory_space=pltpu.SMEM)] + [vmem] * 7,
        out_specs=[vmem, vmem, vmem],
        out_shape=[jax.ShapeDtypeStruct((n, 2 * D_KV + D_POOL), F32),
                   jax.ShapeDtypeStruct((n, d), F32),
                   jax.ShapeDtypeStruct((n, d), BF16)],
        scratch_shapes=[pltpu.VMEM((POOL_HALO + N_META, D_POOL), F32)],
        compiler_params=pltpu.CompilerParams(vmem_limit_bytes=VMEM_LIMIT),
        name="meta_mix",
    )(sinks, h, gmix, win, poolw, pscale, wout, gffn)


def _ffn_kernel(tab_ref, na_ref, x_ref, *refs, masked):
    res_ref = refs[0] if len(refs) == 5 else None
    wg_ref, wu_ref, wd_ref, out_ref = refs[-4:]
    i, j = pl.program_id(0), pl.program_id(1)

    @pl.when(i < na_ref[0])
    def _():
        @pl.when(j == 0)
        def _():
            out_ref[...] = jnp.zeros_like(out_ref) if res_ref is None else res_ref[...]

        wg, wu, wd = (w[...].astype(BF16) for w in (wg_ref, wu_ref, wd_ref))
        valid = tab_ref[2, i]
        tm = x_ref.shape[0]
        sub = min(tm, FFN_SUBTILE)
        for r0 in range(0, tm, sub):
            def part(r0=r0):
                x = x_ref[r0:r0 + sub, :]
                if masked:
                    r = r0 + lax.broadcasted_iota(jnp.int32, (sub, 1), 0)
                    x = jnp.where(r < valid, x, 0.0)
                x = x.astype(BF16)
                gate = _dot(x, wg)
                up = _dot(x, wu)
                mid = (gate * jax.nn.sigmoid(gate) * up).astype(BF16)
                out_ref[r0:r0 + sub, :] += _dot(mid, wd)

            if masked and r0 > 0:
                pl.when(valid > r0)(part)
            else:
                part()


def _ffn_call(x, res, wg, wu, wd, table, n_active, tm, masked):
    rows, d = x.shape
    ff = wg.shape[-1]
    n_grid = table.shape[1]
    nj = ff // FFN_BLOCK
    assert rows % tm == 0 and nj * FFN_BLOCK == ff

    def live(i, na):
        return jnp.maximum(jnp.minimum(i, na[0] - 1), 0)

    def ff_block(i, j, na):
        return jnp.where(i < na[0], j, nj - 1)

    row = pl.BlockSpec((tm, d), lambda i, j, tab, na: (tab[1, live(i, na)], 0))
    col_w = pl.BlockSpec((None, d, FFN_BLOCK),
                         lambda i, j, tab, na: (tab[0, live(i, na)], 0, ff_block(i, j, na)))
    row_w = pl.BlockSpec((None, FFN_BLOCK, d),
                         lambda i, j, tab, na: (tab[0, live(i, na)], ff_block(i, j, na), 0))
    acts = (x,) if res is None else (x, res)
    return pl.pallas_call(
        functools.partial(_ffn_kernel, masked=masked),
        grid_spec=pltpu.PrefetchScalarGridSpec(
            num_scalar_prefetch=2, grid=(n_grid, nj),
            in_specs=[row] * len(acts) + [col_w, col_w, row_w], out_specs=row),
        out_shape=jax.ShapeDtypeStruct((rows, d), F32),
        compiler_params=pltpu.CompilerParams(
            dimension_semantics=("arbitrary", "arbitrary"), vmem_limit_bytes=VMEM_LIMIT),
        name="ffn",
    )(table, n_active, *acts, wg, wu, wd)


def _dense_table(n_tiles, tm):
    ids = jnp.arange(n_tiles, dtype=jnp.int32)
    return (jnp.stack([jnp.zeros_like(ids), ids, jnp.full_like(ids, tm)]),
            jnp.full((1,), n_tiles, jnp.int32))


def _split_bf16(x):
    hi = x.astype(BF16)
    return hi, (x - hi.astype(F32)).astype(BF16)


def _for_each_routed_row(n_rows, pos_ref, fn):
    chunks = n_rows // LANES
    for r in range(n_rows):
        for kk in range(2):
            fn(r, kk, pos_ref[kk * chunks + r // LANES, r % LANES])


def _route_dispatch_kernel(x_ref, wr_ref, tri_ref, xs_ref, pos_ref, gate_ref, counts_ref,
                           run_s, pos_sm, xbuf, sem_pos, sem_rows, *, tr, cap, n):
    i = pl.program_id(0)

    @pl.when(i == 0)
    def _():
        run_s[...] = jnp.zeros_like(run_s)

    x_hi, x_lo = _split_bf16(x_ref[...])
    w_hi, w_lo = _split_bf16(wr_ref[...])
    logits = _dot_nt(w_hi, x_hi) + (_dot_nt(w_lo, x_hi) + _dot_nt(w_hi, x_lo))

    expert = lax.broadcasted_iota(jnp.int32, logits.shape, 0).astype(F32)
    none = float(N_EXPERTS)
    m1 = jnp.max(logits, axis=0, keepdims=True)
    e1 = jnp.min(jnp.where(logits == m1, expert, none), axis=0, keepdims=True)
    rest = jnp.where(expert == e1, -jnp.inf, logits)
    m2 = jnp.max(rest, axis=0, keepdims=True)
    e2 = jnp.min(jnp.where(rest == m2, expert, none), axis=0, keepdims=True)
    t = jnp.exp(m2 - m1)
    gate_ref[0:1, :] = 1.0 / (1.0 + t)
    gate_ref[1:2, :] = t / (1.0 + t)

    chosen = ((expert == e1) | (expert == e2)).astype(F32)
    before = _dot(chosen.astype(BF16), tri_ref[...]) + run_s[:, 0:1]
    pieces = []
    for e in (e1, e2):
        rank = jnp.sum(jnp.where(expert == e, before, 0.0), axis=0, keepdims=True)
        pos = (e * float(cap) + rank).astype(jnp.int32)
        pieces += [pos[:, c * LANES:(c + 1) * LANES] for c in range(tr // LANES)]
    pos_ref[0] = jnp.concatenate(pieces, axis=0)
    run_s[...] += jnp.sum(chosen, axis=1, keepdims=True)
    counts_ref[...] = run_s[...].astype(jnp.int32)

    to_smem = pltpu.make_async_copy(pos_ref, pos_sm, sem_pos)
    to_smem.start()

    def wait_rows(s):
        for _ in range(2):
            pltpu.make_async_copy(xbuf.at[s], xs_ref.at[pl.ds(0, tr)], sem_rows.at[s]).wait()

    slot = i % 2

    @pl.when(i >= 2)
    def _():
        wait_rows(slot)

    xbuf[slot] = x_ref[...]
    to_smem.wait()
    for s in range(2):
        @pl.when(slot == s)
        def _(s=s):
            _for_each_routed_row(tr, pos_sm.at[0], lambda r, kk, p: pltpu.make_async_copy(
                xbuf.at[s, pl.ds(r, 1)], xs_ref.at[pl.ds(p, 1)], sem_rows.at[s]).start(kk))

    @pl.when(i == n - 1)
    def _():
        wait_rows(slot)
        if n > 1:
            wait_rows(1 - slot)


def _route_dispatch_call(xn, w_router):
    t, d = xn.shape
    tr = ROW_TILE
    n = t // tr
    tri = (lax.broadcasted_iota(jnp.int32, (tr, tr), 0)
           < lax.broadcasted_iota(jnp.int32, (tr, tr), 1)).astype(BF16)
    return pl.pallas_call(
        functools.partial(_route_dispatch_kernel, tr=tr, cap=t, n=n),
        grid=(n,),
        in_specs=[pl.BlockSpec((tr, d), lambda i: (i, 0)),
                  pl.BlockSpec((N_EXPERTS, d), lambda i: (0, 0)),
                  pl.BlockSpec((tr, tr), lambda i: (0, 0))],
        out_specs=[pl.BlockSpec(memory_space=pl.ANY),
                   pl.BlockSpec((1, 2 * tr // LANES, LANES), lambda i: (i, 0, 0)),
                   pl.BlockSpec((None, 2, tr), lambda i: (i, 0, 0)),
                   pl.BlockSpec((N_EXPERTS, LANES), lambda i: (0, 0))],
        out_shape=[jax.ShapeDtypeStruct((N_EXPERTS * t, d), F32),
                   jax.ShapeDtypeStruct((n, 2 * tr // LANES, LANES), jnp.int32),
                   jax.ShapeDtypeStruct((n, 2, tr), F32),
                   jax.ShapeDtypeStruct((N_EXPERTS, LANES), jnp.int32)],
        scratch_shapes=[pltpu.VMEM((N_EXPERTS, LANES), F32),
                        pltpu.SMEM((1, 2 * tr // LANES, LANES), jnp.int32),
                        pltpu.VMEM((2, tr, d), F32),
                        pltpu.SemaphoreType.DMA, pltpu.SemaphoreType.DMA((2,))],
        compiler_params=pltpu.CompilerParams(
            dimension_semantics=("arbitrary",), vmem_limit_bytes=VMEM_LIMIT),
        name="route_dispatch",
    )(xn, w_router.T, tri)


def _combine_kernel(pos_ref, pos_next_ref, h_ref, gate_ref, gfin_ref, y_ref, out_ref, buf, sem,
                    *, tc, n):
    i = pl.program_id(0)
    slot = i % 2

    def issue(p_ref, s):
        _for_each_routed_row(tc, p_ref, lambda r, kk, p: pltpu.make_async_copy(
            y_ref.at[pl.ds(p, 1)], buf.at[s, kk, pl.ds(r, 1)], sem.at[s]).start(kk))

    @pl.when(i == 0)
    def _():
        issue(pos_ref, 0)

    for s in range(2):
        @pl.when((i + 1 < n) & (slot == 1 - s))
        def _(s=s):
            issue(pos_next_ref, s)

    for kk in range(2):
        pltpu.make_async_copy(y_ref.at[pl.ds(0, tc)], buf.at[slot, kk], sem.at[slot]).wait()
    gates = gate_ref[...]
    hn = h_ref[...] + (gates[:, 0:1] * buf[slot, 0] + gates[:, 1:2] * buf[slot, 1])
    out_ref[...] = _rmsnorm(hn, gfin_ref[...])


def _combine_call(h, y, pos, gates, gfin):
    t, d = h.shape
    tc = ROW_TILE
    n = t // tc
    smem_pos = lambda index_map: pl.BlockSpec((None, 2 * tc // LANES, LANES), index_map,
                                              memory_space=pltpu.SMEM)
    rows = lambda width: pl.BlockSpec((tc, width), lambda i: (i, 0))
    return pl.pallas_call(
        functools.partial(_combine_kernel, tc=tc, n=n),
        grid=(n,),
        in_specs=[smem_pos(lambda i: (i, 0, 0)),
                  smem_pos(lambda i: (jnp.minimum(i + 1, n - 1), 0, 0)),
                  rows(d), rows(2),
                  pl.BlockSpec((1, d), lambda i: (0, 0)),
                  pl.BlockSpec(memory_space=pl.ANY)],
        out_specs=rows(d),
        out_shape=jax.ShapeDtypeStruct((t, d), F32),
        scratch_shapes=[pltpu.VMEM((2, 2, tc, d), F32), pltpu.SemaphoreType.DMA((2,))],
        compiler_params=pltpu.CompilerParams(
            dimension_semantics=("arbitrary",), vmem_limit_bytes=VMEM_LIMIT),
        name="combine",
    )(pos, pos, h, gates, gfin, y)


def _moe(h, xn, w_router, wg, wu, wd, gfin):
    t, d = h.shape
    tm = FFN_TILE
    xs, pos, gates, counts = _route_dispatch_call(xn, w_router)
    counts = counts[:, 0]

    n_grid = 2 * t // tm + N_EXPERTS
    tiles_of = (counts + tm - 1) // tm
    tile_end = jnp.cumsum(tiles_of)
    ids = jnp.arange(n_grid, dtype=jnp.int32)
    expert = jnp.minimum(jnp.sum(ids[:, None] >= tile_end[None, :], axis=1), N_EXPERTS - 1)
    local = ids - (tile_end - tiles_of)[expert]
    table = jnp.stack([expert, expert * (t // tm) + local,
                       jnp.clip(counts[expert] - local * tm, 0, tm)]).astype(jnp.int32)
    y = _ffn_call(xs, None, wg, wu, wd, table, tile_end[-1:].astype(jnp.int32), tm, True)
    return _combine_call(h, y, pos, gates.transpose(0, 2, 1).reshape(t, 2), gfin)


def kernel(x, meta_tokens, norm_mix_g, w_in, attn_sinks, pool_w, pool_scale, w_out, norm_ffn_g,
           dense_w_gate, dense_w_up, dense_w_down, moe_w_router, moe_w_gate, moe_w_up,
           moe_w_down, final_norm_g):
    b, s, d = x.shape
    h = x.reshape(b * s, d)
    row = lambda a: a.reshape(1, -1)
    w_in_b, w_out_b, pool_w_b = w_in.astype(BF16), w_out.astype(BF16), pool_w.astype(BF16)

    def mix(h, layer, kvu, xn_dtype, to_bf16):
        return _mix_call(h, b, attn_sinks[layer], row(norm_mix_g[layer]), w_in_b[layer],
                         kvu[:, 0:D_KV], kvu[:, D_KV:2 * D_KV], kvu[:, 2 * D_KV:],
                         pool_w_b[layer], row(pool_scale[layer]), w_out_b[layer],
                         row(norm_ffn_g[layer]), xn_dtype, to_bf16)

    kvu0, hm, xnm = _meta_call(meta_tokens, attn_sinks[0], row(norm_mix_g[0]), w_in_b[0],
                               pool_w_b[0], row(pool_scale[0]), w_out_b[0], row(norm_ffn_g[0]))
    h, xn, *dense, moe_gate = mix(h, 0, kvu0, BF16,
                                  (dense_w_gate, dense_w_up, dense_w_down, moe_w_gate[0]))
    hm = _ffn_call(xnm, hm, *dense, *_dense_table(1, N_META), N_META, False)
    kvu1, _, _ = _meta_call(hm, attn_sinks[1], row(norm_mix_g[1]), w_in_b[1], pool_w_b[1],
                            row(pool_scale[1]), w_out_b[1], row(norm_ffn_g[1]))
    h = _ffn_call(xn, h, *dense, *_dense_table(b * s // FFN_TILE, FFN_TILE), FFN_TILE, False)
    h, xn, moe_up, moe_down = mix(h, 1, kvu1, F32, (moe_w_up[0], moe_w_down[0]))
    out = _moe(h, xn, moe_w_router[0], moe_gate, moe_up, moe_down, row(final_norm_g))
    return out.reshape(b, s, d)
```

```python
import functools

import jax
import jax.numpy as jnp
from jax import lax
from jax.experimental import pallas as pl
from jax.experimental.pallas import tpu as pltpu

F32 = jnp.float32
BF16 = jnp.bfloat16

CHUNK = 64
N_META = 16
HEAD_DIM = 64
N_Q_HEADS = 8
D_ATTN = 512
D_KV = 128
D_POOL = 512
POOL_SIZES = (2, 4, 8, 16)
POOL_GROUP = 128
N_EXPERTS = 8
EPS = 1e-5
NEG_INF = -1e30
LOG2_E = 1.4426950408889634

LANES = 128
HALF = LANES // 2
Q_BLOCK = 2 * CHUNK
KEY_WINDOW = 4 * CHUNK
HALO = 2 * CHUNK
E_WIDTH = KEY_WINDOW + LANES
SOFTMAX_ROWS = 32
POOL_HALO = max(POOL_SIZES)
VMEM_LIMIT = 56 * 1024 * 1024

MIX_TILE = 512
MIX_PARTS = 2
FFN_TILE = 512
FFN_SUBTILE = 512
FFN_BLOCK = 1792
ROW_TILE = 512


def _rmsnorm(x, g):
    return x * lax.rsqrt(jnp.mean(x * x, axis=-1, keepdims=True) + EPS) * g


def _dot(a, b):
    return jnp.dot(a, b, preferred_element_type=F32)


def _dot_nt(a, b):
    return lax.dot_general(a, b, (((1,), (1,)), ((), ())), preferred_element_type=F32)


def _head_variants(t, fill=0.0):
    low = lax.broadcasted_iota(jnp.int32, t.shape, 1) < HALF
    t_r = pltpu.roll(t, HALF, axis=1)
    other = jnp.full_like(t, fill)
    return (jnp.where(low, t, other).astype(BF16),
            jnp.where(low, other, t_r).astype(BF16),
            jnp.where(low, t_r, other).astype(BF16),
            jnp.where(low, other, t).astype(BF16))


def _sink_softmax_pv(s_parts, v_parts, sink):
    m = sink
    for s in s_parts:
        m = jnp.maximum(m, jnp.max(s, axis=1, keepdims=True))
    den = jnp.exp(sink - m)
    o = None
    for s, v in zip(s_parts, v_parts):
        e = jnp.exp(s - m)
        den = den + jnp.sum(e, axis=1, keepdims=True)
        pv = _dot(e.astype(BF16), v)
        o = pv if o is None else o + pv
    return o / den


def _meta_tiles(km, vm):
    km_var = _head_variants(km)
    vm_var = _head_variants(vm, 1.0)
    zeros = lambda n: jnp.zeros((n, LANES), BF16)
    row = lax.broadcasted_iota(jnp.int32, (N_META, LANES), 0)
    low = lax.broadcasted_iota(jnp.int32, (N_META, LANES), 1) < HALF
    rest = LANES - 3 * N_META
    k_tiles, v_tiles = [], []
    for h in range(2):
        k_tiles.append(jnp.concatenate([km_var[2 * h], km_var[2 * h + 1], zeros(LANES - 2 * N_META)]))
        for par in range(2):
            den_half = low if par == 1 else jnp.logical_not(low)
            sink_rows = jnp.where((row == par) & den_half, 1.0, 0.0).astype(BF16)
            v_tiles.append(jnp.concatenate(
                [vm_var[2 * h] if par == 0 else zeros(N_META),
                 vm_var[2 * h + 1] if par == 1 else zeros(N_META), sink_rows, zeros(rest)]))
    return k_tiles, v_tiles


def _mix_kernel(*refs, tq, n_cast):
    (sinks_ref, h_ref, gmix_ref, win_ref, km_ref, vm_ref, um_ref, poolw_ref, pscale_ref,
     wout_ref, gffn_ref) = refs[:11]
    cast_in = refs[11:11 + n_cast]
    hout_ref, xn_ref = refs[11 + n_cast:13 + n_cast]
    cast_out = refs[13 + n_cast:13 + 2 * n_cast]
    q_s, kvar_s, vvar_s, uw_s, a_s, kmeta_s, vmeta_s, s_s, sm_s, e_s = refs[13 + 2 * n_cast:]
    i = pl.program_id(1)

    for src, dst in zip(cast_in, cast_out):
        dst[...] = src[...].astype(BF16)

    @pl.when(i == 0)
    def _():
        kvar_s[:, 0:HALO, :] = jnp.zeros((4, HALO, LANES), BF16)
        vvar_s[:, 0:HALO, :] = jnp.zeros((4, HALO, LANES), BF16)
        uw_s[0:POOL_HALO, :] = um_ref[...]

    tp = tq // MIX_PARTS
    for lo in range(0, tq, tp):
        xn = _rmsnorm(h_ref[lo:lo + tp, :], gmix_ref[...]).astype(BF16)
        q_s[lo:lo + tp, :] = (
            _dot(xn, win_ref[:, 0:D_ATTN]) * (HEAD_DIM ** -0.5 * LOG2_E)).astype(BF16)
        k = _dot(xn, win_ref[:, D_ATTN:D_ATTN + D_KV])
        v = _dot(xn, win_ref[:, D_ATTN + D_KV:D_ATTN + 2 * D_KV])
        uw_s[POOL_HALO + lo:POOL_HALO + lo + tp, :] = _dot(xn, win_ref[:, D_ATTN + 2 * D_KV:])
        for idx, (kk, vv) in enumerate(zip(_head_variants(k), _head_variants(v, 1.0))):
            kvar_s[idx, HALO + lo:HALO + lo + tp, :] = kk
            vvar_s[idx, HALO + lo:HALO + lo + tp, :] = vv
    k_tiles, v_tiles = _meta_tiles(km_ref[...], vm_ref[...])
    for h in range(2):
        kmeta_s[h] = k_tiles[h]
    for hp in range(4):
        vmeta_s[hp] = v_tiles[hp]

    def attend(blk):
        r0 = blk * Q_BLOCK
        q_chunk = lax.broadcasted_iota(jnp.int32, (Q_BLOCK, KEY_WINDOW), 0) // CHUNK
        k_chunk = lax.broadcasted_iota(jnp.int32, (Q_BLOCK, KEY_WINDOW), 1) // CHUNK
        valid = (k_chunk >= q_chunk) & (k_chunk <= q_chunk + 2)
        if blk == 0:
            valid = valid & (k_chunk >= jnp.where(i == 0, 2, 0))

        for j in range(D_ATTN // LANES):
            col = q_s[pl.ds(r0, Q_BLOCK), j * LANES:(j + 1) * LANES]
            for par in range(2):
                s = _dot_nt(col, kvar_s[2 * (j // 2) + par, pl.ds(r0, KEY_WINDOW), :])
                s_s[j, :, par * KEY_WINDOW:(par + 1) * KEY_WINDOW] = jnp.where(valid, s, NEG_INF)
            sm_s[j] = _dot_nt(col, kmeta_s[j // 2])

        lane = lax.broadcasted_iota(jnp.int32, (1, LANES), 1)
        for j in range(D_ATTN // LANES):
            for par in range(2):
                own_meta = (lane >= par * N_META) & (lane < (par + 1) * N_META)
                sink = sinks_ref[2 * j + par] * LOG2_E
                other = jnp.where(lane == 2 * N_META + par, sink, NEG_INF)
                for g in range(Q_BLOCK // SOFTMAX_ROWS):
                    rows = slice(g * SOFTMAX_ROWS, (g + 1) * SOFTMAX_ROWS)
                    s = s_s[j, rows, par * KEY_WINDOW:(par + 1) * KEY_WINDOW]
                    sm = jnp.where(own_meta, sm_s[j, rows, :], other)
                    m = jnp.maximum(jnp.maximum(s[:, :LANES], s[:, LANES:]), sm)
                    m = jnp.max(m, axis=1, keepdims=True)
                    e_s[j, rows, par * E_WIDTH:par * E_WIDTH + KEY_WINDOW] = (
                        jnp.exp2(s - m).astype(BF16))
                    e_s[j, rows, par * E_WIDTH + KEY_WINDOW:(par + 1) * E_WIDTH] = (
                        jnp.exp2(sm - m).astype(BF16))

        low = lax.broadcasted_iota(jnp.int32, (Q_BLOCK, LANES), 1) < HALF
        for j in range(D_ATTN // LANES):
            o = []
            for par in range(2):
                hp = 2 * (j // 2) + par
                e0 = par * E_WIDTH
                o.append(_dot(e_s[j, :, e0:e0 + KEY_WINDOW], vvar_s[hp, pl.ds(r0, KEY_WINDOW), :])
                         + _dot(e_s[j, :, e0 + KEY_WINDOW:e0 + E_WIDTH], vmeta_s[hp]))
            num = jnp.where(low, o[0], o[1])
            den = pltpu.roll(jnp.where(low, o[1], o[0]), HALF, axis=1)
            a_s[pl.ds(r0, Q_BLOCK), j * LANES:(j + 1) * LANES] = (num / den).astype(BF16)

    for blk in range(tq // Q_BLOCK):
        attend(blk)

    for lo in range(0, tq, tp):
        pooled = []
        for g, w in enumerate(POOL_SIZES):
            cs = slice(g * POOL_GROUP, (g + 1) * POOL_GROUP)
            cur = uw_s[POOL_HALO + lo:POOL_HALO + lo + tp, cs]
            acc = cur
            for lag in range(1, w):
                acc = acc + uw_s[POOL_HALO + lo - lag:POOL_HALO + lo - lag + tp, cs]
            d = acc * (1.0 / w) - cur
            pooled.append(_dot(d.astype(BF16), poolw_ref[g]) * pscale_ref[:, cs])
        p = jnp.concatenate(pooled, axis=1).astype(BF16)

        hn = (h_ref[lo:lo + tp, :] + _dot(a_s[lo:lo + tp, :], wout_ref[0:D_ATTN, :])
              + _dot(p, wout_ref[D_ATTN:, :]))
        hout_ref[lo:lo + tp, :] = hn
        xn_ref[lo:lo + tp, :] = _rmsnorm(hn, gffn_ref[...]).astype(xn_ref.dtype)

    kvar_s[:, 0:HALO, :] = kvar_s[:, tq:tq + HALO, :]
    vvar_s[:, 0:HALO, :] = vvar_s[:, tq:tq + HALO, :]
    uw_s[0:POOL_HALO, :] = uw_s[tq:tq + POOL_HALO, :]


def _mix_call(h, batch, sinks, gmix, win, km, vm, um, poolw, pscale, wout, gffn, xn_dtype,
              to_bf16=()):
    t, d = h.shape
    tq = MIX_TILE
    nt = t // batch // tq
    assert nt * tq * batch == t
    full = lambda shape: pl.BlockSpec(shape, lambda b, i: (0,) * len(shape))
    row = pl.BlockSpec((tq, d), lambda b, i: (b * nt + i, 0))
    flat = [w.reshape(-1, w.shape[-1]) for w in to_bf16]
    slabs = [pl.BlockSpec((w.shape[0] // (batch * nt), w.shape[1]), lambda b, i: (b * nt + i, 0))
             for w in flat]
    assert all(w.shape[0] % (16 * batch * nt) == 0 for w in flat)
    outs = pl.pallas_call(
        functools.partial(_mix_kernel, tq=tq, n_cast=len(flat)),
        grid=(batch, nt),
        in_specs=[
            pl.BlockSpec(memory_space=pltpu.SMEM),
            row, full((1, d)), full(win.shape), full(km.shape), full(vm.shape), full(um.shape),
            full(poolw.shape), full((1, D_POOL)), full(wout.shape), full((1, d)),
        ] + slabs,
        out_specs=[row, row] + slabs,
        out_shape=[jax.ShapeDtypeStruct((t, d), F32), jax.ShapeDtypeStruct((t, d), xn_dtype)]
        + [jax.ShapeDtypeStruct(w.shape, BF16) for w in flat],
        scratch_shapes=[
            pltpu.VMEM((tq, D_ATTN), BF16),
            pltpu.VMEM((4, HALO + tq, LANES), BF16),
            pltpu.VMEM((4, HALO + tq, LANES), BF16),
            pltpu.VMEM((POOL_HALO + tq, D_POOL), F32),
            pltpu.VMEM((tq, D_ATTN), BF16),
            pltpu.VMEM((2, LANES, LANES), BF16),
            pltpu.VMEM((4, LANES, LANES), BF16),
            pltpu.VMEM((D_ATTN // LANES, Q_BLOCK, 2 * KEY_WINDOW), F32),
            pltpu.VMEM((D_ATTN // LANES, Q_BLOCK, LANES), F32),
            pltpu.VMEM((D_ATTN // LANES, Q_BLOCK, 2 * E_WIDTH), BF16),
        ],
        compiler_params=pltpu.CompilerParams(
            dimension_semantics=("arbitrary", "arbitrary"), vmem_limit_bytes=VMEM_LIMIT),
        name="mix",
    )(sinks, h, gmix, win, km, vm, um, poolw, pscale, wout, gffn, *flat)
    return outs[0], outs[1], *[o.reshape(w.shape) for o, w in zip(outs[2:], to_bf16)]


def _meta_kernel(sinks_ref, h_ref, gmix_ref, win_ref, poolw_ref, pscale_ref, wout_ref, gffn_ref,
                 kvu_ref, hout_ref, xn_ref, uw_s):
    x = h_ref[...]
    xn = _rmsnorm(x, gmix_ref[...]).astype(BF16)
    q = (_dot(xn, win_ref[:, 0:D_ATTN]) * (HEAD_DIM ** -0.5)).astype(BF16)
    k = _dot(xn, win_ref[:, D_ATTN:D_ATTN + D_KV])
    v = _dot(xn, win_ref[:, D_ATTN + D_KV:D_ATTN + 2 * D_KV])
    u = _dot(xn, win_ref[:, D_ATTN + 2 * D_KV:])
    kvu_ref[:, 0:D_KV] = k
    kvu_ref[:, D_KV:2 * D_KV] = v
    kvu_ref[:, 2 * D_KV:] = u

    k_var = _head_variants(k)
    v_var = _head_variants(v)
    cols = []
    for j in range(D_ATTN // LANES):
        col = q[:, j * LANES:(j + 1) * LANES]
        acc = None
        for par in range(2):
            var = 2 * (j // 2) + par
            o = _sink_softmax_pv((_dot_nt(col, k_var[var]),), (v_var[var],),
                                 sinks_ref[2 * j + par])
            acc = o if acc is None else acc + o
        cols.append(acc)
    a = jnp.concatenate(cols, axis=1).astype(BF16)

    uw_s[0:POOL_HALO, :] = jnp.zeros((POOL_HALO, D_POOL), F32)
    uw_s[POOL_HALO:, :] = u
    pos = lax.broadcasted_iota(jnp.int32, (N_META, 1), 0)
    pooled = []
    for g, w in enumerate(POOL_SIZES):
        cs = slice(g * POOL_GROUP, (g + 1) * POOL_GROUP)
        cur = uw_s[POOL_HALO:, cs]
        acc = cur
        for lag in range(1, w):
            acc = acc + uw_s[POOL_HALO - lag:POOL_HALO - lag + N_META, cs]
        count = jnp.minimum(pos + 1, w).astype(F32)
        d = acc / count - cur
        pooled.append(_dot(d.astype(BF16), poolw_ref[g]) * pscale_ref[:, cs])
    p = jnp.concatenate(pooled, axis=1).astype(BF16)

    hn = x + _dot(a, wout_ref[0:D_ATTN, :]) + _dot(p, wout_ref[D_ATTN:, :])
    hout_ref[...] = hn
    xn_ref[...] = _rmsnorm(hn, gffn_ref[...]).astype(xn_ref.dtype)


def _meta_call(h, sinks, gmix, win, poolw, pscale, wout, gffn):
    n, d = h.shape
    vmem = pl.BlockSpec(memory_space=pltpu.VMEM)
    return pl.pallas_call(
        _meta_kernel,
        in_specs=[pl.BlockSpec(memory_space=pltpu.SMEM)] + [vmem] * 7,
        out_specs=[vmem, vmem, vmem],
        out_shape=[jax.ShapeDtypeStruct((n, 2 * D_KV + D_POOL), F32),
                   jax.ShapeDtypeStruct((n, d), F32),
                   jax.ShapeDtypeStruct((n, d), BF16)],
        scratch_shapes=[pltpu.VMEM((POOL_HALO + N_META, D_POOL), F32)],
        compiler_params=pltpu.CompilerParams(vmem_limit_bytes=VMEM_LIMIT),
        name="meta_mix",
    )(sinks, h, gmix, win, poolw, pscale, wout, gffn)


def _ffn_kernel(tab_ref, na_ref, x_ref, *refs, masked):
    res_ref = refs[0] if len(refs) == 5 else None
    wg_ref, wu_ref, wd_ref, out_ref = refs[-4:]
    i, j = pl.program_id(0), pl.program_id(1)

    @pl.when(i < na_ref[0])
    def _():
        @pl.when(j == 0)
        def _():
            out_ref[...] = jnp.zeros_like(out_ref) if res_ref is None else res_ref[...]

        wg, wu, wd = (w[...].astype(BF16) for w in (wg_ref, wu_ref, wd_ref))
        valid = tab_ref[2, i]
        tm = x_ref.shape[0]
        sub = min(tm, FFN_SUBTILE)
        for r0 in range(0, tm, sub):
            def part(r0=r0):
                x = x_ref[r0:r0 + sub, :]
                if masked:
                    r = r0 + lax.broadcasted_iota(jnp.int32, (sub, 1), 0)
                    x = jnp.where(r < valid, x, 0.0)
                x = x.astype(BF16)
                gate = _dot(x, wg)
                up = _dot(x, wu)
                mid = (gate * jax.nn.sigmoid(gate) * up).astype(BF16)
                out_ref[r0:r0 + sub, :] += _dot(mid, wd)

            if masked and r0 > 0:
                pl.when(valid > r0)(part)
            else:
                part()


def _ffn_call(x, res, wg, wu, wd, table, n_active, tm, masked):
    rows, d = x.shape
    ff = wg.shape[-1]
    n_grid = table.shape[1]
    nj = ff // FFN_BLOCK
    assert rows % tm == 0 and nj * FFN_BLOCK == ff

    def live(i, na):
        return jnp.maximum(jnp.minimum(i, na[0] - 1), 0)

    def ff_block(i, j, na):
        return jnp.where(i < na[0], j, nj - 1)

    row = pl.BlockSpec((tm, d), lambda i, j, tab, na: (tab[1, live(i, na)], 0))
    col_w = pl.BlockSpec((None, d, FFN_BLOCK),
                         lambda i, j, tab, na: (tab[0, live(i, na)], 0, ff_block(i, j, na)))
    row_w = pl.BlockSpec((None, FFN_BLOCK, d),
                         lambda i, j, tab, na: (tab[0, live(i, na)], ff_block(i, j, na), 0))
    acts = (x,) if res is None else (x, res)
    return pl.pallas_call(
        functools.partial(_ffn_kernel, masked=masked),
        grid_spec=pltpu.PrefetchScalarGridSpec(
            num_scalar_prefetch=2, grid=(n_grid, nj),
            in_specs=[row] * len(acts) + [col_w, col_w, row_w], out_specs=row),
        out_shape=jax.ShapeDtypeStruct((rows, d), F32),
        compiler_params=pltpu.CompilerParams(
            dimension_semantics=("arbitrary", "arbitrary"), vmem_limit_bytes=VMEM_LIMIT),
        name="ffn",
    )(table, n_active, *acts, wg, wu, wd)


def _dense_table(n_tiles, tm):
    ids = jnp.arange(n_tiles, dtype=jnp.int32)
    return (jnp.stack([jnp.zeros_like(ids), ids, jnp.full_like(ids, tm)]),
            jnp.full((1,), n_tiles, jnp.int32))


def _split_bf16(x):
    hi = x.astype(BF16)
    return hi, (x - hi.astype(F32)).astype(BF16)


def _for_each_routed_row(n_rows, pos_ref, fn):
    chunks = n_rows // LANES
    for r in range(n_rows):
        for kk in range(2):
            fn(r, kk, pos_ref[kk * chunks + r // LANES, r % LANES])


def _route_dispatch_kernel(x_ref, wr_ref, tri_ref, xs_ref, pos_ref, gate_ref, counts_ref,
                           run_s, pos_sm, xbuf, sem_pos, sem_rows, *, tr, cap, n):
    i = pl.program_id(0)

    @pl.when(i == 0)
    def _():
        run_s[...] = jnp.zeros_like(run_s)

    x_hi, x_lo = _split_bf16(x_ref[...])
    w_hi, w_lo = _split_bf16(wr_ref[...])
    logits = _dot_nt(w_hi, x_hi) + (_dot_nt(w_lo, x_hi) + _dot_nt(w_hi, x_lo))

    expert = lax.broadcasted_iota(jnp.int32, logits.shape, 0).astype(F32)
    none = float(N_EXPERTS)
    m1 = jnp.max(logits, axis=0, keepdims=True)
    e1 = jnp.min(jnp.where(logits == m1, expert, none), axis=0, keepdims=True)
    rest = jnp.where(expert == e1, -jnp.inf, logits)
    m2 = jnp.max(rest, axis=0, keepdims=True)
    e2 = jnp.min(jnp.where(rest == m2, expert, none), axis=0, keepdims=True)
    t = jnp.exp(m2 - m1)
    gate_ref[0:1, :] = 1.0 / (1.0 + t)
    gate_ref[1:2, :] = t / (1.0 + t)

    chosen = ((expert == e1) | (expert == e2)).astype(F32)
    before = _dot(chosen.astype(BF16), tri_ref[...]) + run_s[:, 0:1]
    pieces = []
    for e in (e1, e2):
        rank = jnp.sum(jnp.where(expert == e, before, 0.0), axis=0, keepdims=True)
        pos = (e * float(cap) + rank).astype(jnp.int32)
        pieces += [pos[:, c * LANES:(c + 1) * LANES] for c in range(tr // LANES)]
    pos_ref[0] = jnp.concatenate(pieces, axis=0)
    run_s[...] += jnp.sum(chosen, axis=1, keepdims=True)
    counts_ref[...] = run_s[...].astype(jnp.int32)

    to_smem = pltpu.make_async_copy(pos_ref, pos_sm, sem_pos)
    to_smem.start()

    def wait_rows(s):
        for _ in range(2):
            pltpu.make_async_copy(xbuf.at[s], xs_ref.at[pl.ds(0, tr)], sem_rows.at[s]).wait()

    slot = i % 2

    @pl.when(i >= 2)
    def _():
        wait_rows(slot)

    xbuf[slot] = x_ref[...]
    to_smem.wait()
    for s in range(2):
        @pl.when(slot == s)
        def _(s=s):
            _for_each_routed_row(tr, pos_sm.at[0], lambda r, kk, p: pltpu.make_async_copy(
                xbuf.at[s, pl.ds(r, 1)], xs_ref.at[pl.ds(p, 1)], sem_rows.at[s]).start(kk))

    @pl.when(i == n - 1)
    def _():
        wait_rows(slot)
        if n > 1:
            wait_rows(1 - slot)


def _route_dispatch_call(xn, w_router):
    t, d = xn.shape
    tr = ROW_TILE
    n = t // tr
    tri = (lax.broadcasted_iota(jnp.int32, (tr, tr), 0)
           < lax.broadcasted_iota(jnp.int32, (tr, tr), 1)).astype(BF16)
    return pl.pallas_call(
        functools.partial(_route_dispatch_kernel, tr=tr, cap=t, n=n),
        grid=(n,),
        in_specs=[pl.BlockSpec((tr, d), lambda i: (i, 0)),
                  pl.BlockSpec((N_EXPERTS, d), lambda i: (0, 0)),
                  pl.BlockSpec((tr, tr), lambda i: (0, 0))],
        out_specs=[pl.BlockSpec(memory_space=pl.ANY),
                   pl.BlockSpec((1, 2 * tr // LANES, LANES), lambda i: (i, 0, 0)),
                   pl.BlockSpec((None, 2, tr), lambda i: (i, 0, 0)),
                   pl.BlockSpec((N_EXPERTS, LANES), lambda i: (0, 0))],
        out_shape=[jax.ShapeDtypeStruct((N_EXPERTS * t, d), F32),
                   jax.ShapeDtypeStruct((n, 2 * tr // LANES, LANES), jnp.int32),
                   jax.ShapeDtypeStruct((n, 2, tr), F32),
                   jax.ShapeDtypeStruct((N_EXPERTS, LANES), jnp.int32)],
        scratch_shapes=[pltpu.VMEM((N_EXPERTS, LANES), F32),
                        pltpu.SMEM((1, 2 * tr // LANES, LANES), jnp.int32),
                        pltpu.VMEM((2, tr, d), F32),
                        pltpu.SemaphoreType.DMA, pltpu.SemaphoreType.DMA((2,))],
        compiler_params=pltpu.CompilerParams(
            dimension_semantics=("arbitrary",), vmem_limit_bytes=VMEM_LIMIT),
        name="route_dispatch",
    )(xn, w_router.T, tri)


def _combine_kernel(pos_ref, pos_next_ref, h_ref, gate_ref, gfin_ref, y_ref, out_ref, buf, sem,
                    *, tc, n):
    i = pl.program_id(0)
    slot = i % 2

    def issue(p_ref, s):
        _for_each_routed_row(tc, p_ref, lambda r, kk, p: pltpu.make_async_copy(
            y_ref.at[pl.ds(p, 1)], buf.at[s, kk, pl.ds(r, 1)], sem.at[s]).start(kk))

    @pl.when(i == 0)
    def _():
        issue(pos_ref, 0)

    for s in range(2):
        @pl.when((i + 1 < n) & (slot == 1 - s))
        def _(s=s):
            issue(pos_next_ref, s)

    for kk in range(2):
        pltpu.make_async_copy(y_ref.at[pl.ds(0, tc)], buf.at[slot, kk], sem.at[slot]).wait()
    gates = gate_ref[...]
    hn = h_ref[...] + (gates[:, 0:1] * buf[slot, 0] + gates[:, 1:2] * buf[slot, 1])
    out_ref[...] = _rmsnorm(hn, gfin_ref[...])


def _combine_call(h, y, pos, gates, gfin):
    t, d = h.shape
    tc = ROW_TILE
    n = t // tc
    smem_pos = lambda index_map: pl.BlockSpec((None, 2 * tc // LANES, LANES), index_map,
                                              memory_space=pltpu.SMEM)
    rows = lambda width: pl.BlockSpec((tc, width), lambda i: (i, 0))
    return pl.pallas_call(
        functools.partial(_combine_kernel, tc=tc, n=n),
        grid=(n,),
        in_specs=[smem_pos(lambda i: (i, 0, 0)),
                  smem_pos(lambda i: (jnp.minimum(i + 1, n - 1), 0, 0)),
                  rows(d), rows(2),
                  pl.BlockSpec((1, d), lambda i: (0, 0)),
                  pl.BlockSpec(memory_space=pl.ANY)],
        out_specs=rows(d),
        out_shape=jax.ShapeDtypeStruct((t, d), F32),
        scratch_shapes=[pltpu.VMEM((2, 2, tc, d), F32), pltpu.SemaphoreType.DMA((2,))],
        compiler_params=pltpu.CompilerParams(
            dimension_semantics=("arbitrary",), vmem_limit_bytes=VMEM_LIMIT),
        name="combine",
    )(pos, pos, h, gates, gfin, y)


def _moe(h, xn, w_router, wg, wu, wd, gfin):
    t, d = h.shape
    tm = FFN_TILE
    xs, pos, gates, counts = _route_dispatch_call(xn, w_router)
    counts = counts[:, 0]

    n_grid = 2 * t // tm + N_EXPERTS
    tiles_of = (counts + tm - 1) // tm
    tile_end = jnp.cumsum(tiles_of)
    ids = jnp.arange(n_grid, dtype=jnp.int32)
    expert = jnp.minimum(jnp.sum(ids[:, None] >= tile_end[None, :], axis=1), N_EXPERTS - 1)
    local = ids - (tile_end - tiles_of)[expert]
    table = jnp.stack([expert, expert * (t // tm) + local,
                       jnp.clip(counts[expert] - local * tm, 0, tm)]).astype(jnp.int32)
    y = _ffn_call(xs, None, wg, wu, wd, table, tile_end[-1:].astype(jnp.int32), tm, True)
    return _combine_call(h, y, pos, gates.transpose(0, 2, 1).reshape(t, 2), gfin)


def kernel(x, meta_tokens, norm_mix_g, w_in, attn_sinks, pool_w, pool_scale, w_out, norm_ffn_g,
           dense_w_gate, dense_w_up, dense_w_down, moe_w_router, moe_w_gate, moe_w_up,
           moe_w_down, final_norm_g):
    b, s, d = x.shape
    h = x.reshape(b * s, d)
    row = lambda a: a.reshape(1, -1)
    w_in_b, w_out_b, pool_w_b = w_in.astype(BF16), w_out.astype(BF16), pool_w.astype(BF16)

    def mix(h, layer, kvu, xn_dtype, to_bf16):
        return _mix_call(h, b, attn_sinks[layer], row(norm_mix_g[layer]), w_in_b[layer],
                         kvu[:, 0:D_KV], kvu[:, D_KV:2 * D_KV], kvu[:, 2 * D_KV:],
                         pool_w_b[layer], row(pool_scale[layer]), w_out_b[layer],
                         row(norm_ffn_g[layer]), xn_dtype, to_bf16)

    kvu0, hm, xnm = _meta_call(meta_tokens, attn_sinks[0], row(norm_mix_g[0]), w_in_b[0],
                               pool_w_b[0], row(pool_scale[0]), w_out_b[0], row(norm_ffn_g[0]))
    h, xn, *dense, moe_gate = mix(h, 0, kvu0, BF16,
                                  (dense_w_gate, dense_w_up, dense_w_down, moe_w_gate[0]))
    hm = _ffn_call(xnm, hm, *dense, *_dense_table(1, N_META), N_META, False)
    kvu1, _, _ = _meta_call(hm, attn_sinks[1], row(norm_mix_g[1]), w_in_b[1], pool_w_b[1],
                            row(pool_scale[1]), w_out_b[1], row(norm_ffn_g[1]))
    h = _ffn_call(xn, h, *dense, *_dense_table(b * s // FFN_TILE, FFN_TILE), FFN_TILE, False)
    h, xn, moe_up, moe_down = mix(h, 1, kvu1, F32, (moe_w_up[0], moe_w_down[0]))
    out = _moe(h, xn, moe_w_router[0], moe_gate, moe_up, moe_down, row(final_norm_g))
    return out.reshape(b, s, d)
```

```python
import functools

import jax
import jax.numpy as jnp
from jax import lax
from jax.experimental import pallas as pl
from jax.experimental.pallas import tpu as pltpu

F32 = jnp.float32
BF16 = jnp.bfloat16

CHUNK = 64
N_META = 16
HEAD_DIM = 64
N_Q_HEADS = 8
D_ATTN = 512
D_KV = 128
D_POOL = 512
POOL_SIZES = (2, 4, 8, 16)
POOL_GROUP = 128
N_EXPERTS = 8
EPS = 1e-5
NEG_INF = -1e30
LOG2_E = 1.4426950408889634

LANES = 128
HALF = LANES // 2
Q_BLOCK = 2 * CHUNK
KEY_WINDOW = 4 * CHUNK
HALO = 2 * CHUNK
E_WIDTH = KEY_WINDOW + LANES
SOFTMAX_ROWS = 32
POOL_HALO = max(POOL_SIZES)
VMEM_LIMIT = 56 * 1024 * 1024

MIX_TILE = 512
MIX_PARTS = 2
FFN_TILE = 512
FFN_SUBTILE = 512
FFN_BLOCK = 1792
ROW_TILE = 512


def _rmsnorm(x, g):
    return x * lax.rsqrt(jnp.mean(x * x, axis=-1, keepdims=True) + EPS) * g


def _dot(a, b):
    return jnp.dot(a, b, preferred_element_type=F32)


def _dot_nt(a, b):
    return lax.dot_general(a, b, (((1,), (1,)), ((), ())), preferred_element_type=F32)


def _head_variants(t, fill=0.0):
    low = lax.broadcasted_iota(jnp.int32, t.shape, 1) < HALF
    t_r = pltpu.roll(t, HALF, axis=1)
    other = jnp.full_like(t, fill)
    return (jnp.where(low, t, other).astype(BF16),
            jnp.where(low, other, t_r).astype(BF16),
            jnp.where(low, t_r, other).astype(BF16),
            jnp.where(low, other, t).astype(BF16))


def _sink_softmax_pv(s_parts, v_parts, sink):
    m = sink
    for s in s_parts:
        m = jnp.maximum(m, jnp.max(s, axis=1, keepdims=True))
    den = jnp.exp(sink - m)
    o = None
    for s, v in zip(s_parts, v_parts):
        e = jnp.exp(s - m)
        den = den + jnp.sum(e, axis=1, keepdims=True)
        pv = _dot(e.astype(BF16), v)
        o = pv if o is None else o + pv
    return o / den


def _meta_tiles(km, vm):
    km_var = _head_variants(km)
    vm_var = _head_variants(vm, 1.0)
    zeros = lambda n: jnp.zeros((n, LANES), BF16)
    row = lax.broadcasted_iota(jnp.int32, (N_META, LANES), 0)
    low = lax.broadcasted_iota(jnp.int32, (N_META, LANES), 1) < HALF
    rest = LANES - 3 * N_META
    k_tiles, v_tiles = [], []
    for h in range(2):
        k_tiles.append(jnp.concatenate([km_var[2 * h], km_var[2 * h + 1], zeros(LANES - 2 * N_META)]))
        for par in range(2):
            den_half = low if par == 1 else jnp.logical_not(low)
            sink_rows = jnp.where((row == par) & den_half, 1.0, 0.0).astype(BF16)
            v_tiles.append(jnp.concatenate(
                [vm_var[2 * h] if par == 0 else zeros(N_META),
                 vm_var[2 * h + 1] if par == 1 else zeros(N_META), sink_rows, zeros(rest)]))
    return k_tiles, v_tiles


def _mix_kernel(*refs, tq, n_cast, layer):
    (sinks_ref, h_ref, gmix_ref, win_ref, kvu_ref, poolw_ref, pscale_ref, wout_ref,
     gffn_ref) = refs[:9]
    cast_in = refs[9:9 + n_cast]
    hout_ref, xn_ref = refs[9 + n_cast:11 + n_cast]
    cast_out = refs[11 + n_cast:11 + 2 * n_cast]
    q_s, kvar_s, vvar_s, uw_s, a_s, kmeta_s, vmeta_s, s_s, sm_s, e_s = refs[11 + 2 * n_cast:]
    i = pl.program_id(1)

    for src, dst in zip(cast_in, cast_out):
        dst[...] = src[...].astype(BF16)

    @pl.when(i == 0)
    def _():
        kvar_s[:, 0:HALO, :] = jnp.zeros((4, HALO, LANES), BF16)
        vvar_s[:, 0:HALO, :] = jnp.zeros((4, HALO, LANES), BF16)
        uw_s[0:POOL_HALO, :] = kvu_ref[:, 2 * D_KV:]

    tp = tq // MIX_PARTS
    for lo in range(0, tq, tp):
        xn = _rmsnorm(h_ref[lo:lo + tp, :], gmix_ref[...]).astype(BF16)
        q_s[lo:lo + tp, :] = (
            _dot(xn, win_ref[:, 0:D_ATTN]) * (HEAD_DIM ** -0.5 * LOG2_E)).astype(BF16)
        k = _dot(xn, win_ref[:, D_ATTN:D_ATTN + D_KV])
        v = _dot(xn, win_ref[:, D_ATTN + D_KV:D_ATTN + 2 * D_KV])
        uw_s[POOL_HALO + lo:POOL_HALO + lo + tp, :] = _dot(xn, win_ref[:, D_ATTN + 2 * D_KV:])
        for idx, (kk, vv) in enumerate(zip(_head_variants(k), _head_variants(v, 1.0))):
            kvar_s[idx, HALO + lo:HALO + lo + tp, :] = kk
            vvar_s[idx, HALO + lo:HALO + lo + tp, :] = vv
    k_tiles, v_tiles = _meta_tiles(kvu_ref[:, 0:D_KV], kvu_ref[:, D_KV:2 * D_KV])
    for h in range(2):
        kmeta_s[h] = k_tiles[h]
    for hp in range(4):
        vmeta_s[hp] = v_tiles[hp]

    def attend(blk):
        r0 = blk * Q_BLOCK
        q_chunk = lax.broadcasted_iota(jnp.int32, (Q_BLOCK, KEY_WINDOW), 0) // CHUNK
        k_chunk = lax.broadcasted_iota(jnp.int32, (Q_BLOCK, KEY_WINDOW), 1) // CHUNK
        valid = (k_chunk >= q_chunk) & (k_chunk <= q_chunk + 2)
        if blk == 0:
            valid = valid & (k_chunk >= jnp.where(i == 0, 2, 0))

        for j in range(D_ATTN // LANES):
            col = q_s[pl.ds(r0, Q_BLOCK), j * LANES:(j + 1) * LANES]
            for par in range(2):
                s = _dot_nt(col, kvar_s[2 * (j // 2) + par, pl.ds(r0, KEY_WINDOW), :])
                s_s[j, :, par * KEY_WINDOW:(par + 1) * KEY_WINDOW] = jnp.where(valid, s, NEG_INF)
            sm_s[j] = _dot_nt(col, kmeta_s[j // 2])

        lane = lax.broadcasted_iota(jnp.int32, (1, LANES), 1)
        for j in range(D_ATTN // LANES):
            for par in range(2):
                own_meta = (lane >= par * N_META) & (lane < (par + 1) * N_META)
                sink = sinks_ref[layer, 2 * j + par] * LOG2_E
                other = jnp.where(lane == 2 * N_META + par, sink, NEG_INF)
                for g in range(Q_BLOCK // SOFTMAX_ROWS):
                    rows = slice(g * SOFTMAX_ROWS, (g + 1) * SOFTMAX_ROWS)
                    s = s_s[j, rows, par * KEY_WINDOW:(par + 1) * KEY_WINDOW]
                    sm = jnp.where(own_meta, sm_s[j, rows, :], other)
                    m = jnp.maximum(jnp.maximum(s[:, :LANES], s[:, LANES:]), sm)
                    m = jnp.max(m, axis=1, keepdims=True)
                    e_s[j, rows, par * E_WIDTH:par * E_WIDTH + KEY_WINDOW] = (
                        jnp.exp2(s - m).astype(BF16))
                    e_s[j, rows, par * E_WIDTH + KEY_WINDOW:(par + 1) * E_WIDTH] = (
                        jnp.exp2(sm - m).astype(BF16))

        low = lax.broadcasted_iota(jnp.int32, (Q_BLOCK, LANES), 1) < HALF
        for j in range(D_ATTN // LANES):
            o = []
            for par in range(2):
                hp = 2 * (j // 2) + par
                e0 = par * E_WIDTH
                o.append(_dot(e_s[j, :, e0:e0 + KEY_WINDOW], vvar_s[hp, pl.ds(r0, KEY_WINDOW), :])
                         + _dot(e_s[j, :, e0 + KEY_WINDOW:e0 + E_WIDTH], vmeta_s[hp]))
            num = jnp.where(low, o[0], o[1])
            den = pltpu.roll(jnp.where(low, o[1], o[0]), HALF, axis=1)
            a_s[pl.ds(r0, Q_BLOCK), j * LANES:(j + 1) * LANES] = (num / den).astype(BF16)

    for blk in range(tq // Q_BLOCK):
        attend(blk)

    for lo in range(0, tq, tp):
        pooled = []
        for g, w in enumerate(POOL_SIZES):
            cs = slice(g * POOL_GROUP, (g + 1) * POOL_GROUP)
            cur = uw_s[POOL_HALO + lo:POOL_HALO + lo + tp, cs]
            acc = cur
            for lag in range(1, w):
                acc = acc + uw_s[POOL_HALO + lo - lag:POOL_HALO + lo - lag + tp, cs]
            d = acc * (1.0 / w) - cur
            pooled.append(_dot(d.astype(BF16), poolw_ref[g]) * pscale_ref[:, cs])
        p = jnp.concatenate(pooled, axis=1).astype(BF16)

        hn = (h_ref[lo:lo + tp, :] + _dot(a_s[lo:lo + tp, :], wout_ref[0:D_ATTN, :])
              + _dot(p, wout_ref[D_ATTN:, :]))
        hout_ref[lo:lo + tp, :] = hn
        xn_ref[lo:lo + tp, :] = _rmsnorm(hn, gffn_ref[...]).astype(xn_ref.dtype)

    kvar_s[:, 0:HALO, :] = kvar_s[:, tq:tq + HALO, :]
    vvar_s[:, 0:HALO, :] = vvar_s[:, tq:tq + HALO, :]
    uw_s[0:POOL_HALO, :] = uw_s[tq:tq + POOL_HALO, :]


def _layer_spec(w, layer):
    return pl.BlockSpec((None,) + w.shape[1:], lambda *_: (layer,) + (0,) * (w.ndim - 1))


def _mix_call(h, batch, layer, sinks, gmix, win, kvu, poolw, pscale, wout, gffn, xn_dtype,
              to_bf16=()):
    t, d = h.shape
    tq = MIX_TILE
    nt = t // batch // tq
    assert nt * tq * batch == t
    row = pl.BlockSpec((tq, d), lambda b, i: (b * nt + i, 0))
    flat = [w.reshape(-1, w.shape[-1]) for w in to_bf16]
    slabs = [pl.BlockSpec((w.shape[0] // (batch * nt), w.shape[1]), lambda b, i: (b * nt + i, 0))
             for w in flat]
    assert all(w.shape[0] % (16 * batch * nt) == 0 for w in flat)
    outs = pl.pallas_call(
        functools.partial(_mix_kernel, tq=tq, n_cast=len(flat), layer=layer),
        grid=(batch, nt),
        in_specs=[
            pl.BlockSpec(memory_space=pltpu.SMEM), row, _layer_spec(gmix, layer),
            _layer_spec(win, layer), pl.BlockSpec(kvu.shape, lambda b, i: (0, 0)),
            _layer_spec(poolw, layer), _layer_spec(pscale, layer), _layer_spec(wout, layer),
            _layer_spec(gffn, layer),
        ] + slabs,
        out_specs=[row, row] + slabs,
        out_shape=[jax.ShapeDtypeStruct((t, d), F32), jax.ShapeDtypeStruct((t, d), xn_dtype)]
        + [jax.ShapeDtypeStruct(w.shape, BF16) for w in flat],
        scratch_shapes=[
            pltpu.VMEM((tq, D_ATTN), BF16),
            pltpu.VMEM((4, HALO + tq, LANES), BF16),
            pltpu.VMEM((4, HALO + tq, LANES), BF16),
            pltpu.VMEM((POOL_HALO + tq, D_POOL), F32),
            pltpu.VMEM((tq, D_ATTN), BF16),
            pltpu.VMEM((2, LANES, LANES), BF16),
            pltpu.VMEM((4, LANES, LANES), BF16),
            pltpu.VMEM((D_ATTN // LANES, Q_BLOCK, 2 * KEY_WINDOW), F32),
            pltpu.VMEM((D_ATTN // LANES, Q_BLOCK, LANES), F32),
            pltpu.VMEM((D_ATTN // LANES, Q_BLOCK, 2 * E_WIDTH), BF16),
        ],
        compiler_params=pltpu.CompilerParams(
            dimension_semantics=("arbitrary", "arbitrary"), vmem_limit_bytes=VMEM_LIMIT),
        name="mix",
    )(sinks, h, gmix, win, kvu, poolw, pscale, wout, gffn, *flat)
    return outs[0], outs[1], *[o.reshape(w.shape) for o, w in zip(outs[2:], to_bf16)]


def _meta_kernel(sinks_ref, h_ref, gmix_ref, win_ref, poolw_ref, pscale_ref, wout_ref, gffn_ref,
                 kvu_ref, hout_ref, xn_ref, uw_s, *, layer):
    x = h_ref[...]
    xn = _rmsnorm(x, gmix_ref[...]).astype(BF16)
    q = (_dot(xn, win_ref[:, 0:D_ATTN]) * (HEAD_DIM ** -0.5)).astype(BF16)
    k = _dot(xn, win_ref[:, D_ATTN:D_ATTN + D_KV])
    v = _dot(xn, win_ref[:, D_ATTN + D_KV:D_ATTN + 2 * D_KV])
    u = _dot(xn, win_ref[:, D_ATTN + 2 * D_KV:])
    kvu_ref[:, 0:D_KV] = k
    kvu_ref[:, D_KV:2 * D_KV] = v
    kvu_ref[:, 2 * D_KV:] = u

    k_var = _head_variants(k)
    v_var = _head_variants(v)
    cols = []
    for j in range(D_ATTN // LANES):
        col = q[:, j * LANES:(j + 1) * LANES]
        acc = None
        for par in range(2):
            var = 2 * (j // 2) + par
            o = _sink_softmax_pv((_dot_nt(col, k_var[var]),), (v_var[var],),
                                 sinks_ref[layer, 2 * j + par])
            acc = o if acc is None else acc + o
        cols.append(acc)
    a = jnp.concatenate(cols, axis=1).astype(BF16)

    uw_s[0:POOL_HALO, :] = jnp.zeros((POOL_HALO, D_POOL), F32)
    uw_s[POOL_HALO:, :] = u
    pos = lax.broadcasted_iota(jnp.int32, (N_META, 1), 0)
    pooled = []
    for g, w in enumerate(POOL_SIZES):
        cs = slice(g * POOL_GROUP, (g + 1) * POOL_GROUP)
        cur = uw_s[POOL_HALO:, cs]
        acc = cur
        for lag in range(1, w):
            acc = acc + uw_s[POOL_HALO - lag:POOL_HALO - lag + N_META, cs]
        count = jnp.minimum(pos + 1, w).astype(F32)
        d = acc / count - cur
        pooled.append(_dot(d.astype(BF16), poolw_ref[g]) * pscale_ref[:, cs])
    p = jnp.concatenate(pooled, axis=1).astype(BF16)

    hn = x + _dot(a, wout_ref[0:D_ATTN, :]) + _dot(p, wout_ref[D_ATTN:, :])
    hout_ref[...] = hn
    xn_ref[...] = _rmsnorm(hn, gffn_ref[...]).astype(xn_ref.dtype)


def _meta_call(h, layer, sinks, gmix, win, poolw, pscale, wout, gffn):
    n, d = h.shape
    whole = lambda shape: pl.BlockSpec(shape, lambda i: (0,) * len(shape))
    params = (gmix, win, poolw, pscale, wout, gffn)
    out_shapes = ((n, 2 * D_KV + D_POOL), (n, d), (n, d))
    return pl.pallas_call(
        functools.partial(_meta_kernel, layer=layer),
        grid=(1,),
        in_specs=[pl.BlockSpec(memory_space=pltpu.SMEM), whole(h.shape)]
        + [_layer_spec(w, layer) for w in params],
        out_specs=[whole(s) for s in out_shapes],
        out_shape=[jax.ShapeDtypeStruct(s, dt) for s, dt in zip(out_shapes, (F32, F32, BF16))],
        scratch_shapes=[pltpu.VMEM((POOL_HALO + N_META, D_POOL), F32)],
        compiler_params=pltpu.CompilerParams(
            dimension_semantics=("arbitrary",), vmem_limit_bytes=VMEM_LIMIT),
        name="meta_mix",
    )(sinks, h, *params)


def _ffn_kernel(tab_ref, na_ref, x_ref, *refs, masked):
    res_ref = refs[0] if len(refs) == 5 else None
    wg_ref, wu_ref, wd_ref, out_ref = refs[-4:]
    i, j = pl.program_id(0), pl.program_id(1)

    @pl.when(i < na_ref[0])
    def _():
        @pl.when(j == 0)
        def _():
            out_ref[...] = jnp.zeros_like(out_ref) if res_ref is None else res_ref[...]

        wg, wu, wd = (w[...].astype(BF16) for w in (wg_ref, wu_ref, wd_ref))
        valid = tab_ref[2, i]
        tm = x_ref.shape[0]
        sub = min(tm, FFN_SUBTILE)
        for r0 in range(0, tm, sub):
            def part(r0=r0):
                x = x_ref[r0:r0 + sub, :]
                if masked:
                    r = r0 + lax.broadcasted_iota(jnp.int32, (sub, 1), 0)
                    x = jnp.where(r < valid, x, 0.0)
                x = x.astype(BF16)
                gate = _dot(x, wg)
                up = _dot(x, wu)
                mid = (gate * jax.nn.sigmoid(gate) * up).astype(BF16)
                out_ref[r0:r0 + sub, :] += _dot(mid, wd)

            if masked and r0 > 0:
                pl.when(valid > r0)(part)
            else:
                part()


def _ffn_call(x, res, wg, wu, wd, table, n_active, tm, masked):
    rows, d = x.shape
    ff = wg.shape[-1]
    n_grid = table.shape[1]
    nj = ff // FFN_BLOCK
    assert rows % tm == 0 and nj * FFN_BLOCK == ff

    def live(i, na):
        return jnp.maximum(jnp.minimum(i, na[0] - 1), 0)

    def ff_block(i, j, na):
        return jnp.where(i < na[0], j, nj - 1)

    row = pl.BlockSpec((tm, d), lambda i, j, tab, na: (tab[1, live(i, na)], 0))
    col_w = pl.BlockSpec((None, d, FFN_BLOCK),
                         lambda i, j, tab, na: (tab[0, live(i, na)], 0, ff_block(i, j, na)))
    row_w = pl.BlockSpec((None, FFN_BLOCK, d),
                         lambda i, j, tab, na: (tab[0, live(i, na)], ff_block(i, j, na), 0))
    acts = (x,) if res is None else (x, res)
    return pl.pallas_call(
        functools.partial(_ffn_kernel, masked=masked),
        grid_spec=pltpu.PrefetchScalarGridSpec(
            num_scalar_prefetch=2, grid=(n_grid, nj),
            in_specs=[row] * len(acts) + [col_w, col_w, row_w], out_specs=row),
        out_shape=jax.ShapeDtypeStruct((rows, d), F32),
        compiler_params=pltpu.CompilerParams(
            dimension_semantics=("arbitrary", "arbitrary"), vmem_limit_bytes=VMEM_LIMIT),
        name="ffn",
    )(table, n_active, *acts, wg, wu, wd)


def _dense_table(n_tiles, tm):
    ids = jnp.arange(n_tiles, dtype=jnp.int32)
    return (jnp.stack([jnp.zeros_like(ids), ids, jnp.full_like(ids, tm)]),
            jnp.full((1,), n_tiles, jnp.int32))


def _split_bf16(x):
    hi = x.astype(BF16)
    return hi, (x - hi.astype(F32)).astype(BF16)


def _for_each_routed_row(n_rows, pos_ref, fn):
    chunks = n_rows // LANES
    for r in range(n_rows):
        for kk in range(2):
            fn(r, kk, pos_ref[kk * chunks + r // LANES, r % LANES])


def _route_dispatch_kernel(x_ref, wr_ref, tri_ref, xs_ref, pos_ref, gate_ref, counts_ref,
                           run_s, pos_sm, xbuf, sem_pos, sem_rows, *, tr, cap, n):
    i = pl.program_id(0)

    @pl.when(i == 0)
    def _():
        run_s[...] = jnp.zeros_like(run_s)

    x_hi, x_lo = _split_bf16(x_ref[...])
    w_hi, w_lo = _split_bf16(wr_ref[...])
    logits = _dot_nt(w_hi, x_hi) + (_dot_nt(w_lo, x_hi) + _dot_nt(w_hi, x_lo))

    expert = lax.broadcasted_iota(jnp.int32, logits.shape, 0).astype(F32)
    none = float(N_EXPERTS)
    m1 = jnp.max(logits, axis=0, keepdims=True)
    e1 = jnp.min(jnp.where(logits == m1, expert, none), axis=0, keepdims=True)
    rest = jnp.where(expert == e1, -jnp.inf, logits)
    m2 = jnp.max(rest, axis=0, keepdims=True)
    e2 = jnp.min(jnp.where(rest == m2, expert, none), axis=0, keepdims=True)
    t = jnp.exp(m2 - m1)
    gate_ref[0:1, :] = 1.0 / (1.0 + t)
    gate_ref[1:2, :] = t / (1.0 + t)

    chosen = ((expert == e1) | (expert == e2)).astype(F32)
    before = _dot(chosen.astype(BF16), tri_ref[...]) + run_s[:, 0:1]
    pieces = []
    for e in (e1, e2):
        rank = jnp.sum(jnp.where(expert == e, before, 0.0), axis=0, keepdims=True)
        pos = (e * float(cap) + rank).astype(jnp.int32)
        pieces += [pos[:, c * LANES:(c + 1) * LANES] for c in range(tr // LANES)]
    pos_ref[0] = jnp.concatenate(pieces, axis=0)
    run_s[...] += jnp.sum(chosen, axis=1, keepdims=True)
    counts_ref[...] = run_s[...].astype(jnp.int32)

    to_smem = pltpu.make_async_copy(pos_ref, pos_sm, sem_pos)
    to_smem.start()

    def wait_rows(s):
        for _ in range(2):
            pltpu.make_async_copy(xbuf.at[s], xs_ref.at[pl.ds(0, tr)], sem_rows.at[s]).wait()

    slot = i % 2

    @pl.when(i >= 2)
    def _():
        wait_rows(slot)

    xbuf[slot] = x_ref[...]
    to_smem.wait()
    for s in range(2):
        @pl.when(slot == s)
        def _(s=s):
            _for_each_routed_row(tr, pos_sm.at[0], lambda r, kk, p: pltpu.make_async_copy(
                xbuf.at[s, pl.ds(r, 1)], xs_ref.at[pl.ds(p, 1)], sem_rows.at[s]).start(kk))

    @pl.when(i == n - 1)
    def _():
        wait_rows(slot)
        if n > 1:
            wait_rows(1 - slot)


def _route_dispatch_call(xn, w_router):
    t, d = xn.shape
    tr = ROW_TILE
    n = t // tr
    tri = (lax.broadcasted_iota(jnp.int32, (tr, tr), 0)
           < lax.broadcasted_iota(jnp.int32, (tr, tr), 1)).astype(BF16)
    return pl.pallas_call(
        functools.partial(_route_dispatch_kernel, tr=tr, cap=t, n=n),
        grid=(n,),
        in_specs=[pl.BlockSpec((tr, d), lambda i: (i, 0)),
                  pl.BlockSpec((N_EXPERTS, d), lambda i: (0, 0)),
                  pl.BlockSpec((tr, tr), lambda i: (0, 0))],
        out_specs=[pl.BlockSpec(memory_space=pl.ANY),
                   pl.BlockSpec((1, 2 * tr // LANES, LANES), lambda i: (i, 0, 0)),
                   pl.BlockSpec((None, 2, tr), lambda i: (i, 0, 0)),
                   pl.BlockSpec((N_EXPERTS, LANES), lambda i: (0, 0))],
        out_shape=[jax.ShapeDtypeStruct((N_EXPERTS * t, d), F32),
                   jax.ShapeDtypeStruct((n, 2 * tr // LANES, LANES), jnp.int32),
                   jax.ShapeDtypeStruct((n, 2, tr), F32),
                   jax.ShapeDtypeStruct((N_EXPERTS, LANES), jnp.int32)],
        scratch_shapes=[pltpu.VMEM((N_EXPERTS, LANES), F32),
                        pltpu.SMEM((1, 2 * tr // LANES, LANES), jnp.int32),
                        pltpu.VMEM((2, tr, d), F32),
                        pltpu.SemaphoreType.DMA, pltpu.SemaphoreType.DMA((2,))],
        compiler_params=pltpu.CompilerParams(
            dimension_semantics=("arbitrary",), vmem_limit_bytes=VMEM_LIMIT),
        name="route_dispatch",
    )(xn, w_router.T, tri)


def _combine_kernel(pos_ref, pos_next_ref, h_ref, gate_ref, gfin_ref, y_ref, out_ref, buf, sem,
                    *, tc, n):
    i = pl.program_id(0)
    slot = i % 2

    def issue(p_ref, s):
        _for_each_routed_row(tc, p_ref, lambda r, kk, p: pltpu.make_async_copy(
            y_ref.at[pl.ds(p, 1)], buf.at[s, kk, pl.ds(r, 1)], sem.at[s]).start(kk))

    @pl.when(i == 0)
    def _():
        issue(pos_ref, 0)

    for s in range(2):
        @pl.when((i + 1 < n) & (slot == 1 - s))
        def _(s=s):
            issue(pos_next_ref, s)

    for kk in range(2):
        pltpu.make_async_copy(y_ref.at[pl.ds(0, tc)], buf.at[slot, kk], sem.at[slot]).wait()
    gates = gate_ref[...]
    hn = h_ref[...] + (gates[:, 0:1] * buf[slot, 0] + gates[:, 1:2] * buf[slot, 1])
    out_ref[...] = _rmsnorm(hn, gfin_ref[...])


def _combine_call(h, y, pos, gates, gfin):
    t, d = h.shape
    tc = ROW_TILE
    n = t // tc
    smem_pos = lambda index_map: pl.BlockSpec((None, 2 * tc // LANES, LANES), index_map,
                                              memory_space=pltpu.SMEM)
    rows = lambda width: pl.BlockSpec((tc, width), lambda i: (i, 0))
    return pl.pallas_call(
        functools.partial(_combine_kernel, tc=tc, n=n),
        grid=(n,),
        in_specs=[smem_pos(lambda i: (i, 0, 0)),
                  smem_pos(lambda i: (jnp.minimum(i + 1, n - 1), 0, 0)),
                  rows(d), rows(2),
                  pl.BlockSpec((1, d), lambda i: (0, 0)),
                  pl.BlockSpec(memory_space=pl.ANY)],
        out_specs=rows(d),
        out_shape=jax.ShapeDtypeStruct((t, d), F32),
        scratch_shapes=[pltpu.VMEM((2, 2, tc, d), F32), pltpu.SemaphoreType.DMA((2,))],
        compiler_params=pltpu.CompilerParams(
            dimension_semantics=("arbitrary",), vmem_limit_bytes=VMEM_LIMIT),
        name="combine",
    )(pos, pos, h, gates, gfin, y)


def _moe(h, xn, w_router, wg, wu, wd, gfin):
    t, d = h.shape
    tm = FFN_TILE
    xs, pos, gates, counts = _route_dispatch_call(xn, w_router)
    counts = counts[:, 0]

    n_grid = 2 * t // tm + N_EXPERTS
    tiles_of = (counts + tm - 1) // tm
    tile_end = jnp.cumsum(tiles_of)
    ids = jnp.arange(n_grid, dtype=jnp.int32)
    expert = jnp.minimum(jnp.sum(ids[:, None] >= tile_end[None, :], axis=1), N_EXPERTS - 1)
    local = ids - (tile_end - tiles_of)[expert]
    table = jnp.stack([expert, expert * (t // tm) + local,
                       jnp.clip(counts[expert] - local * tm, 0, tm)]).astype(jnp.int32)
    y = _ffn_call(xs, None, wg, wu, wd, table, tile_end[-1:].astype(jnp.int32), tm, True)
    return _combine_call(h, y, pos, gates.transpose(0, 2, 1).reshape(t, 2), gfin)


def kernel(x, meta_tokens, norm_mix_g, w_in, attn_sinks, pool_w, pool_scale, w_out, norm_ffn_g,
           dense_w_gate, dense_w_up, dense_w_down, moe_w_router, moe_w_gate, moe_w_up,
           moe_w_down, final_norm_g):
    b, s, d = x.shape
    h = x.reshape(b * s, d)
    rows = lambda a: a.reshape(a.shape[0], 1, -1)
    mixer = (attn_sinks, rows(norm_mix_g), w_in.astype(BF16))
    mixer_out = (pool_w.astype(BF16), rows(pool_scale), w_out.astype(BF16), rows(norm_ffn_g))

    kvu0, hm, xnm = _meta_call(meta_tokens, 0, *mixer, *mixer_out)
    h, xn, *dense, moe_gate = _mix_call(
        h, b, 0, *mixer, kvu0, *mixer_out, BF16,
        (dense_w_gate, dense_w_up, dense_w_down, moe_w_gate[0]))
    hm = _ffn_call(xnm, hm, *dense, *_dense_table(1, N_META), N_META, False)
    kvu1, _, _ = _meta_call(hm, 1, *mixer, *mixer_out)
    h = _ffn_call(xn, h, *dense, *_dense_table(b * s // FFN_TILE, FFN_TILE), FFN_TILE, False)
    h, xn, moe_up, moe_down = _mix_call(
        h, b, 1, *mixer, kvu1, *mixer_out, F32, (moe_w_up[0], moe_w_down[0]))
    out = _moe(h, xn, moe_w_router[0], moe_gate, moe_up, moe_down, final_norm_g.reshape(1, -1))
    return out.reshape(b, s, d)
```

```python
import functools

import jax
import jax.numpy as jnp
from jax import lax
from jax.experimental import pallas as pl
from jax.experimental.pallas import tpu as pltpu

F32 = jnp.float32
BF16 = jnp.bfloat16

CHUNK = 64
N_META = 16
HEAD_DIM = 64
N_Q_HEADS = 8
D_ATTN = 512
D_KV = 128
D_POOL = 512
POOL_SIZES = (2, 4, 8, 16)
POOL_GROUP = 128
N_EXPERTS = 8
EPS = 1e-5
NEG_INF = -1e30
LOG2_E = 1.4426950408889634

LANES = 128
HALF = LANES // 2
Q_BLOCK = 2 * CHUNK
KEY_WINDOW = 4 * CHUNK
HALO = 2 * CHUNK
E_WIDTH = KEY_WINDOW + LANES
SOFTMAX_ROWS = 32
POOL_HALO = max(POOL_SIZES)
VMEM_LIMIT = 56 * 1024 * 1024

MIX_TILE = 512
MIX_PARTS = 2
FFN_TILE = 512
FFN_SUBTILE = 512
FFN_BLOCK = 1792
ROW_TILE = 512


def _rmsnorm(x, g):
    return x * lax.rsqrt(jnp.mean(x * x, axis=-1, keepdims=True) + EPS) * g


def _dot(a, b):
    return jnp.dot(a, b, preferred_element_type=F32)


def _dot_nt(a, b):
    return lax.dot_general(a, b, (((1,), (1,)), ((), ())), preferred_element_type=F32)


def _head_variants(t, fill=0.0):
    low = lax.broadcasted_iota(jnp.int32, t.shape, 1) < HALF
    t_r = pltpu.roll(t, HALF, axis=1)
    other = jnp.full_like(t, fill)
    return (jnp.where(low, t, other).astype(BF16),
            jnp.where(low, other, t_r).astype(BF16),
            jnp.where(low, t_r, other).astype(BF16),
            jnp.where(low, other, t).astype(BF16))


def _sink_softmax_pv(s_parts, v_parts, sink):
    m = sink
    for s in s_parts:
        m = jnp.maximum(m, jnp.max(s, axis=1, keepdims=True))
    den = jnp.exp(sink - m)
    o = None
    for s, v in zip(s_parts, v_parts):
        e = jnp.exp(s - m)
        den = den + jnp.sum(e, axis=1, keepdims=True)
        pv = _dot(e.astype(BF16), v)
        o = pv if o is None else o + pv
    return o / den


def _meta_tiles(km, vm):
    km_var = _head_variants(km)
    vm_var = _head_variants(vm, 1.0)
    zeros = lambda n: jnp.zeros((n, LANES), BF16)
    row = lax.broadcasted_iota(jnp.int32, (N_META, LANES), 0)
    low = lax.broadcasted_iota(jnp.int32, (N_META, LANES), 1) < HALF
    rest = LANES - 3 * N_META
    k_tiles, v_tiles = [], []
    for h in range(2):
        k_tiles.append(jnp.concatenate([km_var[2 * h], km_var[2 * h + 1], zeros(LANES - 2 * N_META)]))
        for par in range(2):
            den_half = low if par == 1 else jnp.logical_not(low)
            sink_rows = jnp.where((row == par) & den_half, 1.0, 0.0).astype(BF16)
            v_tiles.append(jnp.concatenate(
                [vm_var[2 * h] if par == 0 else zeros(N_META),
                 vm_var[2 * h + 1] if par == 1 else zeros(N_META), sink_rows, zeros(rest)]))
    return k_tiles, v_tiles


def _mix_kernel(*refs, tq, n_cast, layer):
    (sinks_ref, h_ref, gmix_ref, win_ref, kvu_ref, poolw_ref, pscale_ref, wout_ref,
     gffn_ref) = refs[:9]
    cast_in = refs[9:9 + n_cast]
    hout_ref, xn_ref = refs[9 + n_cast:11 + n_cast]
    cast_out = refs[11 + n_cast:11 + 2 * n_cast]
    q_s, kvar_s, vvar_s, uw_s, a_s, kmeta_s, vmeta_s, s_s, sm_s, e_s = refs[11 + 2 * n_cast:]
    i = pl.program_id(1)

    for src, dst in zip(cast_in, cast_out):
        dst[...] = src[...].astype(BF16)

    @pl.when(i == 0)
    def _():
        kvar_s[:, 0:HALO, :] = jnp.zeros((4, HALO, LANES), BF16)
        vvar_s[:, 0:HALO, :] = jnp.zeros((4, HALO, LANES), BF16)
        uw_s[0:POOL_HALO, :] = kvu_ref[:, 2 * D_KV:]

    tp = tq // MIX_PARTS
    for lo in range(0, tq, tp):
        xn = _rmsnorm(h_ref[lo:lo + tp, :], gmix_ref[layer:layer + 1, :]).astype(BF16)
        q_s[lo:lo + tp, :] = (
            _dot(xn, win_ref[:, 0:D_ATTN]) * (HEAD_DIM ** -0.5 * LOG2_E)).astype(BF16)
        k = _dot(xn, win_ref[:, D_ATTN:D_ATTN + D_KV])
        v = _dot(xn, win_ref[:, D_ATTN + D_KV:D_ATTN + 2 * D_KV])
        uw_s[POOL_HALO + lo:POOL_HALO + lo + tp, :] = _dot(xn, win_ref[:, D_ATTN + 2 * D_KV:])
        for idx, (kk, vv) in enumerate(zip(_head_variants(k), _head_variants(v, 1.0))):
            kvar_s[idx, HALO + lo:HALO + lo + tp, :] = kk
            vvar_s[idx, HALO + lo:HALO + lo + tp, :] = vv
    k_tiles, v_tiles = _meta_tiles(kvu_ref[:, 0:D_KV], kvu_ref[:, D_KV:2 * D_KV])
    for h in range(2):
        kmeta_s[h] = k_tiles[h]
    for hp in range(4):
        vmeta_s[hp] = v_tiles[hp]

    def attend(blk):
        r0 = blk * Q_BLOCK
        q_chunk = lax.broadcasted_iota(jnp.int32, (Q_BLOCK, KEY_WINDOW), 0) // CHUNK
        k_chunk = lax.broadcasted_iota(jnp.int32, (Q_BLOCK, KEY_WINDOW), 1) // CHUNK
        valid = (k_chunk >= q_chunk) & (k_chunk <= q_chunk + 2)
        if blk == 0:
            valid = valid & (k_chunk >= jnp.where(i == 0, 2, 0))

        for j in range(D_ATTN // LANES):
            col = q_s[pl.ds(r0, Q_BLOCK), j * LANES:(j + 1) * LANES]
            for par in range(2):
                s = _dot_nt(col, kvar_s[2 * (j // 2) + par, pl.ds(r0, KEY_WINDOW), :])
                s_s[j, :, par * KEY_WINDOW:(par + 1) * KEY_WINDOW] = jnp.where(valid, s, NEG_INF)
            sm_s[j] = _dot_nt(col, kmeta_s[j // 2])

        lane = lax.broadcasted_iota(jnp.int32, (1, LANES), 1)
        for j in range(D_ATTN // LANES):
            for par in range(2):
                own_meta = (lane >= par * N_META) & (lane < (par + 1) * N_META)
                sink = sinks_ref[layer, 2 * j + par] * LOG2_E
                other = jnp.where(lane == 2 * N_META + par, sink, NEG_INF)
                for g in range(Q_BLOCK // SOFTMAX_ROWS):
                    rows = slice(g * SOFTMAX_ROWS, (g + 1) * SOFTMAX_ROWS)
                    s = s_s[j, rows, par * KEY_WINDOW:(par + 1) * KEY_WINDOW]
                    sm = jnp.where(own_meta, sm_s[j, rows, :], other)
                    m = jnp.maximum(jnp.maximum(s[:, :LANES], s[:, LANES:]), sm)
                    m = jnp.max(m, axis=1, keepdims=True)
                    e_s[j, rows, par * E_WIDTH:par * E_WIDTH + KEY_WINDOW] = (
                        jnp.exp2(s - m).astype(BF16))
                    e_s[j, rows, par * E_WIDTH + KEY_WINDOW:(par + 1) * E_WIDTH] = (
                        jnp.exp2(sm - m).astype(BF16))

        low = lax.broadcasted_iota(jnp.int32, (Q_BLOCK, LANES), 1) < HALF
        for j in range(D_ATTN // LANES):
            o = []
            for par in range(2):
                hp = 2 * (j // 2) + par
                e0 = par * E_WIDTH
                o.append(_dot(e_s[j, :, e0:e0 + KEY_WINDOW], vvar_s[hp, pl.ds(r0, KEY_WINDOW), :])
                         + _dot(e_s[j, :, e0 + KEY_WINDOW:e0 + E_WIDTH], vmeta_s[hp]))
            num = jnp.where(low, o[0], o[1])
            den = pltpu.roll(jnp.where(low, o[1], o[0]), HALF, axis=1)
            a_s[pl.ds(r0, Q_BLOCK), j * LANES:(j + 1) * LANES] = (num / den).astype(BF16)

    for blk in range(tq // Q_BLOCK):
        attend(blk)

    for lo in range(0, tq, tp):
        pooled = []
        for g, w in enumerate(POOL_SIZES):
            cs = slice(g * POOL_GROUP, (g + 1) * POOL_GROUP)
            cur = uw_s[POOL_HALO + lo:POOL_HALO + lo + tp, cs]
            acc = cur
            for lag in range(1, w):
                acc = acc + uw_s[POOL_HALO + lo - lag:POOL_HALO + lo - lag + tp, cs]
            d = acc * (1.0 / w) - cur
            pooled.append(_dot(d.astype(BF16), poolw_ref[g]) * pscale_ref[layer:layer + 1, cs])
        p = jnp.concatenate(pooled, axis=1).astype(BF16)

        hn = (h_ref[lo:lo + tp, :] + _dot(a_s[lo:lo + tp, :], wout_ref[0:D_ATTN, :])
              + _dot(p, wout_ref[D_ATTN:, :]))
        hout_ref[lo:lo + tp, :] = hn
        xn_ref[lo:lo + tp, :] = _rmsnorm(hn, gffn_ref[layer:layer + 1, :]).astype(xn_ref.dtype)

    kvar_s[:, 0:HALO, :] = kvar_s[:, tq:tq + HALO, :]
    vvar_s[:, 0:HALO, :] = vvar_s[:, tq:tq + HALO, :]
    uw_s[0:POOL_HALO, :] = uw_s[tq:tq + POOL_HALO, :]


def _layer_spec(w, layer):
    if w.ndim == 2:
        return pl.BlockSpec(w.shape, lambda *_: (0, 0))
    return pl.BlockSpec((None,) + w.shape[1:], lambda *_: (layer,) + (0,) * (w.ndim - 1))


def _mix_call(h, batch, layer, sinks, gmix, win, kvu, poolw, pscale, wout, gffn, xn_dtype,
              to_bf16=()):
    t, d = h.shape
    tq = MIX_TILE
    nt = t // batch // tq
    assert nt * tq * batch == t
    row = pl.BlockSpec((tq, d), lambda b, i: (b * nt + i, 0))
    flat = [w.reshape(-1, w.shape[-1]) for w in to_bf16]
    slabs = [pl.BlockSpec((w.shape[0] // (batch * nt), w.shape[1]), lambda b, i: (b * nt + i, 0))
             for w in flat]
    assert all(w.shape[0] % (16 * batch * nt) == 0 for w in flat)
    outs = pl.pallas_call(
        functools.partial(_mix_kernel, tq=tq, n_cast=len(flat), layer=layer),
        grid=(batch, nt),
        in_specs=[
            pl.BlockSpec(memory_space=pltpu.SMEM), row, _layer_spec(gmix, layer),
            _layer_spec(win, layer), pl.BlockSpec(kvu.shape, lambda b, i: (0, 0)),
            _layer_spec(poolw, layer), _layer_spec(pscale, layer), _layer_spec(wout, layer),
            _layer_spec(gffn, layer),
        ] + slabs,
        out_specs=[row, row] + slabs,
        out_shape=[jax.ShapeDtypeStruct((t, d), F32), jax.ShapeDtypeStruct((t, d), xn_dtype)]
        + [jax.ShapeDtypeStruct(w.shape, BF16) for w in flat],
        scratch_shapes=[
            pltpu.VMEM((tq, D_ATTN), BF16),
            pltpu.VMEM((4, HALO + tq, LANES), BF16),
            pltpu.VMEM((4, HALO + tq, LANES), BF16),
            pltpu.VMEM((POOL_HALO + tq, D_POOL), F32),
            pltpu.VMEM((tq, D_ATTN), BF16),
            pltpu.VMEM((2, LANES, LANES), BF16),
            pltpu.VMEM((4, LANES, LANES), BF16),
            pltpu.VMEM((D_ATTN // LANES, Q_BLOCK, 2 * KEY_WINDOW), F32),
            pltpu.VMEM((D_ATTN // LANES, Q_BLOCK, LANES), F32),
            pltpu.VMEM((D_ATTN // LANES, Q_BLOCK, 2 * E_WIDTH), BF16),
        ],
        compiler_params=pltpu.CompilerParams(
            dimension_semantics=("arbitrary", "arbitrary"), vmem_limit_bytes=VMEM_LIMIT),
        name="mix",
    )(sinks, h, gmix, win, kvu, poolw, pscale, wout, gffn, *flat)
    return outs[0], outs[1], *[o.reshape(w.shape) for o, w in zip(outs[2:], to_bf16)]


def _meta_kernel(sinks_ref, h_ref, gmix_ref, win_ref, poolw_ref, pscale_ref, wout_ref, gffn_ref,
                 kvu_ref, hout_ref, xn_ref, uw_s, *, layer):
    x = h_ref[...]
    xn = _rmsnorm(x, gmix_ref[layer:layer + 1, :]).astype(BF16)
    q = (_dot(xn, win_ref[:, 0:D_ATTN]) * (HEAD_DIM ** -0.5)).astype(BF16)
    k = _dot(xn, win_ref[:, D_ATTN:D_ATTN + D_KV])
    v = _dot(xn, win_ref[:, D_ATTN + D_KV:D_ATTN + 2 * D_KV])
    u = _dot(xn, win_ref[:, D_ATTN + 2 * D_KV:])
    kvu_ref[:, 0:D_KV] = k
    kvu_ref[:, D_KV:2 * D_KV] = v
    kvu_ref[:, 2 * D_KV:] = u

    k_var = _head_variants(k)
    v_var = _head_variants(v)
    cols = []
    for j in range(D_ATTN // LANES):
        col = q[:, j * LANES:(j + 1) * LANES]
        acc = None
        for par in range(2):
            var = 2 * (j // 2) + par
            o = _sink_softmax_pv((_dot_nt(col, k_var[var]),), (v_var[var],),
                                 sinks_ref[layer, 2 * j + par])
            acc = o if acc is None else acc + o
        cols.append(acc)
    a = jnp.concatenate(cols, axis=1).astype(BF16)

    uw_s[0:POOL_HALO, :] = jnp.zeros((POOL_HALO, D_POOL), F32)
    uw_s[POOL_HALO:, :] = u
    pos = lax.broadcasted_iota(jnp.int32, (N_META, 1), 0)
    pooled = []
    for g, w in enumerate(POOL_SIZES):
        cs = slice(g * POOL_GROUP, (g + 1) * POOL_GROUP)
        cur = uw_s[POOL_HALO:, cs]
        acc = cur
        for lag in range(1, w):
            acc = acc + uw_s[POOL_HALO - lag:POOL_HALO - lag + N_META, cs]
        count = jnp.minimum(pos + 1, w).astype(F32)
        d = acc / count - cur
        pooled.append(_dot(d.astype(BF16), poolw_ref[g]) * pscale_ref[layer:layer + 1, cs])
    p = jnp.concatenate(pooled, axis=1).astype(BF16)

    hn = x + _dot(a, wout_ref[0:D_ATTN, :]) + _dot(p, wout_ref[D_ATTN:, :])
    hout_ref[...] = hn
    xn_ref[...] = _rmsnorm(hn, gffn_ref[layer:layer + 1, :]).astype(xn_ref.dtype)


def _meta_call(h, layer, sinks, gmix, win, poolw, pscale, wout, gffn):
    n, d = h.shape
    whole = lambda shape: pl.BlockSpec(shape, lambda i: (0,) * len(shape))
    params = (gmix, win, poolw, pscale, wout, gffn)
    out_shapes = ((n, 2 * D_KV + D_POOL), (n, d), (n, d))
    return pl.pallas_call(
        functools.partial(_meta_kernel, layer=layer),
        grid=(1,),
        in_specs=[pl.BlockSpec(memory_space=pltpu.SMEM), whole(h.shape)]
        + [_layer_spec(w, layer) for w in params],
        out_specs=[whole(s) for s in out_shapes],
        out_shape=[jax.ShapeDtypeStruct(s, dt) for s, dt in zip(out_shapes, (F32, F32, BF16))],
        scratch_shapes=[pltpu.VMEM((POOL_HALO + N_META, D_POOL), F32)],
        compiler_params=pltpu.CompilerParams(
            dimension_semantics=("arbitrary",), vmem_limit_bytes=VMEM_LIMIT),
        name="meta_mix",
    )(sinks, h, *params)


def _ffn_kernel(tab_ref, na_ref, x_ref, *refs, masked):
    res_ref = refs[0] if len(refs) == 5 else None
    wg_ref, wu_ref, wd_ref, out_ref = refs[-4:]
    i, j = pl.program_id(0), pl.program_id(1)

    @pl.when(i < na_ref[0])
    def _():
        @pl.when(j == 0)
        def _():
            out_ref[...] = jnp.zeros_like(out_ref) if res_ref is None else res_ref[...]

        wg, wu, wd = (w[...].astype(BF16) for w in (wg_ref, wu_ref, wd_ref))
        valid = tab_ref[2, i]
        tm = x_ref.shape[0]
        sub = min(tm, FFN_SUBTILE)
        for r0 in range(0, tm, sub):
            def part(r0=r0):
                x = x_ref[r0:r0 + sub, :]
                if masked:
                    r = r0 + lax.broadcasted_iota(jnp.int32, (sub, 1), 0)
                    x = jnp.where(r < valid, x, 0.0)
                x = x.astype(BF16)
                gate = _dot(x, wg)
                up = _dot(x, wu)
                mid = (gate * jax.nn.sigmoid(gate) * up).astype(BF16)
                out_ref[r0:r0 + sub, :] += _dot(mid, wd)

            if masked and r0 > 0:
                pl.when(valid > r0)(part)
            else:
                part()


def _ffn_call(x, res, wg, wu, wd, table, n_active, tm, masked):
    rows, d = x.shape
    ff = wg.shape[-1]
    n_grid = table.shape[1]
    nj = ff // FFN_BLOCK
    assert rows % tm == 0 and nj * FFN_BLOCK == ff

    def live(i, na):
        return jnp.maximum(jnp.minimum(i, na[0] - 1), 0)

    def ff_block(i, j, na):
        return jnp.where(i < na[0], j, nj - 1)

    row = pl.BlockSpec((tm, d), lambda i, j, tab, na: (tab[1, live(i, na)], 0))
    col_w = pl.BlockSpec((None, d, FFN_BLOCK),
                         lambda i, j, tab, na: (tab[0, live(i, na)], 0, ff_block(i, j, na)))
    row_w = pl.BlockSpec((None, FFN_BLOCK, d),
                         lambda i, j, tab, na: (tab[0, live(i, na)], ff_block(i, j, na), 0))
    acts = (x,) if res is None else (x, res)
    return pl.pallas_call(
        functools.partial(_ffn_kernel, masked=masked),
        grid_spec=pltpu.PrefetchScalarGridSpec(
            num_scalar_prefetch=2, grid=(n_grid, nj),
            in_specs=[row] * len(acts) + [col_w, col_w, row_w], out_specs=row),
        out_shape=jax.ShapeDtypeStruct((rows, d), F32),
        compiler_params=pltpu.CompilerParams(
            dimension_semantics=("arbitrary", "arbitrary"), vmem_limit_bytes=VMEM_LIMIT),
        name="ffn",
    )(table, n_active, *acts, wg, wu, wd)


def _dense_table(n_tiles, tm):
    ids = jnp.arange(n_tiles, dtype=jnp.int32)
    return (jnp.stack([jnp.zeros_like(ids), ids, jnp.full_like(ids, tm)]),
            jnp.full((1,), n_tiles, jnp.int32))


def _split_bf16(x):
    hi = x.astype(BF16)
    return hi, (x - hi.astype(F32)).astype(BF16)


def _for_each_routed_row(n_rows, pos_ref, fn):
    chunks = n_rows // LANES
    for r in range(n_rows):
        for kk in range(2):
            fn(r, kk, pos_ref[kk * chunks + r // LANES, r % LANES])


def _route_dispatch_kernel(x_ref, wr_ref, tri_ref, xs_ref, pos_ref, gate_ref, counts_ref,
                           run_s, pos_sm, xbuf, sem_pos, sem_rows, *, tr, cap, n):
    i = pl.program_id(0)

    @pl.when(i == 0)
    def _():
        run_s[...] = jnp.zeros_like(run_s)

    x_hi, x_lo = _split_bf16(x_ref[...])
    w_hi, w_lo = _split_bf16(wr_ref[...])
    logits = _dot_nt(w_hi, x_hi) + (_dot_nt(w_lo, x_hi) + _dot_nt(w_hi, x_lo))

    expert = lax.broadcasted_iota(jnp.int32, logits.shape, 0).astype(F32)
    none = float(N_EXPERTS)
    m1 = jnp.max(logits, axis=0, keepdims=True)
    e1 = jnp.min(jnp.where(logits == m1, expert, none), axis=0, keepdims=True)
    rest = jnp.where(expert == e1, -jnp.inf, logits)
    m2 = jnp.max(rest, axis=0, keepdims=True)
    e2 = jnp.min(jnp.where(rest == m2, expert, none), axis=0, keepdims=True)
    t = jnp.exp(m2 - m1)
    gate_ref[...] = jnp.concatenate(
        [1.0 / (1.0 + t), t / (1.0 + t), jnp.zeros((6, tr), F32)], axis=0)

    chosen = ((expert == e1) | (expert == e2)).astype(F32)
    before = _dot(chosen.astype(BF16), tri_ref[...]) + run_s[:, 0:1]
    pieces = []
    for e in (e1, e2):
        rank = jnp.sum(jnp.where(expert == e, before, 0.0), axis=0, keepdims=True)
        pos = (e * float(cap) + rank).astype(jnp.int32)
        pieces += [pos[:, c * LANES:(c + 1) * LANES] for c in range(tr // LANES)]
    pos_ref[0] = jnp.concatenate(pieces, axis=0)
    run_s[...] += jnp.sum(chosen, axis=1, keepdims=True)
    counts_ref[...] = run_s[...].astype(jnp.int32)

    to_smem = pltpu.make_async_copy(pos_ref, pos_sm, sem_pos)
    to_smem.start()

    def wait_rows(s):
        for _ in range(2):
            pltpu.make_async_copy(xbuf.at[s], xs_ref.at[pl.ds(0, tr)], sem_rows.at[s]).wait()

    slot = i % 2

    @pl.when(i >= 2)
    def _():
        wait_rows(slot)

    xbuf[slot] = x_ref[...]
    to_smem.wait()
    for s in range(2):
        @pl.when(slot == s)
        def _(s=s):
            _for_each_routed_row(tr, pos_sm.at[0], lambda r, kk, p: pltpu.make_async_copy(
                xbuf.at[s, pl.ds(r, 1)], xs_ref.at[pl.ds(p, 1)], sem_rows.at[s]).start(kk))

    @pl.when(i == n - 1)
    def _():
        wait_rows(slot)
        if n > 1:
            wait_rows(1 - slot)


def _route_dispatch_call(xn, w_router):
    t, d = xn.shape
    tr = ROW_TILE
    n = t // tr
    tri = (lax.broadcasted_iota(jnp.int32, (tr, tr), 0)
           < lax.broadcasted_iota(jnp.int32, (tr, tr), 1)).astype(BF16)
    return pl.pallas_call(
        functools.partial(_route_dispatch_kernel, tr=tr, cap=t, n=n),
        grid=(n,),
        in_specs=[pl.BlockSpec((tr, d), lambda i: (i, 0)),
                  pl.BlockSpec((N_EXPERTS, d), lambda i: (0, 0)),
                  pl.BlockSpec((tr, tr), lambda i: (0, 0))],
        out_specs=[pl.BlockSpec(memory_space=pl.ANY),
                   pl.BlockSpec((1, 2 * tr // LANES, LANES), lambda i: (i, 0, 0)),
                   pl.BlockSpec((None, 8, tr), lambda i: (i, 0, 0)),
                   pl.BlockSpec((N_EXPERTS, LANES), lambda i: (0, 0))],
        out_shape=[jax.ShapeDtypeStruct((N_EXPERTS * t, d), F32),
                   jax.ShapeDtypeStruct((n, 2 * tr // LANES, LANES), jnp.int32),
                   jax.ShapeDtypeStruct((n, 8, tr), F32),
                   jax.ShapeDtypeStruct((N_EXPERTS, LANES), jnp.int32)],
        scratch_shapes=[pltpu.VMEM((N_EXPERTS, LANES), F32),
                        pltpu.SMEM((1, 2 * tr // LANES, LANES), jnp.int32),
                        pltpu.VMEM((2, tr, d), F32),
                        pltpu.SemaphoreType.DMA, pltpu.SemaphoreType.DMA((2,))],
        compiler_params=pltpu.CompilerParams(
            dimension_semantics=("arbitrary",), vmem_limit_bytes=VMEM_LIMIT),
        name="route_dispatch",
    )(xn, w_router.T, tri)


def _combine_kernel(pos_ref, pos_next_ref, h_ref, gate_ref, gfin_ref, y_ref, out_ref, buf, sem,
                    *, tc, n):
    i = pl.program_id(0)
    slot = i % 2

    def issue(p_ref, s):
        _for_each_routed_row(tc, p_ref, lambda r, kk, p: pltpu.make_async_copy(
            y_ref.at[pl.ds(p, 1)], buf.at[s, kk, pl.ds(r, 1)], sem.at[s]).start(kk))

    @pl.when(i == 0)
    def _():
        issue(pos_ref, 0)

    for s in range(2):
        @pl.when((i + 1 < n) & (slot == 1 - s))
        def _(s=s):
            issue(pos_next_ref, s)

    for kk in range(2):
        pltpu.make_async_copy(y_ref.at[pl.ds(0, tc)], buf.at[slot, kk], sem.at[slot]).wait()
    gates = gate_ref[...].T
    hn = h_ref[...] + (gates[:, 0:1] * buf[slot, 0] + gates[:, 1:2] * buf[slot, 1])
    out_ref[...] = _rmsnorm(hn, gfin_ref[...])


def _combine_call(h, y, pos, gates, gfin):
    t, d = h.shape
    tc = ROW_TILE
    n = t // tc
    smem_pos = lambda index_map: pl.BlockSpec((None, 2 * tc // LANES, LANES), index_map,
                                              memory_space=pltpu.SMEM)
    rows = lambda width: pl.BlockSpec((tc, width), lambda i: (i, 0))
    return pl.pallas_call(
        functools.partial(_combine_kernel, tc=tc, n=n),
        grid=(n,),
        in_specs=[smem_pos(lambda i: (i, 0, 0)),
                  smem_pos(lambda i: (jnp.minimum(i + 1, n - 1), 0, 0)),
                  rows(d), pl.BlockSpec((None, 8, tc), lambda i: (i, 0, 0)),
                  pl.BlockSpec((1, d), lambda i: (0, 0)),
                  pl.BlockSpec(memory_space=pl.ANY)],
        out_specs=rows(d),
        out_shape=jax.ShapeDtypeStruct((t, d), F32),
        scratch_shapes=[pltpu.VMEM((2, 2, tc, d), F32), pltpu.SemaphoreType.DMA((2,))],
        compiler_params=pltpu.CompilerParams(
            dimension_semantics=("arbitrary",), vmem_limit_bytes=VMEM_LIMIT),
        name="combine",
    )(pos, pos, h, gates, gfin, y)


def _moe(h, xn, w_router, wg, wu, wd, gfin):
    t, d = h.shape
    tm = FFN_TILE
    xs, pos, gates, counts = _route_dispatch_call(xn, w_router)
    counts = counts[:, 0]

    n_grid = 2 * t // tm + N_EXPERTS
    tiles_of = (counts + tm - 1) // tm
    tile_end = jnp.cumsum(tiles_of)
    ids = jnp.arange(n_grid, dtype=jnp.int32)[:, None]
    owner = (ids >= (tile_end - tiles_of)[None, :]) & (ids < tile_end[None, :])
    pick = lambda per_expert: jnp.sum(jnp.where(owner, per_expert[None, :], 0), axis=1)
    expert = pick(jnp.arange(N_EXPERTS, dtype=jnp.int32))
    local = ids[:, 0] - pick(tile_end - tiles_of)
    table = jnp.stack([expert, expert * (t // tm) + local,
                       jnp.clip(pick(counts) - local * tm, 0, tm)]).astype(jnp.int32)
    y = _ffn_call(xs, None, wg, wu, wd, table, tile_end[-1:].astype(jnp.int32), tm, True)
    return _combine_call(h, y, pos, gates, gfin)


def kernel(x, meta_tokens, norm_mix_g, w_in, attn_sinks, pool_w, pool_scale, w_out, norm_ffn_g,
           dense_w_gate, dense_w_up, dense_w_down, moe_w_router, moe_w_gate, moe_w_up,
           moe_w_down, final_norm_g):
    b, s, d = x.shape
    h = x.reshape(b * s, d)
    mixer = (attn_sinks, norm_mix_g, w_in.astype(BF16))
    mixer_out = (pool_w.astype(BF16), pool_scale, w_out.astype(BF16), norm_ffn_g)

    kvu0, hm, xnm = _meta_call(meta_tokens, 0, *mixer, *mixer_out)
    h, xn, *dense, moe_gate = _mix_call(
        h, b, 0, *mixer, kvu0, *mixer_out, BF16,
        (dense_w_gate, dense_w_up, dense_w_down, moe_w_gate[0]))
    hm = _ffn_call(xnm, hm, *dense, *_dense_table(1, N_META), N_META, False)
    kvu1, _, _ = _meta_call(hm, 1, *mixer, *mixer_out)
    h = _ffn_call(xn, h, *dense, *_dense_table(b * s // FFN_TILE, FFN_TILE), FFN_TILE, False)
    h, xn, moe_up, moe_down = _mix_call(
        h, b, 1, *mixer, kvu1, *mixer_out, F32, (moe_w_up[0], moe_w_down[0]))
    out = _moe(h, xn, moe_w_router[0], moe_gate, moe_up, moe_down, final_norm_g.reshape(1, -1))
    return out.reshape(b, s, d)
```

```python
import functools

import jax
import jax.numpy as jnp
from jax import lax
from jax.experimental import pallas as pl
from jax.experimental.pallas import tpu as pltpu

F32 = jnp.float32
BF16 = jnp.bfloat16

CHUNK = 64
N_META = 16
HEAD_DIM = 64
N_Q_HEADS = 8
D_ATTN = 512
D_KV = 128
D_POOL = 512
POOL_SIZES = (2, 4, 8, 16)
POOL_GROUP = 128
N_EXPERTS = 8
EPS = 1e-5
NEG_INF = -1e30
LOG2_E = 1.4426950408889634

LANES = 128
HALF = LANES // 2
Q_BLOCK = 2 * CHUNK
KEY_WINDOW = 4 * CHUNK
HALO = 2 * CHUNK
E_WIDTH = KEY_WINDOW + LANES
SOFTMAX_ROWS = 32
POOL_HALO = max(POOL_SIZES)
VMEM_LIMIT = 56 * 1024 * 1024

MIX_TILE = 512
MIX_PARTS = 2
FFN_TILE = 512
FFN_SUBTILE = 512
FFN_BLOCK = 1792
ROW_TILE = 512
ROW_SLOTS = 3


def _rmsnorm(x, g):
    return x * lax.rsqrt(jnp.mean(x * x, axis=-1, keepdims=True) + EPS) * g


def _dot(a, b):
    return jnp.dot(a, b, preferred_element_type=F32)


def _dot_nt(a, b):
    return lax.dot_general(a, b, (((1,), (1,)), ((), ())), preferred_element_type=F32)


def _head_variants(t, fill=0.0):
    low = lax.broadcasted_iota(jnp.int32, t.shape, 1) < HALF
    t_r = pltpu.roll(t, HALF, axis=1)
    other = jnp.full_like(t, fill)
    return (jnp.where(low, t, other).astype(BF16),
            jnp.where(low, other, t_r).astype(BF16),
            jnp.where(low, t_r, other).astype(BF16),
            jnp.where(low, other, t).astype(BF16))


def _sink_softmax_pv(s_parts, v_parts, sink):
    m = sink
    for s in s_parts:
        m = jnp.maximum(m, jnp.max(s, axis=1, keepdims=True))
    den = jnp.exp(sink - m)
    o = None
    for s, v in zip(s_parts, v_parts):
        e = jnp.exp(s - m)
        den = den + jnp.sum(e, axis=1, keepdims=True)
        pv = _dot(e.astype(BF16), v)
        o = pv if o is None else o + pv
    return o / den


def _meta_tiles(km, vm):
    km_var = _head_variants(km)
    vm_var = _head_variants(vm, 1.0)
    zeros = lambda n: jnp.zeros((n, LANES), BF16)
    row = lax.broadcasted_iota(jnp.int32, (N_META, LANES), 0)
    low = lax.broadcasted_iota(jnp.int32, (N_META, LANES), 1) < HALF
    rest = LANES - 3 * N_META
    k_tiles, v_tiles = [], []
    for h in range(2):
        k_tiles.append(jnp.concatenate([km_var[2 * h], km_var[2 * h + 1], zeros(LANES - 2 * N_META)]))
        for par in range(2):
            den_half = low if par == 1 else jnp.logical_not(low)
            sink_rows = jnp.where((row == par) & den_half, 1.0, 0.0).astype(BF16)
            v_tiles.append(jnp.concatenate(
                [vm_var[2 * h] if par == 0 else zeros(N_META),
                 vm_var[2 * h + 1] if par == 1 else zeros(N_META), sink_rows, zeros(rest)]))
    return k_tiles, v_tiles


def _mix_kernel(*refs, tq, n_cast, layer):
    (sinks_ref, h_ref, gmix_ref, win_ref, kvu_ref, poolw_ref, pscale_ref, wout_ref,
     gffn_ref) = refs[:9]
    cast_in = refs[9:9 + n_cast]
    hout_ref, xn_ref = refs[9 + n_cast:11 + n_cast]
    cast_out = refs[11 + n_cast:11 + 2 * n_cast]
    q_s, kvar_s, vvar_s, uw_s, a_s, kmeta_s, vmeta_s, s_s, sm_s, e_s = refs[11 + 2 * n_cast:]
    i = pl.program_id(1)

    for src, dst in zip(cast_in, cast_out):
        dst[...] = src[...].astype(BF16)

    @pl.when(i == 0)
    def _():
        kvar_s[:, 0:HALO, :] = jnp.zeros((4, HALO, LANES), BF16)
        vvar_s[:, 0:HALO, :] = jnp.zeros((4, HALO, LANES), BF16)
        uw_s[0:POOL_HALO, :] = kvu_ref[:, 2 * D_KV:]

    tp = tq // MIX_PARTS
    for lo in range(0, tq, tp):
        xn = _rmsnorm(h_ref[lo:lo + tp, :], gmix_ref[layer:layer + 1, :]).astype(BF16)
        q_s[lo:lo + tp, :] = (
            _dot(xn, win_ref[:, 0:D_ATTN]) * (HEAD_DIM ** -0.5 * LOG2_E)).astype(BF16)
        k = _dot(xn, win_ref[:, D_ATTN:D_ATTN + D_KV])
        v = _dot(xn, win_ref[:, D_ATTN + D_KV:D_ATTN + 2 * D_KV])
        uw_s[POOL_HALO + lo:POOL_HALO + lo + tp, :] = _dot(xn, win_ref[:, D_ATTN + 2 * D_KV:])
        for idx, (kk, vv) in enumerate(zip(_head_variants(k), _head_variants(v, 1.0))):
            kvar_s[idx, HALO + lo:HALO + lo + tp, :] = kk
            vvar_s[idx, HALO + lo:HALO + lo + tp, :] = vv
    k_tiles, v_tiles = _meta_tiles(kvu_ref[:, 0:D_KV], kvu_ref[:, D_KV:2 * D_KV])
    for h in range(2):
        kmeta_s[h] = k_tiles[h]
    for hp in range(4):
        vmeta_s[hp] = v_tiles[hp]

    def attend(blk):
        r0 = blk * Q_BLOCK
        q_chunk = lax.broadcasted_iota(jnp.int32, (Q_BLOCK, KEY_WINDOW), 0) // CHUNK
        k_chunk = lax.broadcasted_iota(jnp.int32, (Q_BLOCK, KEY_WINDOW), 1) // CHUNK
        valid = (k_chunk >= q_chunk) & (k_chunk <= q_chunk + 2)
        if blk == 0:
            valid = valid & (k_chunk >= jnp.where(i == 0, 2, 0))

        for j in range(D_ATTN // LANES):
            col = q_s[pl.ds(r0, Q_BLOCK), j * LANES:(j + 1) * LANES]
            for par in range(2):
                s = _dot_nt(col, kvar_s[2 * (j // 2) + par, pl.ds(r0, KEY_WINDOW), :])
                s_s[j, :, par * KEY_WINDOW:(par + 1) * KEY_WINDOW] = jnp.where(valid, s, NEG_INF)
            sm_s[j] = _dot_nt(col, kmeta_s[j // 2])

        lane = lax.broadcasted_iota(jnp.int32, (1, LANES), 1)
        for j in range(D_ATTN // LANES):
            for par in range(2):
                own_meta = (lane >= par * N_META) & (lane < (par + 1) * N_META)
                sink = sinks_ref[layer, 2 * j + par] * LOG2_E
                other = jnp.where(lane == 2 * N_META + par, sink, NEG_INF)
                for g in range(Q_BLOCK // SOFTMAX_ROWS):
                    rows = slice(g * SOFTMAX_ROWS, (g + 1) * SOFTMAX_ROWS)
                    s = s_s[j, rows, par * KEY_WINDOW:(par + 1) * KEY_WINDOW]
                    sm = jnp.where(own_meta, sm_s[j, rows, :], other)
                    m = jnp.maximum(jnp.maximum(s[:, :LANES], s[:, LANES:]), sm)
                    m = jnp.max(m, axis=1, keepdims=True)
                    e_s[j, rows, par * E_WIDTH:par * E_WIDTH + KEY_WINDOW] = (
                        jnp.exp2(s - m).astype(BF16))
                    e_s[j, rows, par * E_WIDTH + KEY_WINDOW:(par + 1) * E_WIDTH] = (
                        jnp.exp2(sm - m).astype(BF16))

        low = lax.broadcasted_iota(jnp.int32, (Q_BLOCK, LANES), 1) < HALF
        for j in range(D_ATTN // LANES):
            o = []
            for par in range(2):
                hp = 2 * (j // 2) + par
                e0 = par * E_WIDTH
                o.append(_dot(e_s[j, :, e0:e0 + KEY_WINDOW], vvar_s[hp, pl.ds(r0, KEY_WINDOW), :])
                         + _dot(e_s[j, :, e0 + KEY_WINDOW:e0 + E_WIDTH], vmeta_s[hp]))
            num = jnp.where(low, o[0], o[1])
            den = pltpu.roll(jnp.where(low, o[1], o[0]), HALF, axis=1)
            a_s[pl.ds(r0, Q_BLOCK), j * LANES:(j + 1) * LANES] = (num / den).astype(BF16)

    for blk in range(tq // Q_BLOCK):
        attend(blk)

    for lo in range(0, tq, tp):
        pooled = []
        for g, w in enumerate(POOL_SIZES):
            cs = slice(g * POOL_GROUP, (g + 1) * POOL_GROUP)
            cur = uw_s[POOL_HALO + lo:POOL_HALO + lo + tp, cs]
            acc = cur
            for lag in range(1, w):
                acc = acc + uw_s[POOL_HALO + lo - lag:POOL_HALO + lo - lag + tp, cs]
            d = acc * (1.0 / w) - cur
            pooled.append(_dot(d.astype(BF16), poolw_ref[g]) * pscale_ref[layer:layer + 1, cs])
        p = jnp.concatenate(pooled, axis=1).astype(BF16)

        hn = (h_ref[lo:lo + tp, :] + _dot(a_s[lo:lo + tp, :], wout_ref[0:D_ATTN, :])
              + _dot(p, wout_ref[D_ATTN:, :]))
        hout_ref[lo:lo + tp, :] = hn
        xn_ref[lo:lo + tp, :] = _rmsnorm(hn, gffn_ref[layer:layer + 1, :]).astype(xn_ref.dtype)

    kvar_s[:, 0:HALO, :] = kvar_s[:, tq:tq + HALO, :]
    vvar_s[:, 0:HALO, :] = vvar_s[:, tq:tq + HALO, :]
    uw_s[0:POOL_HALO, :] = uw_s[tq:tq + POOL_HALO, :]


def _layer_spec(w, layer):
    if w.ndim == 2:
        return pl.BlockSpec(w.shape, lambda *_: (0, 0))
    return pl.BlockSpec((None,) + w.shape[1:], lambda *_: (layer,) + (0,) * (w.ndim - 1))


def _mix_call(h, batch, layer, sinks, gmix, win, kvu, poolw, pscale, wout, gffn, xn_dtype,
              to_bf16=()):
    t, d = h.shape
    tq = MIX_TILE
    nt = t // batch // tq
    assert nt * tq * batch == t
    row = pl.BlockSpec((tq, d), lambda b, i: (b * nt + i, 0))
    flat = [w.reshape(-1, w.shape[-1]) for w in to_bf16]
    slabs = [pl.BlockSpec((w.shape[0] // (batch * nt), w.shape[1]), lambda b, i: (b * nt + i, 0))
             for w in flat]
    assert all(w.shape[0] % (16 * batch * nt) == 0 for w in flat)
    outs = pl.pallas_call(
        functools.partial(_mix_kernel, tq=tq, n_cast=len(flat), layer=layer),
        grid=(batch, nt),
        in_specs=[
            pl.BlockSpec(memory_space=pltpu.SMEM), row, _layer_spec(gmix, layer),
            _layer_spec(win, layer), pl.BlockSpec(kvu.shape, lambda b, i: (0, 0)),
            _layer_spec(poolw, layer), _layer_spec(pscale, layer), _layer_spec(wout, layer),
            _layer_spec(gffn, layer),
        ] + slabs,
        out_specs=[row, row] + slabs,
        out_shape=[jax.ShapeDtypeStruct((t, d), F32), jax.ShapeDtypeStruct((t, d), xn_dtype)]
        + [jax.ShapeDtypeStruct(w.shape, BF16) for w in flat],
        scratch_shapes=[
            pltpu.VMEM((tq, D_ATTN), BF16),
            pltpu.VMEM((4, HALO + tq, LANES), BF16),
            pltpu.VMEM((4, HALO + tq, LANES), BF16),
            pltpu.VMEM((POOL_HALO + tq, D_POOL), F32),
            pltpu.VMEM((tq, D_ATTN), BF16),
            pltpu.VMEM((2, LANES, LANES), BF16),
            pltpu.VMEM((4, LANES, LANES), BF16),
            pltpu.VMEM((D_ATTN // LANES, Q_BLOCK, 2 * KEY_WINDOW), F32),
            pltpu.VMEM((D_ATTN // LANES, Q_BLOCK, LANES), F32),
            pltpu.VMEM((D_ATTN // LANES, Q_BLOCK, 2 * E_WIDTH), BF16),
        ],
        compiler_params=pltpu.CompilerParams(
            dimension_semantics=("arbitrary", "arbitrary"), vmem_limit_bytes=VMEM_LIMIT),
        name="mix",
    )(sinks, h, gmix, win, kvu, poolw, pscale, wout, gffn, *flat)
    return outs[0], outs[1], *[o.reshape(w.shape) for o, w in zip(outs[2:], to_bf16)]


def _meta_kernel(sinks_ref, h_ref, gmix_ref, win_ref, poolw_ref, pscale_ref, wout_ref, gffn_ref,
                 kvu_ref, hout_ref, xn_ref, uw_s, *, layer):
    x = h_ref[...]
    xn = _rmsnorm(x, gmix_ref[layer:layer + 1, :]).astype(BF16)
    q = (_dot(xn, win_ref[:, 0:D_ATTN]) * (HEAD_DIM ** -0.5)).astype(BF16)
    k = _dot(xn, win_ref[:, D_ATTN:D_ATTN + D_KV])
    v = _dot(xn, win_ref[:, D_ATTN + D_KV:D_ATTN + 2 * D_KV])
    u = _dot(xn, win_ref[:, D_ATTN + 2 * D_KV:])
    kvu_ref[:, 0:D_KV] = k
    kvu_ref[:, D_KV:2 * D_KV] = v
    kvu_ref[:, 2 * D_KV:] = u

    k_var = _head_variants(k)
    v_var = _head_variants(v)
    cols = []
    for j in range(D_ATTN // LANES):
        col = q[:, j * LANES:(j + 1) * LANES]
        acc = None
        for par in range(2):
            var = 2 * (j // 2) + par
            o = _sink_softmax_pv((_dot_nt(col, k_var[var]),), (v_var[var],),
                                 sinks_ref[layer, 2 * j + par])
            acc = o if acc is None else acc + o
        cols.append(acc)
    a = jnp.concatenate(cols, axis=1).astype(BF16)

    uw_s[0:POOL_HALO, :] = jnp.zeros((POOL_HALO, D_POOL), F32)
    uw_s[POOL_HALO:, :] = u
    pos = lax.broadcasted_iota(jnp.int32, (N_META, 1), 0)
    pooled = []
    for g, w in enumerate(POOL_SIZES):
        cs = slice(g * POOL_GROUP, (g + 1) * POOL_GROUP)
        cur = uw_s[POOL_HALO:, cs]
        acc = cur
        for lag in range(1, w):
            acc = acc + uw_s[POOL_HALO - lag:POOL_HALO - lag + N_META, cs]
        count = jnp.minimum(pos + 1, w).astype(F32)
        d = acc / count - cur
        pooled.append(_dot(d.astype(BF16), poolw_ref[g]) * pscale_ref[layer:layer + 1, cs])
    p = jnp.concatenate(pooled, axis=1).astype(BF16)

    hn = x + _dot(a, wout_ref[0:D_ATTN, :]) + _dot(p, wout_ref[D_ATTN:, :])
    hout_ref[...] = hn
    xn_ref[...] = _rmsnorm(hn, gffn_ref[layer:layer + 1, :]).astype(xn_ref.dtype)


def _meta_call(h, layer, sinks, gmix, win, poolw, pscale, wout, gffn):
    n, d = h.shape
    whole = lambda shape: pl.BlockSpec(shape, lambda i: (0,) * len(shape))
    params = (gmix, win, poolw, pscale, wout, gffn)
    out_shapes = ((n, 2 * D_KV + D_POOL), (n, d), (n, d))
    return pl.pallas_call(
        functools.partial(_meta_kernel, layer=layer),
        grid=(1,),
        in_specs=[pl.BlockSpec(memory_space=pltpu.SMEM), whole(h.shape)]
        + [_layer_spec(w, layer) for w in params],
        out_specs=[whole(s) for s in out_shapes],
        out_shape=[jax.ShapeDtypeStruct(s, dt) for s, dt in zip(out_shapes, (F32, F32, BF16))],
        scratch_shapes=[pltpu.VMEM((POOL_HALO + N_META, D_POOL), F32)],
        compiler_params=pltpu.CompilerParams(
            dimension_semantics=("arbitrary",), vmem_limit_bytes=VMEM_LIMIT),
        name="meta_mix",
    )(sinks, h, *params)


def _ffn_kernel(tab_ref, na_ref, x_ref, *refs, masked):
    res_ref = refs[0] if len(refs) == 5 else None
    wg_ref, wu_ref, wd_ref, out_ref = refs[-4:]
    i, j = pl.program_id(0), pl.program_id(1)

    @pl.when(i < na_ref[0])
    def _():
        @pl.when(j == 0)
        def _():
            out_ref[...] = jnp.zeros_like(out_ref) if res_ref is None else res_ref[...]

        wg, wu, wd = (w[...].astype(BF16) for w in (wg_ref, wu_ref, wd_ref))
        valid = tab_ref[2, i]
        tm = x_ref.shape[0]
        sub = min(tm, FFN_SUBTILE)
        for r0 in range(0, tm, sub):
            def part(r0=r0):
                x = x_ref[r0:r0 + sub, :]
                if masked:
                    r = r0 + lax.broadcasted_iota(jnp.int32, (sub, 1), 0)
                    x = jnp.where(r < valid, x, 0.0)
                x = x.astype(BF16)
                gate = _dot(x, wg)
                up = _dot(x, wu)
                mid = (gate * jax.nn.sigmoid(gate) * up).astype(BF16)
                out_ref[r0:r0 + sub, :] += _dot(mid, wd)

            if masked and r0 > 0:
                pl.when(valid > r0)(part)
            else:
                part()


def _ffn_call(x, res, wg, wu, wd, table, n_active, tm, masked):
    rows, d = x.shape
    ff = wg.shape[-1]
    n_grid = table.shape[1]
    nj = ff // FFN_BLOCK
    assert rows % tm == 0 and nj * FFN_BLOCK == ff

    def live(i, na):
        return jnp.maximum(jnp.minimum(i, na[0] - 1), 0)

    def ff_block(i, j, na):
        return jnp.where(i < na[0], j, nj - 1)

    row = pl.BlockSpec((tm, d), lambda i, j, tab, na: (tab[1, live(i, na)], 0))
    col_w = pl.BlockSpec((None, d, FFN_BLOCK),
                         lambda i, j, tab, na: (tab[0, live(i, na)], 0, ff_block(i, j, na)))
    row_w = pl.BlockSpec((None, FFN_BLOCK, d),
                         lambda i, j, tab, na: (tab[0, live(i, na)], ff_block(i, j, na), 0))
    acts = (x,) if res is None else (x, res)
    return pl.pallas_call(
        functools.partial(_ffn_kernel, masked=masked),
        grid_spec=pltpu.PrefetchScalarGridSpec(
            num_scalar_prefetch=2, grid=(n_grid, nj),
            in_specs=[row] * len(acts) + [col_w, col_w, row_w], out_specs=row),
        out_shape=jax.ShapeDtypeStruct((rows, d), F32),
        compiler_params=pltpu.CompilerParams(
            dimension_semantics=("arbitrary", "arbitrary"), vmem_limit_bytes=VMEM_LIMIT),
        name="ffn",
    )(table, n_active, *acts, wg, wu, wd)


def _dense_table(n_tiles, tm):
    ids = jnp.arange(n_tiles, dtype=jnp.int32)
    return (jnp.stack([jnp.zeros_like(ids), ids, jnp.full_like(ids, tm)]),
            jnp.full((1,), n_tiles, jnp.int32))


def _split_bf16(x):
    hi = x.astype(BF16)
    return hi, (x - hi.astype(F32)).astype(BF16)


def _for_each_routed_row(n_rows, pos_ref, fn):
    chunks = n_rows // LANES
    for r in range(n_rows):
        for kk in range(2):
            fn(r, kk, pos_ref[kk * chunks + r // LANES, r % LANES])


def _route_dispatch_kernel(x_ref, wr_ref, tri_ref, xs_ref, pos_ref, gate_ref, counts_ref,
                           run_s, pos_sm, xbuf, sem_pos, sem_rows, *, tr, cap, n):
    i = pl.program_id(0)
    slot = lax.rem(i, ROW_SLOTS)

    def wait_rows(s):
        for _ in range(2):
            pltpu.make_async_copy(xbuf.at[s], xs_ref.at[pl.ds(0, tr)], sem_rows.at[s]).wait()

    def issue_rows(s):
        _for_each_routed_row(tr, pos_sm.at[0], lambda r, kk, p: pltpu.make_async_copy(
            xbuf.at[s, pl.ds(r, 1)], xs_ref.at[pl.ds(p, 1)], sem_rows.at[s]).start(kk))

    @pl.when(i == 0)
    def _():
        run_s[...] = jnp.zeros_like(run_s)

    @pl.when(i >= ROW_SLOTS)
    def _():
        wait_rows(slot)

    @pl.when(i == 0)
    def _():
        _route(x_ref, wr_ref, tri_ref, pos_ref, gate_ref, counts_ref, run_s, xbuf.at[slot],
               pos_sm, sem_pos, tr, cap)

    @pl.when(i > 0)
    def _():
        issue_rows(lax.rem(i + ROW_SLOTS - 1, ROW_SLOTS))
        _route(x_ref, wr_ref, tri_ref, pos_ref, gate_ref, counts_ref, run_s, xbuf.at[slot],
               pos_sm, sem_pos, tr, cap)

    @pl.when(i == n - 1)
    def _():
        issue_rows(slot)
        for back in range(min(n, ROW_SLOTS)):
            wait_rows(lax.rem(i + ROW_SLOTS - back, ROW_SLOTS))


def _route(x_ref, wr_ref, tri_ref, pos_ref, gate_ref, counts_ref, run_s, xcopy, pos_sm, sem_pos,
           tr, cap):
    xcopy[...] = x_ref[...]

    x_hi, x_lo = _split_bf16(x_ref[...])
    w_hi, w_lo = _split_bf16(wr_ref[...])
    logits = _dot_nt(w_hi, x_hi) + (_dot_nt(w_lo, x_hi) + _dot_nt(w_hi, x_lo))

    expert = lax.broadcasted_iota(jnp.int32, logits.shape, 0).astype(F32)
    none = float(N_EXPERTS)
    m1 = jnp.max(logits, axis=0, keepdims=True)
    e1 = jnp.min(jnp.where(logits == m1, expert, none), axis=0, keepdims=True)
    rest = jnp.where(expert == e1, -jnp.inf, logits)
    m2 = jnp.max(rest, axis=0, keepdims=True)
    e2 = jnp.min(jnp.where(rest == m2, expert, none), axis=0, keepdims=True)
    t = jnp.exp(m2 - m1)
    gate_ref[...] = jnp.concatenate(
        [1.0 / (1.0 + t), t / (1.0 + t), jnp.zeros((6, tr), F32)], axis=0)

    chosen = ((expert == e1) | (expert == e2)).astype(F32)
    before = _dot(chosen.astype(BF16), tri_ref[...]) + run_s[:, 0:1]
    pieces = []
    for e in (e1, e2):
        rank = jnp.sum(jnp.where(expert == e, before, 0.0), axis=0, keepdims=True)
        pos = (e * float(cap) + rank).astype(jnp.int32)
        pieces += [pos[:, c * LANES:(c + 1) * LANES] for c in range(tr // LANES)]
    pos_ref[0] = jnp.concatenate(pieces, axis=0)
    run_s[...] += jnp.sum(chosen, axis=1, keepdims=True)
    counts_ref[...] = run_s[...].astype(jnp.int32)

    to_smem = pltpu.make_async_copy(pos_ref, pos_sm, sem_pos)
    to_smem.start()
    to_smem.wait()


def _route_dispatch_call(xn, w_router):
    t, d = xn.shape
    tr = ROW_TILE
    n = t // tr
    tri = (lax.broadcasted_iota(jnp.int32, (tr, tr), 0)
           < lax.broadcasted_iota(jnp.int32, (tr, tr), 1)).astype(BF16)
    return pl.pallas_call(
        functools.partial(_route_dispatch_kernel, tr=tr, cap=t, n=n),
        grid=(n,),
        in_specs=[pl.BlockSpec((tr, d), lambda i: (i, 0)),
                  pl.BlockSpec((N_EXPERTS, d), lambda i: (0, 0)),
                  pl.BlockSpec((tr, tr), lambda i: (0, 0))],
        out_specs=[pl.BlockSpec(memory_space=pl.ANY),
                   pl.BlockSpec((1, 2 * tr // LANES, LANES), lambda i: (i, 0, 0)),
                   pl.BlockSpec((None, 8, tr), lambda i: (i, 0, 0)),
                   pl.BlockSpec((N_EXPERTS, LANES), lambda i: (0, 0))],
        out_shape=[jax.ShapeDtypeStruct((N_EXPERTS * t, d), F32),
                   jax.ShapeDtypeStruct((n, 2 * tr // LANES, LANES), jnp.int32),
                   jax.ShapeDtypeStruct((n, 8, tr), F32),
                   jax.ShapeDtypeStruct((N_EXPERTS, LANES), jnp.int32)],
        scratch_shapes=[pltpu.VMEM((N_EXPERTS, LANES), F32),
                        pltpu.SMEM((1, 2 * tr // LANES, LANES), jnp.int32),
                        pltpu.VMEM((ROW_SLOTS, tr, d), F32),
                        pltpu.SemaphoreType.DMA, pltpu.SemaphoreType.DMA((ROW_SLOTS,))],
        compiler_params=pltpu.CompilerParams(
            dimension_semantics=("arbitrary",), vmem_limit_bytes=VMEM_LIMIT),
        name="route_dispatch",
    )(xn, w_router.T, tri)


def _combine_kernel(pos_ref, pos_next_ref, h_ref, gate_ref, gfin_ref, y_ref, out_ref, buf, sem,
                    *, tc, n):
    i = pl.program_id(0)
    slot = i % 2

    def issue(p_ref, s):
        _for_each_routed_row(tc, p_ref, lambda r, kk, p: pltpu.make_async_copy(
            y_ref.at[pl.ds(p, 1)], buf.at[s, kk, pl.ds(r, 1)], sem.at[s]).start(kk))

    @pl.when(i == 0)
    def _():
        issue(pos_ref, 0)

    for s in range(2):
        @pl.when((i + 1 < n) & (slot == 1 - s))
        def _(s=s):
            issue(pos_next_ref, s)

    for kk in range(2):
        pltpu.make_async_copy(y_ref.at[pl.ds(0, tc)], buf.at[slot, kk], sem.at[slot]).wait()
    gates = gate_ref[...].T
    hn = h_ref[...] + (gates[:, 0:1] * buf[slot, 0] + gates[:, 1:2] * buf[slot, 1])
    out_ref[...] = _rmsnorm(hn, gfin_ref[...])


def _combine_call(h, y, pos, gates, gfin):
    t, d = h.shape
    tc = ROW_TILE
    n = t // tc
    smem_pos = lambda index_map: pl.BlockSpec((None, 2 * tc // LANES, LANES), index_map,
                                              memory_space=pltpu.SMEM)
    rows = lambda width: pl.BlockSpec((tc, width), lambda i: (i, 0))
    return pl.pallas_call(
        functools.partial(_combine_kernel, tc=tc, n=n),
        grid=(n,),
        in_specs=[smem_pos(lambda i: (i, 0, 0)),
                  smem_pos(lambda i: (jnp.minimum(i + 1, n - 1), 0, 0)),
                  rows(d), pl.BlockSpec((None, 8, tc), lambda i: (i, 0, 0)),
                  pl.BlockSpec((1, d), lambda i: (0, 0)),
                  pl.BlockSpec(memory_space=pl.ANY)],
        out_specs=rows(d),
        out_shape=jax.ShapeDtypeStruct((t, d), F32),
        scratch_shapes=[pltpu.VMEM((2, 2, tc, d), F32), pltpu.SemaphoreType.DMA((2,))],
        compiler_params=pltpu.CompilerParams(
            dimension_semantics=("arbitrary",), vmem_limit_bytes=VMEM_LIMIT),
        name="combine",
    )(pos, pos, h, gates, gfin, y)


def _moe(h, xn, w_router, wg, wu, wd, gfin):
    t, d = h.shape
    tm = FFN_TILE
    xs, pos, gates, counts = _route_dispatch_call(xn, w_router)
    counts = counts[:, 0]

    n_grid = 2 * t // tm + N_EXPERTS
    tiles_of = (counts + tm - 1) // tm
    tile_end = jnp.cumsum(tiles_of)
    ids = jnp.arange(n_grid, dtype=jnp.int32)[:, None]
    owner = (ids >= (tile_end - tiles_of)[None, :]) & (ids < tile_end[None, :])
    pick = lambda per_expert: jnp.sum(jnp.where(owner, per_expert[None, :], 0), axis=1)
    expert = pick(jnp.arange(N_EXPERTS, dtype=jnp.int32))
    local = ids[:, 0] - pick(tile_end - tiles_of)
    table = jnp.stack([expert, expert * (t // tm) + local,
                       jnp.clip(pick(counts) - local * tm, 0, tm)]).astype(jnp.int32)
    y = _ffn_call(xs, None, wg, wu, wd, table, tile_end[-1:].astype(jnp.int32), tm, True)
    return _combine_call(h, y, pos, gates, gfin)


def kernel(x, meta_tokens, norm_mix_g, w_in, attn_sinks, pool_w, pool_scale, w_out, norm_ffn_g,
           dense_w_gate, dense_w_up, dense_w_down, moe_w_router, moe_w_gate, moe_w_up,
           moe_w_down, final_norm_g):
    b, s, d = x.shape
    h = x.reshape(b * s, d)
    mixer = (attn_sinks, norm_mix_g, w_in.astype(BF16))
    mixer_out = (pool_w.astype(BF16), pool_scale, w_out.astype(BF16), norm_ffn_g)

    kvu0, hm, xnm = _meta_call(meta_tokens, 0, *mixer, *mixer_out)
    h, xn, *dense, moe_gate = _mix_call(
        h, b, 0, *mixer, kvu0, *mixer_out, BF16,
        (dense_w_gate, dense_w_up, dense_w_down, moe_w_gate[0]))
    hm = _ffn_call(xnm, hm, *dense, *_dense_table(1, N_META), N_META, False)
    kvu1, _, _ = _meta_call(hm, 1, *mixer, *mixer_out)
    h = _ffn_call(xn, h, *dense, *_dense_table(b * s // FFN_TILE, FFN_TILE), FFN_TILE, False)
    h, xn, moe_up, moe_down = _mix_call(
        h, b, 1, *mixer, kvu1, *mixer_out, F32, (moe_w_up[0], moe_w_down[0]))
    out = _moe(h, xn, moe_w_router[0], moe_gate, moe_up, moe_down, final_norm_g.reshape(1, -1))
    return out.reshape(b, s, d)
```

```python
import functools

import jax
import jax.numpy as jnp
from jax import lax
from jax.experimental import pallas as pl
from jax.experimental.pallas import tpu as pltpu

F32 = jnp.float32
BF16 = jnp.bfloat16

CHUNK = 64
N_META = 16
HEAD_DIM = 64
D_ATTN = 512
D_KV = 128
D_POOL = 512
POOL_SIZES = (2, 4, 8, 16)
POOL_GROUP = 128
N_EXPERTS = 8
EPS = 1e-5
NEG_INF = -1e30
LOG2_E = 1.4426950408889634

LANES = 128
HALF = LANES // 2
Q_BLOCK = 2 * CHUNK
KEY_WINDOW = 4 * CHUNK
HALO = 2 * CHUNK
E_WIDTH = KEY_WINDOW + LANES
SOFTMAX_ROWS = 32
POOL_HALO = max(POOL_SIZES)
VMEM_LIMIT = 56 * 1024 * 1024

MIX_TILE = 512
MIX_PARTS = 2
FFN_TILE = 512
FFN_SUBTILE = 512
FFN_BLOCK = 1792
ROW_TILE = 512
ROW_SLOTS = 3


def _rmsnorm(x, g):
    return x * lax.rsqrt(jnp.mean(x * x, axis=-1, keepdims=True) + EPS) * g


def _dot(a, b):
    return jnp.dot(a, b, preferred_element_type=F32)


def _dot_nt(a, b):
    return lax.dot_general(a, b, (((1,), (1,)), ((), ())), preferred_element_type=F32)


def _head_variants(t, fill=0.0):
    low = lax.broadcasted_iota(jnp.int32, t.shape, 1) < HALF
    t_r = pltpu.roll(t, HALF, axis=1)
    other = jnp.full_like(t, fill)
    return (jnp.where(low, t, other).astype(BF16),
            jnp.where(low, other, t_r).astype(BF16),
            jnp.where(low, t_r, other).astype(BF16),
            jnp.where(low, other, t).astype(BF16))


def _sink_softmax_pv(s_parts, v_parts, sink):
    m = sink
    for s in s_parts:
        m = jnp.maximum(m, jnp.max(s, axis=1, keepdims=True))
    den = jnp.exp(sink - m)
    o = None
    for s, v in zip(s_parts, v_parts):
        e = jnp.exp(s - m)
        den = den + jnp.sum(e, axis=1, keepdims=True)
        pv = _dot(e.astype(BF16), v)
        o = pv if o is None else o + pv
    return o / den


def _meta_tiles(km, vm):
    km_var = _head_variants(km)
    vm_var = _head_variants(vm, 1.0)
    zeros = lambda n: jnp.zeros((n, LANES), BF16)
    row = lax.broadcasted_iota(jnp.int32, (N_META, LANES), 0)
    low = lax.broadcasted_iota(jnp.int32, (N_META, LANES), 1) < HALF
    rest = LANES - 3 * N_META
    k_tiles, v_tiles = [], []
    for h in range(2):
        k_tiles.append(jnp.concatenate([km_var[2 * h], km_var[2 * h + 1], zeros(LANES - 2 * N_META)]))
        for par in range(2):
            den_half = low if par == 1 else jnp.logical_not(low)
            sink_rows = jnp.where((row == par) & den_half, 1.0, 0.0).astype(BF16)
            v_tiles.append(jnp.concatenate(
                [vm_var[2 * h] if par == 0 else zeros(N_META),
                 vm_var[2 * h + 1] if par == 1 else zeros(N_META), sink_rows, zeros(rest)]))
    return k_tiles, v_tiles


def _mix_kernel(*refs, tq, n_cast, layer):
    (sinks_ref, h_ref, gmix_ref, win_ref, kvu_ref, poolw_ref, pscale_ref, wout_ref,
     gffn_ref) = refs[:9]
    cast_in = refs[9:9 + n_cast]
    hout_ref, xn_ref = refs[9 + n_cast:11 + n_cast]
    cast_out = refs[11 + n_cast:11 + 2 * n_cast]
    q_s, kvar_s, vvar_s, uw_s, a_s, kmeta_s, vmeta_s, s_s, sm_s, e_s = refs[11 + 2 * n_cast:]
    i = pl.program_id(1)

    for src, dst in zip(cast_in, cast_out):
        dst[...] = src[...].astype(BF16)

    @pl.when(i == 0)
    def _():
        kvar_s[:, 0:HALO, :] = jnp.zeros((4, HALO, LANES), BF16)
        vvar_s[:, 0:HALO, :] = jnp.zeros((4, HALO, LANES), BF16)
        uw_s[0:POOL_HALO, :] = kvu_ref[:, 2 * D_KV:]

    tp = tq // MIX_PARTS
    for lo in range(0, tq, tp):
        xn = _rmsnorm(h_ref[lo:lo + tp, :], gmix_ref[layer:layer + 1, :]).astype(BF16)
        q_s[lo:lo + tp, :] = (
            _dot(xn, win_ref[:, 0:D_ATTN]) * (HEAD_DIM ** -0.5 * LOG2_E)).astype(BF16)
        k = _dot(xn, win_ref[:, D_ATTN:D_ATTN + D_KV])
        v = _dot(xn, win_ref[:, D_ATTN + D_KV:D_ATTN + 2 * D_KV])
        uw_s[POOL_HALO + lo:POOL_HALO + lo + tp, :] = _dot(xn, win_ref[:, D_ATTN + 2 * D_KV:])
        for idx, (kk, vv) in enumerate(zip(_head_variants(k), _head_variants(v, 1.0))):
            kvar_s[idx, HALO + lo:HALO + lo + tp, :] = kk
            vvar_s[idx, HALO + lo:HALO + lo + tp, :] = vv
    k_tiles, v_tiles = _meta_tiles(kvu_ref[:, 0:D_KV], kvu_ref[:, D_KV:2 * D_KV])
    for h in range(2):
        kmeta_s[h] = k_tiles[h]
    for hp in range(4):
        vmeta_s[hp] = v_tiles[hp]

    def attend(blk):
        r0 = blk * Q_BLOCK
        q_chunk = lax.broadcasted_iota(jnp.int32, (Q_BLOCK, KEY_WINDOW), 0) // CHUNK
        k_chunk = lax.broadcasted_iota(jnp.int32, (Q_BLOCK, KEY_WINDOW), 1) // CHUNK
        valid = (k_chunk >= q_chunk) & (k_chunk <= q_chunk + 2)
        if blk == 0:
            valid = valid & (k_chunk >= jnp.where(i == 0, 2, 0))

        for j in range(D_ATTN // LANES):
            col = q_s[pl.ds(r0, Q_BLOCK), j * LANES:(j + 1) * LANES]
            for par in range(2):
                s = _dot_nt(col, kvar_s[2 * (j // 2) + par, pl.ds(r0, KEY_WINDOW), :])
                s_s[j, :, par * KEY_WINDOW:(par + 1) * KEY_WINDOW] = jnp.where(valid, s, NEG_INF)
            sm_s[j] = _dot_nt(col, kmeta_s[j // 2])

        lane = lax.broadcasted_iota(jnp.int32, (1, LANES), 1)
        for j in range(D_ATTN // LANES):
            for par in range(2):
                own_meta = (lane >= par * N_META) & (lane < (par + 1) * N_META)
                sink = sinks_ref[layer, 2 * j + par] * LOG2_E
                other = jnp.where(lane == 2 * N_META + par, sink, NEG_INF)
                for g in range(Q_BLOCK // SOFTMAX_ROWS):
                    rows = slice(g * SOFTMAX_ROWS, (g + 1) * SOFTMAX_ROWS)
                    s = s_s[j, rows, par * KEY_WINDOW:(par + 1) * KEY_WINDOW]
                    sm = jnp.where(own_meta, sm_s[j, rows, :], other)
                    m = jnp.maximum(jnp.maximum(s[:, :LANES], s[:, LANES:]), sm)
                    m = jnp.max(m, axis=1, keepdims=True)
                    e_s[j, rows, par * E_WIDTH:par * E_WIDTH + KEY_WINDOW] = (
                        jnp.exp2(s - m).astype(BF16))
                    e_s[j, rows, par * E_WIDTH + KEY_WINDOW:(par + 1) * E_WIDTH] = (
                        jnp.exp2(sm - m).astype(BF16))

        low = lax.broadcasted_iota(jnp.int32, (Q_BLOCK, LANES), 1) < HALF
        for j in range(D_ATTN // LANES):
            o = []
            for par in range(2):
                hp = 2 * (j // 2) + par
                e0 = par * E_WIDTH
                o.append(_dot(e_s[j, :, e0:e0 + KEY_WINDOW], vvar_s[hp, pl.ds(r0, KEY_WINDOW), :])
                         + _dot(e_s[j, :, e0 + KEY_WINDOW:e0 + E_WIDTH], vmeta_s[hp]))
            num = jnp.where(low, o[0], o[1])
            den = pltpu.roll(jnp.where(low, o[1], o[0]), HALF, axis=1)
            a_s[pl.ds(r0, Q_BLOCK), j * LANES:(j + 1) * LANES] = (num / den).astype(BF16)

    for blk in range(tq // Q_BLOCK):
        attend(blk)

    for lo in range(0, tq, tp):
        pooled = []
        for g, w in enumerate(POOL_SIZES):
            cs = slice(g * POOL_GROUP, (g + 1) * POOL_GROUP)
            cur = uw_s[POOL_HALO + lo:POOL_HALO + lo + tp, cs]
            acc = cur
            for lag in range(1, w):
                acc = acc + uw_s[POOL_HALO + lo - lag:POOL_HALO + lo - lag + tp, cs]
            d = acc * (1.0 / w) - cur
            pooled.append(_dot(d.astype(BF16), poolw_ref[g]) * pscale_ref[layer:layer + 1, cs])
        p = jnp.concatenate(pooled, axis=1).astype(BF16)

        hn = (h_ref[lo:lo + tp, :] + _dot(a_s[lo:lo + tp, :], wout_ref[0:D_ATTN, :])
              + _dot(p, wout_ref[D_ATTN:, :]))
        hout_ref[lo:lo + tp, :] = hn
        xn_ref[lo:lo + tp, :] = _rmsnorm(hn, gffn_ref[layer:layer + 1, :]).astype(xn_ref.dtype)

    kvar_s[:, 0:HALO, :] = kvar_s[:, tq:tq + HALO, :]
    vvar_s[:, 0:HALO, :] = vvar_s[:, tq:tq + HALO, :]
    uw_s[0:POOL_HALO, :] = uw_s[tq:tq + POOL_HALO, :]


def _layer_spec(w, layer):
    if w.ndim == 2:
        return pl.BlockSpec(w.shape, lambda *_: (0, 0))
    return pl.BlockSpec((None,) + w.shape[1:], lambda *_: (layer,) + (0,) * (w.ndim - 1))


def _mix_call(h, batch, layer, sinks, gmix, win, kvu, poolw, pscale, wout, gffn, xn_dtype,
              to_bf16=()):
    t, d = h.shape
    tq = MIX_TILE
    nt = t // batch // tq
    assert nt * tq * batch == t
    row = pl.BlockSpec((tq, d), lambda b, i: (b * nt + i, 0))
    flat = [w.reshape(-1, w.shape[-1]) for w in to_bf16]
    slabs = [pl.BlockSpec((w.shape[0] // (batch * nt), w.shape[1]), lambda b, i: (b * nt + i, 0))
             for w in flat]
    assert all(w.shape[0] % (16 * batch * nt) == 0 for w in flat)
    outs = pl.pallas_call(
        functools.partial(_mix_kernel, tq=tq, n_cast=len(flat), layer=layer),
        grid=(batch, nt),
        in_specs=[
            pl.BlockSpec(memory_space=pltpu.SMEM), row, _layer_spec(gmix, layer),
            _layer_spec(win, layer), pl.BlockSpec(kvu.shape, lambda b, i: (0, 0)),
            _layer_spec(poolw, layer), _layer_spec(pscale, layer), _layer_spec(wout, layer),
            _layer_spec(gffn, layer),
        ] + slabs,
        out_specs=[row, row] + slabs,
        out_shape=[jax.ShapeDtypeStruct((t, d), F32), jax.ShapeDtypeStruct((t, d), xn_dtype)]
        + [jax.ShapeDtypeStruct(w.shape, BF16) for w in flat],
        scratch_shapes=[
            pltpu.VMEM((tq, D_ATTN), BF16),
            pltpu.VMEM((4, HALO + tq, LANES), BF16),
            pltpu.VMEM((4, HALO + tq, LANES), BF16),
            pltpu.VMEM((POOL_HALO + tq, D_POOL), F32),
            pltpu.VMEM((tq, D_ATTN), BF16),
            pltpu.VMEM((2, LANES, LANES), BF16),
            pltpu.VMEM((4, LANES, LANES), BF16),
            pltpu.VMEM((D_ATTN // LANES, Q_BLOCK, 2 * KEY_WINDOW), F32),
            pltpu.VMEM((D_ATTN // LANES, Q_BLOCK, LANES), F32),
            pltpu.VMEM((D_ATTN // LANES, Q_BLOCK, 2 * E_WIDTH), BF16),
        ],
        compiler_params=pltpu.CompilerParams(
            dimension_semantics=("arbitrary", "arbitrary"), vmem_limit_bytes=VMEM_LIMIT),
        name="mix",
    )(sinks, h, gmix, win, kvu, poolw, pscale, wout, gffn, *flat)
    return outs[0], outs[1], *[o.reshape(w.shape) for o, w in zip(outs[2:], to_bf16)]


def _meta_kernel(sinks_ref, h_ref, gmix_ref, win_ref, poolw_ref, pscale_ref, wout_ref, gffn_ref,
                 kvu_ref, hout_ref, xn_ref, uw_s, *, layer):
    x = h_ref[...]
    xn = _rmsnorm(x, gmix_ref[layer:layer + 1, :]).astype(BF16)
    q = (_dot(xn, win_ref[:, 0:D_ATTN]) * (HEAD_DIM ** -0.5)).astype(BF16)
    k = _dot(xn, win_ref[:, D_ATTN:D_ATTN + D_KV])
    v = _dot(xn, win_ref[:, D_ATTN + D_KV:D_ATTN + 2 * D_KV])
    u = _dot(xn, win_ref[:, D_ATTN + 2 * D_KV:])
    kvu_ref[:, 0:D_KV] = k
    kvu_ref[:, D_KV:2 * D_KV] = v
    kvu_ref[:, 2 * D_KV:] = u

    k_var = _head_variants(k)
    v_var = _head_variants(v)
    cols = []
    for j in range(D_ATTN // LANES):
        col = q[:, j * LANES:(j + 1) * LANES]
        acc = None
        for par in range(2):
            var = 2 * (j // 2) + par
            o = _sink_softmax_pv((_dot_nt(col, k_var[var]),), (v_var[var],),
                                 sinks_ref[layer, 2 * j + par])
            acc = o if acc is None else acc + o
        cols.append(acc)
    a = jnp.concatenate(cols, axis=1).astype(BF16)

    uw_s[0:POOL_HALO, :] = jnp.zeros((POOL_HALO, D_POOL), F32)
    uw_s[POOL_HALO:, :] = u
    pos = lax.broadcasted_iota(jnp.int32, (N_META, 1), 0)
    pooled = []
    for g, w in enumerate(POOL_SIZES):
        cs = slice(g * POOL_GROUP, (g + 1) * POOL_GROUP)
        cur = uw_s[POOL_HALO:, cs]
        acc = cur
        for lag in range(1, w):
            acc = acc + uw_s[POOL_HALO - lag:POOL_HALO - lag + N_META, cs]
        count = jnp.minimum(pos + 1, w).astype(F32)
        d = acc / count - cur
        pooled.append(_dot(d.astype(BF16), poolw_ref[g]) * pscale_ref[layer:layer + 1, cs])
    p = jnp.concatenate(pooled, axis=1).astype(BF16)

    hn = x + _dot(a, wout_ref[0:D_ATTN, :]) + _dot(p, wout_ref[D_ATTN:, :])
    hout_ref[...] = hn
    xn_ref[...] = _rmsnorm(hn, gffn_ref[layer:layer + 1, :]).astype(xn_ref.dtype)


def _meta_call(h, layer, sinks, gmix, win, poolw, pscale, wout, gffn):
    n, d = h.shape
    whole = lambda shape: pl.BlockSpec(shape, lambda i: (0,) * len(shape))
    params = (gmix, win, poolw, pscale, wout, gffn)
    out_shapes = ((n, 2 * D_KV + D_POOL), (n, d), (n, d))
    return pl.pallas_call(
        functools.partial(_meta_kernel, layer=layer),
        grid=(1,),
        in_specs=[pl.BlockSpec(memory_space=pltpu.SMEM), whole(h.shape)]
        + [_layer_spec(w, layer) for w in params],
        out_specs=[whole(s) for s in out_shapes],
        out_shape=[jax.ShapeDtypeStruct(s, dt) for s, dt in zip(out_shapes, (F32, F32, BF16))],
        scratch_shapes=[pltpu.VMEM((POOL_HALO + N_META, D_POOL), F32)],
        compiler_params=pltpu.CompilerParams(
            dimension_semantics=("arbitrary",), vmem_limit_bytes=VMEM_LIMIT),
        name="meta_mix",
    )(sinks, h, *params)


def _ffn_kernel(tab_ref, na_ref, x_ref, *refs, masked):
    res_ref = refs[0] if len(refs) == 5 else None
    wg_ref, wu_ref, wd_ref, out_ref = refs[-4:]
    i, j = pl.program_id(0), pl.program_id(1)

    @pl.when(i < na_ref[0])
    def _():
        @pl.when(j == 0)
        def _():
            out_ref[...] = jnp.zeros_like(out_ref) if res_ref is None else res_ref[...]

        wg, wu, wd = (w[...].astype(BF16) for w in (wg_ref, wu_ref, wd_ref))
        valid = tab_ref[2, i]
        tm = x_ref.shape[0]
        sub = min(tm, FFN_SUBTILE)
        for r0 in range(0, tm, sub):
            def part(r0=r0):
                x = x_ref[r0:r0 + sub, :]
                if masked:
                    r = r0 + lax.broadcasted_iota(jnp.int32, (sub, 1), 0)
                    x = jnp.where(r < valid, x, 0.0)
                x = x.astype(BF16)
                gate = _dot(x, wg)
                up = _dot(x, wu)
                mid = (gate * jax.nn.sigmoid(gate) * up).astype(BF16)
                out_ref[r0:r0 + sub, :] += _dot(mid, wd)

            if masked and r0 > 0:
                pl.when(valid > r0)(part)
            else:
                part()


def _ffn_call(x, res, wg, wu, wd, table, n_active, tm, masked):
    rows, d = x.shape
    ff = wg.shape[-1]
    n_grid = table.shape[1]
    nj = ff // FFN_BLOCK
    assert rows % tm == 0 and nj * FFN_BLOCK == ff

    def live(i, na):
        return jnp.maximum(jnp.minimum(i, na[0] - 1), 0)

    def ff_block(i, j, na):
        return jnp.where(i < na[0], j, nj - 1)

    row = pl.BlockSpec((tm, d), lambda i, j, tab, na: (tab[1, live(i, na)], 0))
    col_w = pl.BlockSpec((None, d, FFN_BLOCK),
                         lambda i, j, tab, na: (tab[0, live(i, na)], 0, ff_block(i, j, na)))
    row_w = pl.BlockSpec((None, FFN_BLOCK, d),
                         lambda i, j, tab, na: (tab[0, live(i, na)], ff_block(i, j, na), 0))
    acts = (x,) if res is None else (x, res)
    return pl.pallas_call(
        functools.partial(_ffn_kernel, masked=masked),
        grid_spec=pltpu.PrefetchScalarGridSpec(
            num_scalar_prefetch=2, grid=(n_grid, nj),
            in_specs=[row] * len(acts) + [col_w, col_w, row_w], out_specs=row),
        out_shape=jax.ShapeDtypeStruct((rows, d), F32),
        compiler_params=pltpu.CompilerParams(
            dimension_semantics=("arbitrary", "arbitrary"), vmem_limit_bytes=VMEM_LIMIT),
        name="ffn",
    )(table, n_active, *acts, wg, wu, wd)


def _dense_table(n_tiles, tm):
    ids = jnp.arange(n_tiles, dtype=jnp.int32)
    return (jnp.stack([jnp.zeros_like(ids), ids, jnp.full_like(ids, tm)]),
            jnp.full((1,), n_tiles, jnp.int32))


def _split_bf16(x):
    hi = x.astype(BF16)
    return hi, (x - hi.astype(F32)).astype(BF16)


def _for_each_routed_row(n_rows, pos_ref, fn):
    chunks = n_rows // LANES
    for r in range(n_rows):
        for kk in range(2):
            fn(r, kk, pos_ref[kk * chunks + r // LANES, r % LANES])


def _route_dispatch_kernel(x_ref, wr_ref, tri_ref, xs_ref, pos_ref, gate_ref, counts_ref,
                           run_s, pos_sm, xbuf, sem_pos, sem_rows, *, tr, cap, n):
    i = pl.program_id(0)
    slot = lax.rem(i, ROW_SLOTS)

    def wait_rows(s):
        for _ in range(2):
            pltpu.make_async_copy(xbuf.at[s], xs_ref.at[pl.ds(0, tr)], sem_rows.at[s]).wait()

    def issue_rows(s):
        _for_each_routed_row(tr, pos_sm.at[0], lambda r, kk, p: pltpu.make_async_copy(
            xbuf.at[s, pl.ds(r, 1)], xs_ref.at[pl.ds(p, 1)], sem_rows.at[s]).start(kk))

    @pl.when(i == 0)
    def _():
        run_s[...] = jnp.zeros_like(run_s)

    @pl.when(i >= ROW_SLOTS)
    def _():
        wait_rows(slot)

    @pl.when(i == 0)
    def _():
        _route(x_ref, wr_ref, tri_ref, pos_ref, gate_ref, counts_ref, run_s, xbuf.at[slot],
               pos_sm, sem_pos, tr, cap)

    for s in range(ROW_SLOTS):
        @pl.when((i > 0) & (slot == s))
        def _(s=s):
            issue_rows((s + ROW_SLOTS - 1) % ROW_SLOTS)
            _route(x_ref, wr_ref, tri_ref, pos_ref, gate_ref, counts_ref, run_s, xbuf.at[s],
                   pos_sm, sem_pos, tr, cap)

    @pl.when(i == n - 1)
    def _():
        issue_rows(slot)
        for back in range(min(n, ROW_SLOTS)):
            wait_rows(lax.rem(i + ROW_SLOTS - back, ROW_SLOTS))


def _route(x_ref, wr_ref, tri_ref, pos_ref, gate_ref, counts_ref, run_s, xcopy, pos_sm, sem_pos,
           tr, cap):
    xcopy[...] = x_ref[...]

    x_hi, x_lo = _split_bf16(x_ref[...])
    w_hi, w_lo = _split_bf16(wr_ref[...])
    logits = _dot_nt(w_hi, x_hi) + (_dot_nt(w_lo, x_hi) + _dot_nt(w_hi, x_lo))

    expert = lax.broadcasted_iota(jnp.int32, logits.shape, 0).astype(F32)
    none = float(N_EXPERTS)
    m1 = jnp.max(logits, axis=0, keepdims=True)
    e1 = jnp.min(jnp.where(logits == m1, expert, none), axis=0, keepdims=True)
    rest = jnp.where(expert == e1, -jnp.inf, logits)
    m2 = jnp.max(rest, axis=0, keepdims=True)
    e2 = jnp.min(jnp.where(rest == m2, expert, none), axis=0, keepdims=True)
    t = jnp.exp(m2 - m1)
    gate_ref[...] = jnp.concatenate(
        [1.0 / (1.0 + t), t / (1.0 + t), jnp.zeros((6, tr), F32)], axis=0)

    chosen = ((expert == e1) | (expert == e2)).astype(F32)
    before = _dot(chosen.astype(BF16), tri_ref[...]) + run_s[:, 0:1]
    pieces = []
    for e in (e1, e2):
        rank = jnp.sum(jnp.where(expert == e, before, 0.0), axis=0, keepdims=True)
        pos = (e * float(cap) + rank).astype(jnp.int32)
        pieces += [pos[:, c * LANES:(c + 1) * LANES] for c in range(tr // LANES)]
    pos_ref[0] = jnp.concatenate(pieces, axis=0)
    run_s[...] += jnp.sum(chosen, axis=1, keepdims=True)
    counts_ref[...] = run_s[...].astype(jnp.int32)

    to_smem = pltpu.make_async_copy(pos_ref, pos_sm, sem_pos)
    to_smem.start()
    to_smem.wait()


def _route_dispatch_call(xn, w_router):
    t, d = xn.shape
    tr = ROW_TILE
    n = t // tr
    tri = (lax.broadcasted_iota(jnp.int32, (tr, tr), 0)
           < lax.broadcasted_iota(jnp.int32, (tr, tr), 1)).astype(BF16)
    return pl.pallas_call(
        functools.partial(_route_dispatch_kernel, tr=tr, cap=t, n=n),
        grid=(n,),
        in_specs=[pl.BlockSpec((tr, d), lambda i: (i, 0)),
                  pl.BlockSpec((N_EXPERTS, d), lambda i: (0, 0)),
                  pl.BlockSpec((tr, tr), lambda i: (0, 0))],
        out_specs=[pl.BlockSpec(memory_space=pl.ANY),
                   pl.BlockSpec((1, 2 * tr // LANES, LANES), lambda i: (i, 0, 0)),
                   pl.BlockSpec((None, 8, tr), lambda i: (i, 0, 0)),
                   pl.BlockSpec((N_EXPERTS, LANES), lambda i: (0, 0))],
        out_shape=[jax.ShapeDtypeStruct((N_EXPERTS * t, d), F32),
                   jax.ShapeDtypeStruct((n, 2 * tr // LANES, LANES), jnp.int32),
                   jax.ShapeDtypeStruct((n, 8, tr), F32),
                   jax.ShapeDtypeStruct((N_EXPERTS, LANES), jnp.int32)],
        scratch_shapes=[pltpu.VMEM((N_EXPERTS, LANES), F32),
                        pltpu.SMEM((1, 2 * tr // LANES, LANES), jnp.int32),
                        pltpu.VMEM((ROW_SLOTS, tr, d), F32),
                        pltpu.SemaphoreType.DMA, pltpu.SemaphoreType.DMA((ROW_SLOTS,))],
        compiler_params=pltpu.CompilerParams(
            dimension_semantics=("arbitrary",), vmem_limit_bytes=VMEM_LIMIT),
        name="route_dispatch",
    )(xn, w_router.T, tri)


def _combine_kernel(pos_ref, pos_next_ref, h_ref, gate_ref, gfin_ref, y_ref, out_ref, buf, sem,
                    *, tc, n):
    i = pl.program_id(0)
    slot = i % 2

    def issue(p_ref, s):
        _for_each_routed_row(tc, p_ref, lambda r, kk, p: pltpu.make_async_copy(
            y_ref.at[pl.ds(p, 1)], buf.at[s, kk, pl.ds(r, 1)], sem.at[s]).start(kk))

    @pl.when(i == 0)
    def _():
        issue(pos_ref, 0)

    for s in range(2):
        @pl.when((i + 1 < n) & (slot == 1 - s))
        def _(s=s):
            issue(pos_next_ref, s)

    for kk in range(2):
        pltpu.make_async_copy(y_ref.at[pl.ds(0, tc)], buf.at[slot, kk], sem.at[slot]).wait()
    gates = gate_ref[...].T
    hn = h_ref[...] + (gates[:, 0:1] * buf[slot, 0] + gates[:, 1:2] * buf[slot, 1])
    out_ref[...] = _rmsnorm(hn, gfin_ref[...])


def _combine_call(h, y, pos, gates, gfin):
    t, d = h.shape
    tc = ROW_TILE
    n = t // tc
    smem_pos = lambda index_map: pl.BlockSpec((None, 2 * tc // LANES, LANES), index_map,
                                              memory_space=pltpu.SMEM)
    rows = lambda width: pl.BlockSpec((tc, width), lambda i: (i, 0))
    return pl.pallas_call(
        functools.partial(_combine_kernel, tc=tc, n=n),
        grid=(n,),
        in_specs=[smem_pos(lambda i: (i, 0, 0)),
                  smem_pos(lambda i: (jnp.minimum(i + 1, n - 1), 0, 0)),
                  rows(d), pl.BlockSpec((None, 8, tc), lambda i: (i, 0, 0)),
                  pl.BlockSpec((1, d), lambda i: (0, 0)),
                  pl.BlockSpec(memory_space=pl.ANY)],
        out_specs=rows(d),
        out_shape=jax.ShapeDtypeStruct((t, d), F32),
        scratch_shapes=[pltpu.VMEM((2, 2, tc, d), F32), pltpu.SemaphoreType.DMA((2,))],
        compiler_params=pltpu.CompilerParams(
            dimension_semantics=("arbitrary",), vmem_limit_bytes=VMEM_LIMIT),
        name="combine",
    )(pos, pos, h, gates, gfin, y)


def _moe(h, xn, w_router, wg, wu, wd, gfin):
    t, d = h.shape
    tm = FFN_TILE
    xs, pos, gates, counts = _route_dispatch_call(xn, w_router)
    counts = counts[:, 0]

    n_grid = 2 * t // tm + N_EXPERTS
    tiles_of = (counts + tm - 1) // tm
    tile_end = jnp.cumsum(tiles_of)
    ids = jnp.arange(n_grid, dtype=jnp.int32)[:, None]
    owner = (ids >= (tile_end - tiles_of)[None, :]) & (ids < tile_end[None, :])
    pick = lambda per_expert: jnp.sum(jnp.where(owner, per_expert[None, :], 0), axis=1)
    expert = pick(jnp.arange(N_EXPERTS, dtype=jnp.int32))
    local = ids[:, 0] - pick(tile_end - tiles_of)
    table = jnp.stack([expert, expert * (t // tm) + local,
                       jnp.clip(pick(counts) - local * tm, 0, tm)]).astype(jnp.int32)
    y = _ffn_call(xs, None, wg, wu, wd, table, tile_end[-1:].astype(jnp.int32), tm, True)
    return _combine_call(h, y, pos, gates, gfin)


def kernel(x, meta_tokens, norm_mix_g, w_in, attn_sinks, pool_w, pool_scale, w_out, norm_ffn_g,
           dense_w_gate, dense_w_up, dense_w_down, moe_w_router, moe_w_gate, moe_w_up,
           moe_w_down, final_norm_g):
    b, s, d = x.shape
    h = x.reshape(b * s, d)
    mixer = (attn_sinks, norm_mix_g, w_in.astype(BF16))
    mixer_out = (pool_w.astype(BF16), pool_scale, w_out.astype(BF16), norm_ffn_g)

    kvu0, hm, xnm = _meta_call(meta_tokens, 0, *mixer, *mixer_out)
    h, xn, *dense, moe_gate = _mix_call(
        h, b, 0, *mixer, kvu0, *mixer_out, BF16,
        (dense_w_gate, dense_w_up, dense_w_down, moe_w_gate[0]))
    hm = _ffn_call(xnm, hm, *dense, *_dense_table(1, N_META), N_META, False)
    kvu1, _, _ = _meta_call(hm, 1, *mixer, *mixer_out)
    h = _ffn_call(xn, h, *dense, *_dense_table(b * s // FFN_TILE, FFN_TILE), FFN_TILE, False)
    h, xn, moe_up, moe_down = _mix_call(
        h, b, 1, *mixer, kvu1, *mixer_out, F32, (moe_w_up[0], moe_w_down[0]))
    out = _moe(h, xn, moe_w_router[0], moe_gate, moe_up, moe_down, final_norm_g.reshape(1, -1))
    return out.reshape(b, s, d)
```

```python
import functools

import jax
import jax.numpy as jnp
from jax import lax
from jax.experimental import pallas as pl
from jax.experimental.pallas import tpu as pltpu

F32 = jnp.float32
BF16 = jnp.bfloat16

CHUNK = 64
N_META = 16
HEAD_DIM = 64
D_ATTN = 512
D_KV = 128
D_POOL = 512
POOL_SIZES = (2, 4, 8, 16)
POOL_GROUP = 128
N_EXPERTS = 8
EPS = 1e-5
NEG_INF = -1e30
LOG2_E = 1.4426950408889634

LANES = 128
HALF = LANES // 2
Q_BLOCK = 2 * CHUNK
KEY_WINDOW = 4 * CHUNK
HALO = 2 * CHUNK
E_WIDTH = KEY_WINDOW + LANES
SOFTMAX_ROWS = 32
POOL_HALO = max(POOL_SIZES)
VMEM_LIMIT = 56 * 1024 * 1024

MIX_TILE = 512
MIX_PARTS = 2
FFN_TILE = 512
FFN_SUBTILE = 512
FFN_BLOCK = 1792
ROW_TILE = 512
ROW_SLOTS = 3


def _rmsnorm(x, g):
    return x * lax.rsqrt(jnp.mean(x * x, axis=-1, keepdims=True) + EPS) * g


def _dot(a, b):
    return jnp.dot(a, b, preferred_element_type=F32)


def _dot_nt(a, b):
    return lax.dot_general(a, b, (((1,), (1,)), ((), ())), preferred_element_type=F32)


def _head_variants(t, fill=0.0):
    low = lax.broadcasted_iota(jnp.int32, t.shape, 1) < HALF
    t_r = pltpu.roll(t, HALF, axis=1)
    other = jnp.full_like(t, fill)
    return (jnp.where(low, t, other).astype(BF16),
            jnp.where(low, other, t_r).astype(BF16),
            jnp.where(low, t_r, other).astype(BF16),
            jnp.where(low, other, t).astype(BF16))


def _sink_softmax_pv(s_parts, v_parts, sink):
    m = sink
    for s in s_parts:
        m = jnp.maximum(m, jnp.max(s, axis=1, keepdims=True))
    den = jnp.exp(sink - m)
    o = None
    for s, v in zip(s_parts, v_parts):
        e = jnp.exp(s - m)
        den = den + jnp.sum(e, axis=1, keepdims=True)
        pv = _dot(e.astype(BF16), v)
        o = pv if o is None else o + pv
    return o / den


def _meta_tiles(km, vm):
    km_var = _head_variants(km)
    vm_var = _head_variants(vm, 1.0)
    zeros = lambda n: jnp.zeros((n, LANES), BF16)
    row = lax.broadcasted_iota(jnp.int32, (N_META, LANES), 0)
    low = lax.broadcasted_iota(jnp.int32, (N_META, LANES), 1) < HALF
    rest = LANES - 3 * N_META
    k_tiles, v_tiles = [], []
    for h in range(2):
        k_tiles.append(jnp.concatenate([km_var[2 * h], km_var[2 * h + 1], zeros(LANES - 2 * N_META)]))
        for par in range(2):
            den_half = low if par == 1 else jnp.logical_not(low)
            sink_rows = jnp.where((row == par) & den_half, 1.0, 0.0).astype(BF16)
            v_tiles.append(jnp.concatenate(
                [vm_var[2 * h] if par == 0 else zeros(N_META),
                 vm_var[2 * h + 1] if par == 1 else zeros(N_META), sink_rows, zeros(rest)]))
    return k_tiles, v_tiles


def _mix_kernel(*refs, tq, n_cast, layer):
    (sinks_ref, h_ref, gmix_ref, win_ref, kvu_ref, poolw_ref, pscale_ref, wout_ref,
     gffn_ref) = refs[:9]
    cast_in = refs[9:9 + n_cast]
    hout_ref, xn_ref = refs[9 + n_cast:11 + n_cast]
    cast_out = refs[11 + n_cast:11 + 2 * n_cast]
    q_s, kvar_s, vvar_s, uw_s, a_s, kmeta_s, vmeta_s, s_s, sm_s, e_s = refs[11 + 2 * n_cast:]
    i = pl.program_id(1)

    for src, dst in zip(cast_in, cast_out):
        dst[...] = src[...].astype(BF16)

    @pl.when(i == 0)
    def _():
        kvar_s[:, 0:HALO, :] = jnp.zeros((4, HALO, LANES), BF16)
        vvar_s[:, 0:HALO, :] = jnp.zeros((4, HALO, LANES), BF16)
        uw_s[0:POOL_HALO, :] = kvu_ref[:, 2 * D_KV:]

    tp = tq // MIX_PARTS
    for lo in range(0, tq, tp):
        xn = _rmsnorm(h_ref[lo:lo + tp, :], gmix_ref[layer:layer + 1, :]).astype(BF16)
        q_s[lo:lo + tp, :] = (
            _dot(xn, win_ref[:, 0:D_ATTN]) * (HEAD_DIM ** -0.5 * LOG2_E)).astype(BF16)
        k = _dot(xn, win_ref[:, D_ATTN:D_ATTN + D_KV])
        v = _dot(xn, win_ref[:, D_ATTN + D_KV:D_ATTN + 2 * D_KV])
        uw_s[POOL_HALO + lo:POOL_HALO + lo + tp, :] = _dot(xn, win_ref[:, D_ATTN + 2 * D_KV:])
        for idx, (kk, vv) in enumerate(zip(_head_variants(k), _head_variants(v, 1.0))):
            kvar_s[idx, HALO + lo:HALO + lo + tp, :] = kk
            vvar_s[idx, HALO + lo:HALO + lo + tp, :] = vv
    k_tiles, v_tiles = _meta_tiles(kvu_ref[:, 0:D_KV], kvu_ref[:, D_KV:2 * D_KV])
    for h in range(2):
        kmeta_s[h] = k_tiles[h]
    for hp in range(4):
        vmeta_s[hp] = v_tiles[hp]

    def attend(blk):
        r0 = blk * Q_BLOCK
        q_chunk = lax.broadcasted_iota(jnp.int32, (Q_BLOCK, KEY_WINDOW), 0) // CHUNK
        k_chunk = lax.broadcasted_iota(jnp.int32, (Q_BLOCK, KEY_WINDOW), 1) // CHUNK
        valid = (k_chunk >= q_chunk) & (k_chunk <= q_chunk + 2)
        if blk == 0:
            valid = valid & (k_chunk >= jnp.where(i == 0, 2, 0))

        for j in range(D_ATTN // LANES):
            col = q_s[pl.ds(r0, Q_BLOCK), j * LANES:(j + 1) * LANES]
            for par in range(2):
                s = _dot_nt(col, kvar_s[2 * (j // 2) + par, pl.ds(r0, KEY_WINDOW), :])
                s_s[j, :, par * KEY_WINDOW:(par + 1) * KEY_WINDOW] = jnp.where(valid, s, NEG_INF)
            sm_s[j] = _dot_nt(col, kmeta_s[j // 2])

        lane = lax.broadcasted_iota(jnp.int32, (1, LANES), 1)
        for j in range(D_ATTN // LANES):
            for par in range(2):
                own_meta = (lane >= par * N_META) & (lane < (par + 1) * N_META)
                sink = sinks_ref[layer, 2 * j + par] * LOG2_E
                other = jnp.where(lane == 2 * N_META + par, sink, NEG_INF)
                for g in range(Q_BLOCK // SOFTMAX_ROWS):
                    rows = slice(g * SOFTMAX_ROWS, (g + 1) * SOFTMAX_ROWS)
                    s = s_s[j, rows, par * KEY_WINDOW:(par + 1) * KEY_WINDOW]
                    sm = jnp.where(own_meta, sm_s[j, rows, :], other)
                    m = jnp.maximum(jnp.maximum(s[:, :LANES], s[:, LANES:]), sm)
                    m = jnp.max(m, axis=1, keepdims=True)
                    e_s[j, rows, par * E_WIDTH:par * E_WIDTH + KEY_WINDOW] = (
                        jnp.exp2(s - m).astype(BF16))
                    e_s[j, rows, par * E_WIDTH + KEY_WINDOW:(par + 1) * E_WIDTH] = (
                        jnp.exp2(sm - m).astype(BF16))

        low = lax.broadcasted_iota(jnp.int32, (Q_BLOCK, LANES), 1) < HALF
        for j in range(D_ATTN // LANES):
            o = []
            for par in range(2):
                hp = 2 * (j // 2) + par
                e0 = par * E_WIDTH
                o.append(_dot(e_s[j, :, e0:e0 + KEY_WINDOW], vvar_s[hp, pl.ds(r0, KEY_WINDOW), :])
                         + _dot(e_s[j, :, e0 + KEY_WINDOW:e0 + E_WIDTH], vmeta_s[hp]))
            num = jnp.where(low, o[0], o[1])
            den = pltpu.roll(jnp.where(low, o[1], o[0]), HALF, axis=1)
            a_s[pl.ds(r0, Q_BLOCK), j * LANES:(j + 1) * LANES] = (num / den).astype(BF16)

    for blk in range(tq // Q_BLOCK):
        attend(blk)

    for lo in range(0, tq, tp):
        pooled = []
        for g, w in enumerate(POOL_SIZES):
            cs = slice(g * POOL_GROUP, (g + 1) * POOL_GROUP)
            cur = uw_s[POOL_HALO + lo:POOL_HALO + lo + tp, cs]
            acc = cur
            for lag in range(1, w):
                acc = acc + uw_s[POOL_HALO + lo - lag:POOL_HALO + lo - lag + tp, cs]
            d = acc * (1.0 / w) - cur
            pooled.append(_dot(d.astype(BF16), poolw_ref[g]) * pscale_ref[layer:layer + 1, cs])
        p = jnp.concatenate(pooled, axis=1).astype(BF16)

        hn = (h_ref[lo:lo + tp, :] + _dot(a_s[lo:lo + tp, :], wout_ref[0:D_ATTN, :])
              + _dot(p, wout_ref[D_ATTN:, :]))
        hout_ref[lo:lo + tp, :] = hn
        xn_ref[lo:lo + tp, :] = _rmsnorm(hn, gffn_ref[layer:layer + 1, :]).astype(xn_ref.dtype)

    kvar_s[:, 0:HALO, :] = kvar_s[:, tq:tq + HALO, :]
    vvar_s[:, 0:HALO, :] = vvar_s[:, tq:tq + HALO, :]
    uw_s[0:POOL_HALO, :] = uw_s[tq:tq + POOL_HALO, :]


def _layer_spec(w, layer):
    if w.ndim == 2:
        return pl.BlockSpec(w.shape, lambda *_: (0, 0))
    return pl.BlockSpec((None,) + w.shape[1:], lambda *_: (layer,) + (0,) * (w.ndim - 1))


def _mix_call(h, batch, layer, sinks, gmix, win, kvu, poolw, pscale, wout, gffn, xn_dtype,
              to_bf16=()):
    t, d = h.shape
    tq = MIX_TILE
    nt = t // batch // tq
    assert nt * tq * batch == t
    row = pl.BlockSpec((tq, d), lambda b, i: (b * nt + i, 0))
    flat = [w.reshape(-1, w.shape[-1]) for w in to_bf16]
    slabs = [pl.BlockSpec((w.shape[0] // (batch * nt), w.shape[1]), lambda b, i: (b * nt + i, 0))
             for w in flat]
    assert all(w.shape[0] % (16 * batch * nt) == 0 for w in flat)
    outs = pl.pallas_call(
        functools.partial(_mix_kernel, tq=tq, n_cast=len(flat), layer=layer),
        grid=(batch, nt),
        in_specs=[
            pl.BlockSpec(memory_space=pltpu.SMEM), row, _layer_spec(gmix, layer),
            _layer_spec(win, layer), pl.BlockSpec(kvu.shape, lambda b, i: (0, 0)),
            _layer_spec(poolw, layer), _layer_spec(pscale, layer), _layer_spec(wout, layer),
            _layer_spec(gffn, layer),
        ] + slabs,
        out_specs=[row, row] + slabs,
        out_shape=[jax.ShapeDtypeStruct((t, d), F32), jax.ShapeDtypeStruct((t, d), xn_dtype)]
        + [jax.ShapeDtypeStruct(w.shape, BF16) for w in flat],
        scratch_shapes=[
            pltpu.VMEM((tq, D_ATTN), BF16),
            pltpu.VMEM((4, HALO + tq, LANES), BF16),
            pltpu.VMEM((4, HALO + tq, LANES), BF16),
            pltpu.VMEM((POOL_HALO + tq, D_POOL), F32),
            pltpu.VMEM((tq, D_ATTN), BF16),
            pltpu.VMEM((2, LANES, LANES), BF16),
            pltpu.VMEM((4, LANES, LANES), BF16),
            pltpu.VMEM((D_ATTN // LANES, Q_BLOCK, 2 * KEY_WINDOW), F32),
            pltpu.VMEM((D_ATTN // LANES, Q_BLOCK, LANES), F32),
            pltpu.VMEM((D_ATTN // LANES, Q_BLOCK, 2 * E_WIDTH), BF16),
        ],
        compiler_params=pltpu.CompilerParams(
            dimension_semantics=("arbitrary", "arbitrary"), vmem_limit_bytes=VMEM_LIMIT),
        name="mix",
    )(sinks, h, gmix, win, kvu, poolw, pscale, wout, gffn, *flat)
    return outs[0], outs[1], *[o.reshape(w.shape) for o, w in zip(outs[2:], to_bf16)]


def _meta_kernel(sinks_ref, h_ref, gmix_ref, win_ref, poolw_ref, pscale_ref, wout_ref, gffn_ref,
                 kvu_ref, hout_ref, xn_ref, uw_s, *, layer):
    x = h_ref[...]
    xn = _rmsnorm(x, gmix_ref[layer:layer + 1, :]).astype(BF16)
    q = (_dot(xn, win_ref[:, 0:D_ATTN]) * (HEAD_DIM ** -0.5)).astype(BF16)
    k = _dot(xn, win_ref[:, D_ATTN:D_ATTN + D_KV])
    v = _dot(xn, win_ref[:, D_ATTN + D_KV:D_ATTN + 2 * D_KV])
    u = _dot(xn, win_ref[:, D_ATTN + 2 * D_KV:])
    kvu_ref[:, 0:D_KV] = k
    kvu_ref[:, D_KV:2 * D_KV] = v
    kvu_ref[:, 2 * D_KV:] = u

    k_var = _head_variants(k)
    v_var = _head_variants(v)
    cols = []
    for j in range(D_ATTN // LANES):
        col = q[:, j * LANES:(j + 1) * LANES]
        acc = None
        for par in range(2):
            var = 2 * (j // 2) + par
            o = _sink_softmax_pv((_dot_nt(col, k_var[var]),), (v_var[var],),
                                 sinks_ref[layer, 2 * j + par])
            acc = o if acc is None else acc + o
        cols.append(acc)
    a = jnp.concatenate(cols, axis=1).astype(BF16)

    uw_s[0:POOL_HALO, :] = jnp.zeros((POOL_HALO, D_POOL), F32)
    uw_s[POOL_HALO:, :] = u
    pos = lax.broadcasted_iota(jnp.int32, (N_META, 1), 0)
    pooled = []
    for g, w in enumerate(POOL_SIZES):
        cs = slice(g * POOL_GROUP, (g + 1) * POOL_GROUP)
        cur = uw_s[POOL_HALO:, cs]
        acc = cur
        for lag in range(1, w):
            acc = acc + uw_s[POOL_HALO - lag:POOL_HALO - lag + N_META, cs]
        count = jnp.minimum(pos + 1, w).astype(F32)
        d = acc / count - cur
        pooled.append(_dot(d.astype(BF16), poolw_ref[g]) * pscale_ref[layer:layer + 1, cs])
    p = jnp.concatenate(pooled, axis=1).astype(BF16)

    hn = x + _dot(a, wout_ref[0:D_ATTN, :]) + _dot(p, wout_ref[D_ATTN:, :])
    hout_ref[...] = hn
    xn_ref[...] = _rmsnorm(hn, gffn_ref[layer:layer + 1, :]).astype(xn_ref.dtype)


def _meta_call(h, layer, sinks, gmix, win, poolw, pscale, wout, gffn):
    n, d = h.shape
    whole = lambda shape: pl.BlockSpec(shape, lambda i: (0,) * len(shape))
    params = (gmix, win, poolw, pscale, wout, gffn)
    out_shapes = ((n, 2 * D_KV + D_POOL), (n, d), (n, d))
    return pl.pallas_call(
        functools.partial(_meta_kernel, layer=layer),
        grid=(1,),
        in_specs=[pl.BlockSpec(memory_space=pltpu.SMEM), whole(h.shape)]
        + [_layer_spec(w, layer) for w in params],
        out_specs=[whole(s) for s in out_shapes],
        out_shape=[jax.ShapeDtypeStruct(s, dt) for s, dt in zip(out_shapes, (F32, F32, BF16))],
        scratch_shapes=[pltpu.VMEM((POOL_HALO + N_META, D_POOL), F32)],
        compiler_params=pltpu.CompilerParams(
            dimension_semantics=("arbitrary",), vmem_limit_bytes=VMEM_LIMIT),
        name="meta_mix",
    )(sinks, h, *params)


def _ffn_kernel(tab_ref, na_ref, x_ref, *refs, masked):
    res_ref = refs[0] if len(refs) == 5 else None
    wg_ref, wu_ref, wd_ref, out_ref = refs[-4:]
    i, j = pl.program_id(0), pl.program_id(1)

    @pl.when(i < na_ref[0])
    def _():
        @pl.when(j == 0)
        def _():
            out_ref[...] = jnp.zeros_like(out_ref) if res_ref is None else res_ref[...]

        wg, wu, wd = (w[...].astype(BF16) for w in (wg_ref, wu_ref, wd_ref))
        valid = tab_ref[2, i]
        tm = x_ref.shape[0]
        sub = min(tm, FFN_SUBTILE)
        for r0 in range(0, tm, sub):
            def part(r0=r0):
                x = x_ref[r0:r0 + sub].reshape(sub, -1)
                if masked:
                    r = r0 + lax.broadcasted_iota(jnp.int32, (sub, 1), 0)
                    x = jnp.where(r < valid, x, 0.0)
                x = x.astype(BF16)
                gate = _dot(x, wg)
                up = _dot(x, wu)
                mid = (gate * jax.nn.sigmoid(gate) * up).astype(BF16)
                out_ref[r0:r0 + sub, :] += _dot(mid, wd)

            if masked and r0 > 0:
                pl.when(valid > r0)(part)
            else:
                part()


def _ffn_call(x, res, wg, wu, wd, table, n_active, tm, masked):
    rows, d = x.shape[0], wg.shape[-2]
    ff = wg.shape[-1]
    n_grid = table.shape[1]
    nj = ff // FFN_BLOCK
    assert rows % tm == 0 and nj * FFN_BLOCK == ff

    def live(i, na):
        return jnp.maximum(jnp.minimum(i, na[0] - 1), 0)

    def ff_block(i, j, na):
        return jnp.where(i < na[0], j, nj - 1)

    row = pl.BlockSpec((tm, d), lambda i, j, tab, na: (tab[1, live(i, na)], 0))
    col_w = pl.BlockSpec((None, d, FFN_BLOCK),
                         lambda i, j, tab, na: (tab[0, live(i, na)], 0, ff_block(i, j, na)))
    row_w = pl.BlockSpec((None, FFN_BLOCK, d),
                         lambda i, j, tab, na: (tab[0, live(i, na)], ff_block(i, j, na), 0))
    acts = (x,) if res is None else (x, res)
    x_rows = row if x.ndim == 2 else pl.BlockSpec(
        (tm,) + x.shape[1:], lambda i, j, tab, na: (tab[1, live(i, na)], 0, 0))
    return pl.pallas_call(
        functools.partial(_ffn_kernel, masked=masked),
        grid_spec=pltpu.PrefetchScalarGridSpec(
            num_scalar_prefetch=2, grid=(n_grid, nj),
            in_specs=[x_rows] + [row] * (len(acts) - 1) + [col_w, col_w, row_w], out_specs=row),
        out_shape=jax.ShapeDtypeStruct((rows, d), F32),
        compiler_params=pltpu.CompilerParams(
            dimension_semantics=("arbitrary", "arbitrary"), vmem_limit_bytes=VMEM_LIMIT),
        name="ffn",
    )(table, n_active, *acts, wg, wu, wd)


def _dense_table(n_tiles, tm):
    ids = jnp.arange(n_tiles, dtype=jnp.int32)
    return (jnp.stack([jnp.zeros_like(ids), ids, jnp.full_like(ids, tm)]),
            jnp.full((1,), n_tiles, jnp.int32))


def _split_bf16(x):
    hi = x.astype(BF16)
    return hi, (x - hi.astype(F32)).astype(BF16)


def _for_each_routed_row(n_rows, pos_ref, fn):
    chunks = n_rows // LANES
    for r in range(n_rows):
        for kk in range(2):
            fn(r, kk, pos_ref[kk * chunks + r // LANES, r % LANES])


def _route_dispatch_kernel(x_ref, wr_ref, tri_ref, xs_ref, pos_ref, gate_ref, counts_ref,
                           run_s, pos_sm, xbuf, sem_pos, sem_rows, *, tr, cap, n):
    i = pl.program_id(0)
    slot = lax.rem(i, ROW_SLOTS)

    def wait_rows(s):
        for _ in range(2):
            pltpu.make_async_copy(xbuf.at[s], xs_ref.at[pl.ds(0, tr)], sem_rows.at[s]).wait()

    def issue_rows(s):
        _for_each_routed_row(tr, pos_sm.at[0], lambda r, kk, p: pltpu.make_async_copy(
            xbuf.at[s, r], xs_ref.at[p], sem_rows.at[s]).start(kk))

    @pl.when(i == 0)
    def _():
        run_s[...] = jnp.zeros_like(run_s)

    @pl.when(i >= ROW_SLOTS)
    def _():
        wait_rows(slot)

    @pl.when(i == 0)
    def _():
        _route(x_ref, wr_ref, tri_ref, pos_ref, gate_ref, counts_ref, run_s, xbuf.at[slot],
               pos_sm, sem_pos, tr, cap)

    for s in range(ROW_SLOTS):
        @pl.when((i > 0) & (slot == s))
        def _(s=s):
            issue_rows((s + ROW_SLOTS - 1) % ROW_SLOTS)
            _route(x_ref, wr_ref, tri_ref, pos_ref, gate_ref, counts_ref, run_s, xbuf.at[s],
                   pos_sm, sem_pos, tr, cap)

    @pl.when(i == n - 1)
    def _():
        issue_rows(slot)
        for back in range(min(n, ROW_SLOTS)):
            wait_rows(lax.rem(i + ROW_SLOTS - back, ROW_SLOTS))


def _route(x_ref, wr_ref, tri_ref, pos_ref, gate_ref, counts_ref, run_s, xcopy, pos_sm, sem_pos,
           tr, cap):
    xcopy[...] = x_ref[...].reshape(xcopy.shape)

    x_hi, x_lo = _split_bf16(x_ref[...])
    w_hi, w_lo = _split_bf16(wr_ref[...])
    logits = _dot_nt(w_hi, x_hi) + (_dot_nt(w_lo, x_hi) + _dot_nt(w_hi, x_lo))

    expert = lax.broadcasted_iota(jnp.int32, logits.shape, 0).astype(F32)
    none = float(N_EXPERTS)
    m1 = jnp.max(logits, axis=0, keepdims=True)
    e1 = jnp.min(jnp.where(logits == m1, expert, none), axis=0, keepdims=True)
    rest = jnp.where(expert == e1, -jnp.inf, logits)
    m2 = jnp.max(rest, axis=0, keepdims=True)
    e2 = jnp.min(jnp.where(rest == m2, expert, none), axis=0, keepdims=True)
    t = jnp.exp(m2 - m1)
    gate_ref[...] = jnp.concatenate(
        [1.0 / (1.0 + t), t / (1.0 + t), jnp.zeros((6, tr), F32)], axis=0)

    chosen = ((expert == e1) | (expert == e2)).astype(F32)
    before = _dot(chosen.astype(BF16), tri_ref[...]) + run_s[:, 0:1]
    pieces = []
    for e in (e1, e2):
        rank = jnp.sum(jnp.where(expert == e, before, 0.0), axis=0, keepdims=True)
        pos = (e * float(cap) + rank).astype(jnp.int32)
        pieces += [pos[:, c * LANES:(c + 1) * LANES] for c in range(tr // LANES)]
    pos_ref[0] = jnp.concatenate(pieces, axis=0)
    run_s[...] += jnp.sum(chosen, axis=1, keepdims=True)
    counts_ref[...] = run_s[...].astype(jnp.int32)

    to_smem = pltpu.make_async_copy(pos_ref, pos_sm, sem_pos)
    to_smem.start()
    to_smem.wait()


def _route_dispatch_call(xn, w_router):
    t, d = xn.shape
    tr = ROW_TILE
    n = t // tr
    tri = (lax.broadcasted_iota(jnp.int32, (tr, tr), 0)
           < lax.broadcasted_iota(jnp.int32, (tr, tr), 1)).astype(BF16)
    return pl.pallas_call(
        functools.partial(_route_dispatch_kernel, tr=tr, cap=t, n=n),
        grid=(n,),
        in_specs=[pl.BlockSpec((tr, d), lambda i: (i, 0)),
                  pl.BlockSpec((N_EXPERTS, d), lambda i: (0, 0)),
                  pl.BlockSpec((tr, tr), lambda i: (0, 0))],
        out_specs=[pl.BlockSpec(memory_space=pl.ANY),
                   pl.BlockSpec((1, 2 * tr // LANES, LANES), lambda i: (i, 0, 0)),
                   pl.BlockSpec((None, 8, tr), lambda i: (i, 0, 0)),
                   pl.BlockSpec((N_EXPERTS, LANES), lambda i: (0, 0))],
        out_shape=[jax.ShapeDtypeStruct((N_EXPERTS * t, d // LANES, LANES), F32),
                   jax.ShapeDtypeStruct((n, 2 * tr // LANES, LANES), jnp.int32),
                   jax.ShapeDtypeStruct((n, 8, tr), F32),
                   jax.ShapeDtypeStruct((N_EXPERTS, LANES), jnp.int32)],
        scratch_shapes=[pltpu.VMEM((N_EXPERTS, LANES), F32),
                        pltpu.SMEM((1, 2 * tr // LANES, LANES), jnp.int32),
                        pltpu.VMEM((ROW_SLOTS, tr, d // LANES, LANES), F32),
                        pltpu.SemaphoreType.DMA, pltpu.SemaphoreType.DMA((ROW_SLOTS,))],
        compiler_params=pltpu.CompilerParams(
            dimension_semantics=("arbitrary",), vmem_limit_bytes=VMEM_LIMIT),
        name="route_dispatch",
    )(xn, w_router.T, tri)


def _combine_kernel(pos_ref, pos_next_ref, h_ref, gate_ref, gfin_ref, y_ref, out_ref, buf, sem,
                    *, tc, n):
    i = pl.program_id(0)
    slot = i % 2

    def issue(p_ref, s):
        _for_each_routed_row(tc, p_ref, lambda r, kk, p: pltpu.make_async_copy(
            y_ref.at[pl.ds(p, 1)], buf.at[s, kk, pl.ds(r, 1)], sem.at[s]).start(kk))

    @pl.when(i == 0)
    def _():
        issue(pos_ref, 0)

    for s in range(2):
        @pl.when((i + 1 < n) & (slot == 1 - s))
        def _(s=s):
            issue(pos_next_ref, s)

    for kk in range(2):
        pltpu.make_async_copy(y_ref.at[pl.ds(0, tc)], buf.at[slot, kk], sem.at[slot]).wait()
    gates = gate_ref[...].T
    hn = h_ref[...] + (gates[:, 0:1] * buf[slot, 0] + gates[:, 1:2] * buf[slot, 1])
    out_ref[...] = _rmsnorm(hn, gfin_ref[...])


def _combine_call(h, y, pos, gates, gfin):
    t, d = h.shape
    tc = ROW_TILE
    n = t // tc
    smem_pos = lambda index_map: pl.BlockSpec((None, 2 * tc // LANES, LANES), index_map,
                                              memory_space=pltpu.SMEM)
    rows = lambda width: pl.BlockSpec((tc, width), lambda i: (i, 0))
    return pl.pallas_call(
        functools.partial(_combine_kernel, tc=tc, n=n),
        grid=(n,),
        in_specs=[smem_pos(lambda i: (i, 0, 0)),
                  smem_pos(lambda i: (jnp.minimum(i + 1, n - 1), 0, 0)),
                  rows(d), pl.BlockSpec((None, 8, tc), lambda i: (i, 0, 0)),
                  pl.BlockSpec((1, d), lambda i: (0, 0)),
                  pl.BlockSpec(memory_space=pl.ANY)],
        out_specs=rows(d),
        out_shape=jax.ShapeDtypeStruct((t, d), F32),
        scratch_shapes=[pltpu.VMEM((2, 2, tc, d), F32), pltpu.SemaphoreType.DMA((2,))],
        compiler_params=pltpu.CompilerParams(
            dimension_semantics=("arbitrary",), vmem_limit_bytes=VMEM_LIMIT),
        name="combine",
    )(pos, pos, h, gates, gfin, y)


def _moe(h, xn, w_router, wg, wu, wd, gfin):
    t, d = h.shape
    tm = FFN_TILE
    xs, pos, gates, counts = _route_dispatch_call(xn, w_router)
    counts = counts[:, 0]

    n_grid = 2 * t // tm + N_EXPERTS
    tiles_of = (counts + tm - 1) // tm
    tile_end = jnp.cumsum(tiles_of)
    ids = jnp.arange(n_grid, dtype=jnp.int32)[:, None]
    owner = (ids >= (tile_end - tiles_of)[None, :]) & (ids < tile_end[None, :])
    pick = lambda per_expert: jnp.sum(jnp.where(owner, per_expert[None, :], 0), axis=1)
    expert = pick(jnp.arange(N_EXPERTS, dtype=jnp.int32))
    local = ids[:, 0] - pick(tile_end - tiles_of)
    table = jnp.stack([expert, expert * (t // tm) + local,
                       jnp.clip(pick(counts) - local * tm, 0, tm)]).astype(jnp.int32)
    y = _ffn_call(xs, None, wg, wu, wd, table, tile_end[-1:].astype(jnp.int32), tm, True)
    return _combine_call(h, y, pos, gates, gfin)


def kernel(x, meta_tokens, norm_mix_g, w_in, attn_sinks, pool_w, pool_scale, w_out, norm_ffn_g,
           dense_w_gate, dense_w_up, dense_w_down, moe_w_router, moe_w_gate, moe_w_up,
           moe_w_down, final_norm_g):
    b, s, d = x.shape
    h = x.reshape(b * s, d)
    mixer = (attn_sinks, norm_mix_g, w_in.astype(BF16))
    mixer_out = (pool_w.astype(BF16), pool_scale, w_out.astype(BF16), norm_ffn_g)

    kvu0, hm, xnm = _meta_call(meta_tokens, 0, *mixer, *mixer_out)
    h, xn, *dense, moe_gate = _mix_call(
        h, b, 0, *mixer, kvu0, *mixer_out, BF16,
        (dense_w_gate, dense_w_up, dense_w_down, moe_w_gate[0]))
    hm = _ffn_call(xnm, hm, *dense, *_dense_table(1, N_META), N_META, False)
    kvu1, _, _ = _meta_call(hm, 1, *mixer, *mixer_out)
    h = _ffn_call(xn, h, *dense, *_dense_table(b * s // FFN_TILE, FFN_TILE), FFN_TILE, False)
    h, xn, moe_up, moe_down = _mix_call(
        h, b, 1, *mixer, kvu1, *mixer_out, F32, (moe_w_up[0], moe_w_down[0]))
    out = _moe(h, xn, moe_w_router[0], moe_gate, moe_up, moe_down, final_norm_g.reshape(1, -1))
    return out.reshape(b, s, d)
```

```python
import functools

import jax
import jax.numpy as jnp
from jax import lax
from jax.experimental import pallas as pl
from jax.experimental.pallas import tpu as pltpu

F32 = jnp.float32
BF16 = jnp.bfloat16

CHUNK = 64
N_META = 16
HEAD_DIM = 64
D_ATTN = 512
D_KV = 128
D_POOL = 512
POOL_SIZES = (2, 4, 8, 16)
POOL_GROUP = 128
N_EXPERTS = 8
EPS = 1e-5
NEG_INF = -1e30
LOG2_E = 1.4426950408889634

LANES = 128
HALF = LANES // 2
Q_BLOCK = 2 * CHUNK
KEY_WINDOW = 4 * CHUNK
HALO = 2 * CHUNK
E_WIDTH = KEY_WINDOW + LANES
SOFTMAX_ROWS = 32
POOL_HALO = max(POOL_SIZES)
VMEM_LIMIT = 56 * 1024 * 1024

MIX_TILE = 512
MIX_PARTS = 2
FFN_TILE = 512
FFN_SUBTILE = 512
FFN_BLOCK = 1792
ROW_TILE = 512
ROW_SLOTS = 3


def _rmsnorm(x, g):
    return x * lax.rsqrt(jnp.mean(x * x, axis=-1, keepdims=True) + EPS) * g


def _dot(a, b):
    return jnp.dot(a, b, preferred_element_type=F32)


def _dot_nt(a, b):
    return lax.dot_general(a, b, (((1,), (1,)), ((), ())), preferred_element_type=F32)


def _head_variants(t, fill=0.0):
    low = lax.broadcasted_iota(jnp.int32, t.shape, 1) < HALF
    t_r = pltpu.roll(t, HALF, axis=1)
    other = jnp.full_like(t, fill)
    return (jnp.where(low, t, other).astype(BF16),
            jnp.where(low, other, t_r).astype(BF16),
            jnp.where(low, t_r, other).astype(BF16),
            jnp.where(low, other, t).astype(BF16))


def _sink_softmax_pv(s_parts, v_parts, sink):
    m = sink
    for s in s_parts:
        m = jnp.maximum(m, jnp.max(s, axis=1, keepdims=True))
    den = jnp.exp(sink - m)
    o = None
    for s, v in zip(s_parts, v_parts):
        e = jnp.exp(s - m)
        den = den + jnp.sum(e, axis=1, keepdims=True)
        pv = _dot(e.astype(BF16), v)
        o = pv if o is None else o + pv
    return o / den


def _meta_tiles(km, vm):
    km_var = _head_variants(km)
    vm_var = _head_variants(vm, 1.0)
    zeros = lambda n: jnp.zeros((n, LANES), BF16)
    row = lax.broadcasted_iota(jnp.int32, (N_META, LANES), 0)
    low = lax.broadcasted_iota(jnp.int32, (N_META, LANES), 1) < HALF
    rest = LANES - 3 * N_META
    k_tiles, v_tiles = [], []
    for h in range(2):
        k_tiles.append(jnp.concatenate([km_var[2 * h], km_var[2 * h + 1], zeros(LANES - 2 * N_META)]))
        for par in range(2):
            den_half = low if par == 1 else jnp.logical_not(low)
            sink_rows = jnp.where((row == par) & den_half, 1.0, 0.0).astype(BF16)
            v_tiles.append(jnp.concatenate(
                [vm_var[2 * h] if par == 0 else zeros(N_META),
                 vm_var[2 * h + 1] if par == 1 else zeros(N_META), sink_rows, zeros(rest)]))
    return k_tiles, v_tiles


def _mix_kernel(*refs, tq, n_cast, layer):
    (sinks_ref, h_ref, gmix_ref, win_ref, kvu_ref, poolw_ref, pscale_ref, wout_ref,
     gffn_ref) = refs[:9]
    cast_in = refs[9:9 + n_cast]
    hout_ref, xn_ref = refs[9 + n_cast:11 + n_cast]
    cast_out = refs[11 + n_cast:11 + 2 * n_cast]
    q_s, kvar_s, vvar_s, uw_s, a_s, kmeta_s, vmeta_s, s_s, sm_s, e_s = refs[11 + 2 * n_cast:]
    i = pl.program_id(1)

    for src, dst in zip(cast_in, cast_out):
        dst[...] = src[...].astype(BF16)

    @pl.when(i == 0)
    def _():
        kvar_s[:, 0:HALO, :] = jnp.zeros((4, HALO, LANES), BF16)
        vvar_s[:, 0:HALO, :] = jnp.zeros((4, HALO, LANES), BF16)
        uw_s[0:POOL_HALO, :] = kvu_ref[:, 2 * D_KV:]

    tp = tq // MIX_PARTS
    for lo in range(0, tq, tp):
        xn = _rmsnorm(h_ref[lo:lo + tp, :], gmix_ref[layer:layer + 1, :]).astype(BF16)
        q_s[lo:lo + tp, :] = (
            _dot(xn, win_ref[:, 0:D_ATTN]) * (HEAD_DIM ** -0.5 * LOG2_E)).astype(BF16)
        k = _dot(xn, win_ref[:, D_ATTN:D_ATTN + D_KV])
        v = _dot(xn, win_ref[:, D_ATTN + D_KV:D_ATTN + 2 * D_KV])
        uw_s[POOL_HALO + lo:POOL_HALO + lo + tp, :] = _dot(xn, win_ref[:, D_ATTN + 2 * D_KV:])
        for idx, (kk, vv) in enumerate(zip(_head_variants(k), _head_variants(v, 1.0))):
            kvar_s[idx, HALO + lo:HALO + lo + tp, :] = kk
            vvar_s[idx, HALO + lo:HALO + lo + tp, :] = vv
    k_tiles, v_tiles = _meta_tiles(kvu_ref[:, 0:D_KV], kvu_ref[:, D_KV:2 * D_KV])
    for h in range(2):
        kmeta_s[h] = k_tiles[h]
    for hp in range(4):
        vmeta_s[hp] = v_tiles[hp]

    def attend(blk):
        r0 = blk * Q_BLOCK
        q_chunk = lax.broadcasted_iota(jnp.int32, (Q_BLOCK, KEY_WINDOW), 0) // CHUNK
        k_chunk = lax.broadcasted_iota(jnp.int32, (Q_BLOCK, KEY_WINDOW), 1) // CHUNK
        valid = (k_chunk >= q_chunk) & (k_chunk <= q_chunk + 2)
        if blk == 0:
            valid = valid & (k_chunk >= jnp.where(i == 0, 2, 0))

        for j in range(D_ATTN // LANES):
            col = q_s[pl.ds(r0, Q_BLOCK), j * LANES:(j + 1) * LANES]
            for par in range(2):
                s = _dot_nt(col, kvar_s[2 * (j // 2) + par, pl.ds(r0, KEY_WINDOW), :])
                s_s[j, :, par * KEY_WINDOW:(par + 1) * KEY_WINDOW] = jnp.where(valid, s, NEG_INF)
            sm_s[j] = _dot_nt(col, kmeta_s[j // 2])

        lane = lax.broadcasted_iota(jnp.int32, (1, LANES), 1)
        for j in range(D_ATTN // LANES):
            for par in range(2):
                own_meta = (lane >= par * N_META) & (lane < (par + 1) * N_META)
                sink = sinks_ref[layer, 2 * j + par] * LOG2_E
                other = jnp.where(lane == 2 * N_META + par, sink, NEG_INF)
                for g in range(Q_BLOCK // SOFTMAX_ROWS):
                    rows = slice(g * SOFTMAX_ROWS, (g + 1) * SOFTMAX_ROWS)
                    s = s_s[j, rows, par * KEY_WINDOW:(par + 1) * KEY_WINDOW]
                    sm = jnp.where(own_meta, sm_s[j, rows, :], other)
                    m = jnp.maximum(jnp.maximum(s[:, :LANES], s[:, LANES:]), sm)
                    m = jnp.max(m, axis=1, keepdims=True)
                    e_s[j, rows, par * E_WIDTH:par * E_WIDTH + KEY_WINDOW] = (
                        jnp.exp2(s - m).astype(BF16))
                    e_s[j, rows, par * E_WIDTH + KEY_WINDOW:(par + 1) * E_WIDTH] = (
                        jnp.exp2(sm - m).astype(BF16))

        low = lax.broadcasted_iota(jnp.int32, (Q_BLOCK, LANES), 1) < HALF
        for j in range(D_ATTN // LANES):
            o = []
            for par in range(2):
                hp = 2 * (j // 2) + par
                e0 = par * E_WIDTH
                o.append(_dot(e_s[j, :, e0:e0 + KEY_WINDOW], vvar_s[hp, pl.ds(r0, KEY_WINDOW), :])
                         + _dot(e_s[j, :, e0 + KEY_WINDOW:e0 + E_WIDTH], vmeta_s[hp]))
            num = jnp.where(low, o[0], o[1])
            den = pltpu.roll(jnp.where(low, o[1], o[0]), HALF, axis=1)
            a_s[pl.ds(r0, Q_BLOCK), j * LANES:(j + 1) * LANES] = (num / den).astype(BF16)

    for blk in range(tq // Q_BLOCK):
        attend(blk)

    for lo in range(0, tq, tp):
        pooled = []
        for g, w in enumerate(POOL_SIZES):
            cs = slice(g * POOL_GROUP, (g + 1) * POOL_GROUP)
            cur = uw_s[POOL_HALO + lo:POOL_HALO + lo + tp, cs]
            acc = cur
            for lag in range(1, w):
                acc = acc + uw_s[POOL_HALO + lo - lag:POOL_HALO + lo - lag + tp, cs]
            d = acc * (1.0 / w) - cur
            pooled.append(_dot(d.astype(BF16), poolw_ref[g]) * pscale_ref[layer:layer + 1, cs])
        p = jnp.concatenate(pooled, axis=1).astype(BF16)

        hn = (h_ref[lo:lo + tp, :] + _dot(a_s[lo:lo + tp, :], wout_ref[0:D_ATTN, :])
              + _dot(p, wout_ref[D_ATTN:, :]))
        hout_ref[lo:lo + tp, :] = hn
        xn_ref[lo:lo + tp, :] = _rmsnorm(hn, gffn_ref[layer:layer + 1, :]).astype(xn_ref.dtype)

    kvar_s[:, 0:HALO, :] = kvar_s[:, tq:tq + HALO, :]
    vvar_s[:, 0:HALO, :] = vvar_s[:, tq:tq + HALO, :]
    uw_s[0:POOL_HALO, :] = uw_s[tq:tq + POOL_HALO, :]


def _layer_spec(w, layer):
    if w.ndim == 2:
        return pl.BlockSpec(w.shape, lambda *_: (0, 0))
    return pl.BlockSpec((None,) + w.shape[1:], lambda *_: (layer,) + (0,) * (w.ndim - 1))


def _mix_call(h, batch, layer, sinks, gmix, win, kvu, poolw, pscale, wout, gffn, xn_dtype,
              to_bf16=()):
    t, d = h.shape
    tq = MIX_TILE
    nt = t // batch // tq
    assert nt * tq * batch == t
    row = pl.BlockSpec((tq, d), lambda b, i: (b * nt + i, 0))
    flat = [w.reshape(-1, w.shape[-1]) for w in to_bf16]
    slabs = [pl.BlockSpec((w.shape[0] // (batch * nt), w.shape[1]), lambda b, i: (b * nt + i, 0))
             for w in flat]
    assert all(w.shape[0] % (16 * batch * nt) == 0 for w in flat)
    outs = pl.pallas_call(
        functools.partial(_mix_kernel, tq=tq, n_cast=len(flat), layer=layer),
        grid=(batch, nt),
        in_specs=[
            pl.BlockSpec(memory_space=pltpu.SMEM), row, _layer_spec(gmix, layer),
            _layer_spec(win, layer), pl.BlockSpec(kvu.shape, lambda b, i: (0, 0)),
            _layer_spec(poolw, layer), _layer_spec(pscale, layer), _layer_spec(wout, layer),
            _layer_spec(gffn, layer),
        ] + slabs,
        out_specs=[row, row] + slabs,
        out_shape=[jax.ShapeDtypeStruct((t, d), F32), jax.ShapeDtypeStruct((t, d), xn_dtype)]
        + [jax.ShapeDtypeStruct(w.shape, BF16) for w in flat],
        scratch_shapes=[
            pltpu.VMEM((tq, D_ATTN), BF16),
            pltpu.VMEM((4, HALO + tq, LANES), BF16),
            pltpu.VMEM((4, HALO + tq, LANES), BF16),
            pltpu.VMEM((POOL_HALO + tq, D_POOL), F32),
            pltpu.VMEM((tq, D_ATTN), BF16),
            pltpu.VMEM((2, LANES, LANES), BF16),
            pltpu.VMEM((4, LANES, LANES), BF16),
            pltpu.VMEM((D_ATTN // LANES, Q_BLOCK, 2 * KEY_WINDOW), F32),
            pltpu.VMEM((D_ATTN // LANES, Q_BLOCK, LANES), F32),
            pltpu.VMEM((D_ATTN // LANES, Q_BLOCK, 2 * E_WIDTH), BF16),
        ],
        compiler_params=pltpu.CompilerParams(
            dimension_semantics=("arbitrary", "arbitrary"), vmem_limit_bytes=VMEM_LIMIT),
        name="mix",
    )(sinks, h, gmix, win, kvu, poolw, pscale, wout, gffn, *flat)
    return outs[0], outs[1], *[o.reshape(w.shape) for o, w in zip(outs[2:], to_bf16)]


def _meta_kernel(sinks_ref, h_ref, gmix_ref, win_ref, poolw_ref, pscale_ref, wout_ref, gffn_ref,
                 kvu_ref, hout_ref, xn_ref, uw_s, *, layer):
    x = h_ref[...]
    xn = _rmsnorm(x, gmix_ref[layer:layer + 1, :]).astype(BF16)
    q = (_dot(xn, win_ref[:, 0:D_ATTN]) * (HEAD_DIM ** -0.5)).astype(BF16)
    k = _dot(xn, win_ref[:, D_ATTN:D_ATTN + D_KV])
    v = _dot(xn, win_ref[:, D_ATTN + D_KV:D_ATTN + 2 * D_KV])
    u = _dot(xn, win_ref[:, D_ATTN + 2 * D_KV:])
    kvu_ref[:, 0:D_KV] = k
    kvu_ref[:, D_KV:2 * D_KV] = v
    kvu_ref[:, 2 * D_KV:] = u

    k_var = _head_variants(k)
    v_var = _head_variants(v)
    cols = []
    for j in range(D_ATTN // LANES):
        col = q[:, j * LANES:(j + 1) * LANES]
        acc = None
        for par in range(2):
            var = 2 * (j // 2) + par
            o = _sink_softmax_pv((_dot_nt(col, k_var[var]),), (v_var[var],),
                                 sinks_ref[layer, 2 * j + par])
            acc = o if acc is None else acc + o
        cols.append(acc)
    a = jnp.concatenate(cols, axis=1).astype(BF16)

    uw_s[0:POOL_HALO, :] = jnp.zeros((POOL_HALO, D_POOL), F32)
    uw_s[POOL_HALO:, :] = u
    pos = lax.broadcasted_iota(jnp.int32, (N_META, 1), 0)
    pooled = []
    for g, w in enumerate(POOL_SIZES):
        cs = slice(g * POOL_GROUP, (g + 1) * POOL_GROUP)
        cur = uw_s[POOL_HALO:, cs]
        acc = cur
        for lag in range(1, w):
            acc = acc + uw_s[POOL_HALO - lag:POOL_HALO - lag + N_META, cs]
        count = jnp.minimum(pos + 1, w).astype(F32)
        d = acc / count - cur
        pooled.append(_dot(d.astype(BF16), poolw_ref[g]) * pscale_ref[layer:layer + 1, cs])
    p = jnp.concatenate(pooled, axis=1).astype(BF16)

    hn = x + _dot(a, wout_ref[0:D_ATTN, :]) + _dot(p, wout_ref[D_ATTN:, :])
    hout_ref[...] = hn
    xn_ref[...] = _rmsnorm(hn, gffn_ref[layer:layer + 1, :]).astype(xn_ref.dtype)


def _meta_call(h, layer, sinks, gmix, win, poolw, pscale, wout, gffn):
    n, d = h.shape
    whole = lambda shape: pl.BlockSpec(shape, lambda i: (0,) * len(shape))
    params = (gmix, win, poolw, pscale, wout, gffn)
    out_shapes = ((n, 2 * D_KV + D_POOL), (n, d), (n, d))
    return pl.pallas_call(
        functools.partial(_meta_kernel, layer=layer),
        grid=(1,),
        in_specs=[pl.BlockSpec(memory_space=pltpu.SMEM), whole(h.shape)]
        + [_layer_spec(w, layer) for w in params],
        out_specs=[whole(s) for s in out_shapes],
        out_shape=[jax.ShapeDtypeStruct(s, dt) for s, dt in zip(out_shapes, (F32, F32, BF16))],
        scratch_shapes=[pltpu.VMEM((POOL_HALO + N_META, D_POOL), F32)],
        compiler_params=pltpu.CompilerParams(
            dimension_semantics=("arbitrary",), vmem_limit_bytes=VMEM_LIMIT),
        name="meta_mix",
    )(sinks, h, *params)


def _ffn_kernel(tab_ref, na_ref, x_ref, *refs, masked):
    if masked:
        wg_ref, wu_ref, wd_ref, tiles_ref, out_ref = refs
        res_ref = None
    else:
        res_ref, wg_ref, wu_ref, wd_ref, out_ref = refs
    i, j = pl.program_id(0), pl.program_id(1)
    last_j = pl.num_programs(1) - 1

    def step(store_tiles):
        wg, wu, wd = (w[...].astype(BF16) for w in (wg_ref, wu_ref, wd_ref))
        valid = tab_ref[2, i]
        tm = x_ref.shape[0]
        sub = min(tm, FFN_SUBTILE)
        for r0 in range(0, tm, sub):
            as_tiles = lambda rows: rows.reshape((sub,) + tiles_ref.shape[1:])

            def part(r0=r0):
                x = x_ref[r0:r0 + sub].reshape(sub, -1)
                if masked:
                    r = r0 + lax.broadcasted_iota(jnp.int32, (sub, 1), 0)
                    x = jnp.where(r < valid, x, 0.0)
                x = x.astype(BF16)
                gate = _dot(x, wg)
                up = _dot(x, wu)
                mid = (gate * jax.nn.sigmoid(gate) * up).astype(BF16)
                y = _dot(mid, wd)
                if store_tiles:
                    tiles_ref[r0:r0 + sub] = as_tiles(out_ref[r0:r0 + sub, :] + y)
                else:
                    out_ref[r0:r0 + sub, :] += y

            if masked and r0 > 0:
                pl.when(valid > r0)(part)
                if store_tiles:
                    @pl.when(valid <= r0)
                    def _(r0=r0):
                        tiles_ref[r0:r0 + sub] = as_tiles(out_ref[r0:r0 + sub, :])
            else:
                part()

    @pl.when(i < na_ref[0])
    def _():
        @pl.when(j == 0)
        def _():
            out_ref[...] = jnp.zeros_like(out_ref) if res_ref is None else res_ref[...]

        if masked:
            pl.when(j < last_j)(lambda: step(False))
            pl.when(j == last_j)(lambda: step(True))
        else:
            step(False)


def _ffn_call(x, res, wg, wu, wd, table, n_active, tm, masked):
    rows, d = x.shape[0], wg.shape[-2]
    ff = wg.shape[-1]
    n_grid = table.shape[1]
    nj = ff // FFN_BLOCK
    assert rows % tm == 0 and nj * FFN_BLOCK == ff

    def live(i, na):
        return jnp.maximum(jnp.minimum(i, na[0] - 1), 0)

    def ff_block(i, j, na):
        return jnp.where(i < na[0], j, nj - 1)

    row = pl.BlockSpec((tm, d), lambda i, j, tab, na: (tab[1, live(i, na)], 0))
    col_w = pl.BlockSpec((None, d, FFN_BLOCK),
                         lambda i, j, tab, na: (tab[0, live(i, na)], 0, ff_block(i, j, na)))
    row_w = pl.BlockSpec((None, FFN_BLOCK, d),
                         lambda i, j, tab, na: (tab[0, live(i, na)], ff_block(i, j, na), 0))
    acts = (x,) if res is None else (x, res)
    tiles = pl.BlockSpec((tm, d // LANES, LANES),
                         lambda i, j, tab, na: (tab[1, live(i, na)], 0, 0))
    assert masked == (x.ndim == 3) == (res is None)
    return pl.pallas_call(
        functools.partial(_ffn_kernel, masked=masked),
        grid_spec=pltpu.PrefetchScalarGridSpec(
            num_scalar_prefetch=2, grid=(n_grid, nj),
            in_specs=[tiles if masked else row] + [row] * (len(acts) - 1) + [col_w, col_w, row_w],
            out_specs=tiles if masked else row,
            scratch_shapes=[pltpu.VMEM((tm, d), F32)] if masked else []),
        out_shape=jax.ShapeDtypeStruct((rows, d // LANES, LANES) if masked else (rows, d), F32),
        compiler_params=pltpu.CompilerParams(
            dimension_semantics=("arbitrary", "arbitrary"), vmem_limit_bytes=VMEM_LIMIT),
        name="ffn",
    )(table, n_active, *acts, wg, wu, wd)


def _dense_table(n_tiles, tm):
    ids = jnp.arange(n_tiles, dtype=jnp.int32)
    return (jnp.stack([jnp.zeros_like(ids), ids, jnp.full_like(ids, tm)]),
            jnp.full((1,), n_tiles, jnp.int32))


def _split_bf16(x):
    hi = x.astype(BF16)
    return hi, (x - hi.astype(F32)).astype(BF16)


def _for_each_routed_row(n_rows, pos_ref, fn):
    chunks = n_rows // LANES
    for r in range(n_rows):
        for kk in range(2):
            fn(r, kk, pos_ref[kk * chunks + r // LANES, r % LANES])


def _route_dispatch_kernel(x_ref, wr_ref, tri_ref, xs_ref, pos_ref, gate_ref, counts_ref,
                           run_s, pos_sm, xbuf, sem_pos, sem_rows, *, tr, cap, n):
    i = pl.program_id(0)
    slot = lax.rem(i, ROW_SLOTS)

    def wait_rows(s):
        for _ in range(2):
            pltpu.make_async_copy(xbuf.at[s], xs_ref.at[pl.ds(0, tr)], sem_rows.at[s]).wait()

    def issue_rows(s):
        _for_each_routed_row(tr, pos_sm.at[0], lambda r, kk, p: pltpu.make_async_copy(
            xbuf.at[s, r], xs_ref.at[p], sem_rows.at[s]).start(kk))

    @pl.when(i == 0)
    def _():
        run_s[...] = jnp.zeros_like(run_s)

    @pl.when(i >= ROW_SLOTS)
    def _():
        wait_rows(slot)

    @pl.when(i == 0)
    def _():
        _route(x_ref, wr_ref, tri_ref, pos_ref, gate_ref, counts_ref, run_s, xbuf.at[slot],
               pos_sm, sem_pos, tr, cap)

    for s in range(ROW_SLOTS):
        @pl.when((i > 0) & (slot == s))
        def _(s=s):
            issue_rows((s + ROW_SLOTS - 1) % ROW_SLOTS)
            _route(x_ref, wr_ref, tri_ref, pos_ref, gate_ref, counts_ref, run_s, xbuf.at[s],
                   pos_sm, sem_pos, tr, cap)

    @pl.when(i == n - 1)
    def _():
        issue_rows(slot)
        for back in range(min(n, ROW_SLOTS)):
            wait_rows(lax.rem(i + ROW_SLOTS - back, ROW_SLOTS))


def _route(x_ref, wr_ref, tri_ref, pos_ref, gate_ref, counts_ref, run_s, xcopy, pos_sm, sem_pos,
           tr, cap):
    xcopy[...] = x_ref[...].reshape(xcopy.shape)

    x_hi, x_lo = _split_bf16(x_ref[...])
    w_hi, w_lo = _split_bf16(wr_ref[...])
    logits = _dot_nt(w_hi, x_hi) + (_dot_nt(w_lo, x_hi) + _dot_nt(w_hi, x_lo))

    expert = lax.broadcasted_iota(jnp.int32, logits.shape, 0).astype(F32)
    none = float(N_EXPERTS)
    m1 = jnp.max(logits, axis=0, keepdims=True)
    e1 = jnp.min(jnp.where(logits == m1, expert, none), axis=0, keepdims=True)
    rest = jnp.where(expert == e1, -jnp.inf, logits)
    m2 = jnp.max(rest, axis=0, keepdims=True)
    e2 = jnp.min(jnp.where(rest == m2, expert, none), axis=0, keepdims=True)
    t = jnp.exp(m2 - m1)
    gate_ref[...] = jnp.concatenate(
        [1.0 / (1.0 + t), t / (1.0 + t), jnp.zeros((6, tr), F32)], axis=0)

    chosen = ((expert == e1) | (expert == e2)).astype(F32)
    before = _dot(chosen.astype(BF16), tri_ref[...]) + run_s[:, 0:1]
    pieces = []
    for e in (e1, e2):
        rank = jnp.sum(jnp.where(expert == e, before, 0.0), axis=0, keepdims=True)
        pos = (e * float(cap) + rank).astype(jnp.int32)
        pieces += [pos[:, c * LANES:(c + 1) * LANES] for c in range(tr // LANES)]
    pos_ref[0] = jnp.concatenate(pieces, axis=0)
    run_s[...] += jnp.sum(chosen, axis=1, keepdims=True)
    counts_ref[...] = run_s[...].astype(jnp.int32)

    to_smem = pltpu.make_async_copy(pos_ref, pos_sm, sem_pos)
    to_smem.start()
    to_smem.wait()


def _route_dispatch_call(xn, w_router):
    t, d = xn.shape
    tr = ROW_TILE
    n = t // tr
    tri = (lax.broadcasted_iota(jnp.int32, (tr, tr), 0)
           < lax.broadcasted_iota(jnp.int32, (tr, tr), 1)).astype(BF16)
    return pl.pallas_call(
        functools.partial(_route_dispatch_kernel, tr=tr, cap=t, n=n),
        grid=(n,),
        in_specs=[pl.BlockSpec((tr, d), lambda i: (i, 0)),
                  pl.BlockSpec((N_EXPERTS, d), lambda i: (0, 0)),
                  pl.BlockSpec((tr, tr), lambda i: (0, 0))],
        out_specs=[pl.BlockSpec(memory_space=pl.ANY),
                   pl.BlockSpec((1, 2 * tr // LANES, LANES), lambda i: (i, 0, 0)),
                   pl.BlockSpec((None, 8, tr), lambda i: (i, 0, 0)),
                   pl.BlockSpec((N_EXPERTS, LANES), lambda i: (0, 0))],
        out_shape=[jax.ShapeDtypeStruct((N_EXPERTS * t, d // LANES, LANES), F32),
                   jax.ShapeDtypeStruct((n, 2 * tr // LANES, LANES), jnp.int32),
                   jax.ShapeDtypeStruct((n, 8, tr), F32),
                   jax.ShapeDtypeStruct((N_EXPERTS, LANES), jnp.int32)],
        scratch_shapes=[pltpu.VMEM((N_EXPERTS, LANES), F32),
                        pltpu.SMEM((1, 2 * tr // LANES, LANES), jnp.int32),
                        pltpu.VMEM((ROW_SLOTS, tr, d // LANES, LANES), F32),
                        pltpu.SemaphoreType.DMA, pltpu.SemaphoreType.DMA((ROW_SLOTS,))],
        compiler_params=pltpu.CompilerParams(
            dimension_semantics=("arbitrary",), vmem_limit_bytes=VMEM_LIMIT),
        name="route_dispatch",
    )(xn, w_router.T, tri)


def _combine_kernel(pos_ref, pos_next_ref, h_ref, gate_ref, gfin_ref, y_ref, out_ref, buf, sem,
                    *, tc, n):
    i = pl.program_id(0)
    slot = i % 2

    def issue(p_ref, s):
        _for_each_routed_row(tc, p_ref, lambda r, kk, p: pltpu.make_async_copy(
            y_ref.at[p], buf.at[s, kk, r], sem.at[s]).start(kk))

    @pl.when(i == 0)
    def _():
        issue(pos_ref, 0)

    for s in range(2):
        @pl.when((i + 1 < n) & (slot == 1 - s))
        def _(s=s):
            issue(pos_next_ref, s)

    for kk in range(2):
        pltpu.make_async_copy(y_ref.at[pl.ds(0, tc)], buf.at[slot, kk], sem.at[slot]).wait()
    gates = gate_ref[...].T
    y0, y1 = (buf[slot, kk].reshape(tc, -1) for kk in range(2))
    hn = h_ref[...] + (gates[:, 0:1] * y0 + gates[:, 1:2] * y1)
    out_ref[...] = _rmsnorm(hn, gfin_ref[...])


def _combine_call(h, y, pos, gates, gfin):
    t, d = h.shape
    tc = ROW_TILE
    n = t // tc
    smem_pos = lambda index_map: pl.BlockSpec((None, 2 * tc // LANES, LANES), index_map,
                                              memory_space=pltpu.SMEM)
    rows = lambda width: pl.BlockSpec((tc, width), lambda i: (i, 0))
    return pl.pallas_call(
        functools.partial(_combine_kernel, tc=tc, n=n),
        grid=(n,),
        in_specs=[smem_pos(lambda i: (i, 0, 0)),
                  smem_pos(lambda i: (jnp.minimum(i + 1, n - 1), 0, 0)),
                  rows(d), pl.BlockSpec((None, 8, tc), lambda i: (i, 0, 0)),
                  pl.BlockSpec((1, d), lambda i: (0, 0)),
                  pl.BlockSpec(memory_space=pl.ANY)],
        out_specs=rows(d),
        out_shape=jax.ShapeDtypeStruct((t, d), F32),
        scratch_shapes=[pltpu.VMEM((2, 2, tc, d // LANES, LANES), F32),
                        pltpu.SemaphoreType.DMA((2,))],
        compiler_params=pltpu.CompilerParams(
            dimension_semantics=("arbitrary",), vmem_limit_bytes=VMEM_LIMIT),
        name="combine",
    )(pos, pos, h, gates, gfin, y)


def _moe(h, xn, w_router, wg, wu, wd, gfin):
    t, d = h.shape
    tm = FFN_TILE
    xs, pos, gates, counts = _route_dispatch_call(xn, w_router)
    counts = counts[:, 0]

    n_grid = 2 * t // tm + N_EXPERTS
    tiles_of = (counts + tm - 1) // tm
    tile_end = jnp.cumsum(tiles_of)
    ids = jnp.arange(n_grid, dtype=jnp.int32)[:, None]
    owner = (ids >= (tile_end - tiles_of)[None, :]) & (ids < tile_end[None, :])
    pick = lambda per_expert: jnp.sum(jnp.where(owner, per_expert[None, :], 0), axis=1)
    expert = pick(jnp.arange(N_EXPERTS, dtype=jnp.int32))
    local = ids[:, 0] - pick(tile_end - tiles_of)
    table = jnp.stack([expert, expert * (t // tm) + local,
                       jnp.clip(pick(counts) - local * tm, 0, tm)]).astype(jnp.int32)
    y = _ffn_call(xs, None, wg, wu, wd, table, tile_end[-1:].astype(jnp.int32), tm, True)
    return _combine_call(h, y, pos, gates, gfin)


def kernel(x, meta_tokens, norm_mix_g, w_in, attn_sinks, pool_w, pool_scale, w_out, norm_ffn_g,
           dense_w_gate, dense_w_up, dense_w_down, moe_w_router, moe_w_gate, moe_w_up,
           moe_w_down, final_norm_g):
    b, s, d = x.shape
    h = x.reshape(b * s, d)
    mixer = (attn_sinks, norm_mix_g, w_in.astype(BF16))
    mixer_out = (pool_w.astype(BF16), pool_scale, w_out.astype(BF16), norm_ffn_g)

    kvu0, hm, xnm = _meta_call(meta_tokens, 0, *mixer, *mixer_out)
    h, xn, *dense, moe_gate = _mix_call(
        h, b, 0, *mixer, kvu0, *mixer_out, BF16,
        (dense_w_gate, dense_w_up, dense_w_down, moe_w_gate[0]))
    hm = _ffn_call(xnm, hm, *dense, *_dense_table(1, N_META), N_META, False)
    kvu1, _, _ = _meta_call(hm, 1, *mixer, *mixer_out)
    h = _ffn_call(xn, h, *dense, *_dense_table(b * s // FFN_TILE, FFN_TILE), FFN_TILE, False)
    h, xn, moe_up, moe_down = _mix_call(
        h, b, 1, *mixer, kvu1, *mixer_out, F32, (moe_w_up[0], moe_w_down[0]))
    out = _moe(h, xn, moe_w_router[0], moe_gate, moe_up, moe_down, final_norm_g.reshape(1, -1))
    return out.reshape(b, s, d)
```
